```python
import math
import jax
import jax.numpy as jnp
from jax import lax
import numpy as np

D_MODEL = 2048
BATCH = 4
SEQ = 2048
DEPTH = 1

D_MIX = D_MODEL
D_SSM = D_MIX // 2
D_RWKV = D_MIX - D_SSM
S5_CH = 16
S5_GROUPS = D_SSM // S5_CH
S5_STATE = 64
RW_HEAD = 64
RW_HEADS = D_RWKV // RW_HEAD
RW_W_LORA = 64
RW_A_LORA = 64
RW_G_LORA = 160
RW_COLS = 3 * D_RWKV + RW_W_LORA + RW_A_LORA + RW_G_LORA
RW_SPLITS = [D_RWKV, 2 * D_RWKV, 3 * D_RWKV, 3 * D_RWKV + RW_W_LORA,
             3 * D_RWKV + RW_W_LORA + RW_A_LORA]
IN_COLS = D_SSM + RW_COLS
PEER_HEADS = 8
PEER_NKEYS = 128
PEER_EXPERTS = PEER_NKEYS * PEER_NKEYS
PEER_DKEY = 256
PEER_HALF = PEER_DKEY // 2
PEER_TOPK = 16
PEER_BLOCK = 128
DEEPNORM_ALPHA = (2.0 * DEPTH) ** 0.25
DEEPNORM_BETA = (8.0 * DEPTH) ** -0.25
LN_EPS = 1e-5
RW_GN_EPS = 64e-5

kernel_name = 'hybrid_s5_rwkv7_peer_layer'


def layer_norm(x, g, b):
    xf = x.astype(jnp.float32)
    mu = jnp.mean(xf, axis=-1, keepdims=True)
    var = jnp.mean(jnp.square(xf - mu), axis=-1, keepdims=True)
    return ((xf - mu) * lax.rsqrt(var + LN_EPS) * g + b).astype(x.dtype)


def shift_prev(p):
    return jnp.pad(p, ((0, 0), (1, 0), (0, 0)))[:, :-1]


def s5_mixer(u, lam_re, lam_im, log_step, b_re, b_im, c_re, c_im, d_skip, w_glu, b_glu):
    bsz, seq, _ = u.shape
    ug = u.reshape(bsz, seq, S5_GROUPS, S5_CH)
    step = jnp.exp(log_step)[:, None]
    mag = jnp.exp(lam_re * step)
    ang = lam_im * step
    a_re = mag * jnp.cos(ang)
    a_im = mag * jnp.sin(ang)
    den = lam_re * lam_re + lam_im * lam_im
    f_re = ((a_re - 1.0) * lam_re + a_im * lam_im) / den
    f_im = (a_im * lam_re - (a_re - 1.0) * lam_im) / den
    bb_re = f_re[..., None] * b_re - f_im[..., None] * b_im
    bb_im = f_re[..., None] * b_im + f_im[..., None] * b_re
    bu_re = jnp.einsum('blgh,gph->blgp', ug, bb_re)
    bu_im = jnp.einsum('blgh,gph->blgp', ug, bb_im)
    a_re_l = jnp.broadcast_to(a_re, (1, seq) + a_re.shape)
    a_im_l = jnp.broadcast_to(a_im, (1, seq) + a_im.shape)

    def combine(e1, e2):
        a1r, a1i, b1r, b1i = e1
        a2r, a2i, b2r, b2i = e2
        return (a2r * a1r - a2i * a1i,
                a2r * a1i + a2i * a1r,
                a2r * b1r - a2i * b1i + b2r,
                a2r * b1i + a2i * b1r + b2i)

    _, _, h_re, h_im = lax.associative_scan(combine, (a_re_l, a_im_l, bu_re, bu_im), axis=1)
    y = jnp.einsum('blgp,ghp->blgh', h_re, c_re) - jnp.einsum('blgp,ghp->blgh', h_im, c_im)
    y = y.reshape(bsz, seq, D_SSM) + d_skip * u
    y = jax.nn.gelu(y)
    return y * jax.nn.sigmoid(y @ w_glu + b_glu)


def rwkv7_mixer(p, mu, w0, w_up, a0, a_up, g_up, k_k, k_a, r_k, gn_g, gn_b):
    bsz, seq, _ = p.shape
    f32 = jnp.float32

    def heads(t):
        return t.reshape(bsz, seq, RW_HEADS, RW_HEAD)

    p = p + mu * (shift_prev(p) - p)
    r, k, v, wd, ad, gd = jnp.split(p, RW_SPLITS, axis=-1)
    w = -jax.nn.softplus(-(w0 + jnp.tanh(wd) @ w_up)) - 0.5
    decay = jnp.exp(-jnp.exp(w.astype(f32)))
    a = jax.nn.sigmoid(a0 + ad @ a_up)
    g = jax.nn.sigmoid(gd) @ g_up
    kk = heads(k * k_k).astype(f32)
    kk = kk / jnp.maximum(jnp.linalg.norm(kk, axis=-1, keepdims=True), 1e-12)
    k = k * (1.0 + (a - 1.0) * k_a)
    rh, kh, vh, ah, wh = (heads(t).astype(f32) for t in (r, k, v, a, decay))
    bh = kk * ah
    xs = tuple(jnp.moveaxis(t, 1, 0) for t in (rh, wh, kh, vh, kk, bh))

    def step(state, inp):
        r_t, w_t, k_t, v_t, kk_t, b_t = inp
        sa = jnp.einsum('bhij,bhj->bhi', state, kk_t)
        state = (state * w_t[:, :, None, :] - sa[..., None] * b_t[:, :, None, :]
                 + v_t[..., None] * k_t[:, :, None, :])
        return state, jnp.einsum('bhij,bhj->bhi', state, r_t)

    s0 = jnp.zeros((bsz, RW_HEADS, RW_HEAD, RW_HEAD), f32)
    _, o = lax.scan(step, s0, xs)
    o = jnp.moveaxis(o, 0, 1)
    mean = jnp.mean(o, axis=-1, keepdims=True)
    var = jnp.mean(jnp.square(o - mean), axis=-1, keepdims=True)
    o = (o - mean) * lax.rsqrt(var + RW_GN_EPS)
    bonus = jnp.sum(rh * kh * r_k, axis=-1, keepdims=True) * vh
    o = o.reshape(bsz, seq, D_RWKV) * gn_g + gn_b + bonus.reshape(bsz, seq, D_RWKV)
    return (o * g).astype(p.dtype)


def peer_channel_mixer(x, w_q, keys1, keys2, u_tab, v_tab):
    bsz, seq, dim = x.shape
    n_tok = bsz * seq
    f32 = jnp.float32
    xt = x.reshape(n_tok, dim)
    q = (xt @ w_q).reshape(n_tok, PEER_HEADS, 2, PEER_HALF)
    s1 = jnp.einsum('thd,hkd->thk', q[:, :, 0], keys1).astype(f32)
    s2 = jnp.einsum('thd,hkd->thk', q[:, :, 1], keys2).astype(f32)
    v1, i1 = lax.top_k(s1, PEER_TOPK)
    v2, i2 = lax.top_k(s2, PEER_TOPK)
    cand = (v1[..., :, None] + v2[..., None, :]).reshape(n_tok, PEER_HEADS, PEER_TOPK * PEER_TOPK)
    cidx = (i1[..., :, None] * PEER_NKEYS + i2[..., None, :]).reshape(n_tok, PEER_HEADS, PEER_TOPK * PEER_TOPK)
    top_s, pos = lax.top_k(cand, PEER_TOPK)
    idx = jnp.take_along_axis(cidx, pos, axis=-1)
    gate = jax.nn.softmax(top_s, axis=-1).astype(x.dtype)
    n_blk = n_tok // PEER_BLOCK
    n_sel = PEER_HEADS * PEER_TOPK
    xb = xt.reshape(n_blk, PEER_BLOCK, dim)
    ib = idx.reshape(n_blk, PEER_BLOCK, n_sel)
    gb = gate.reshape(n_blk, PEER_BLOCK, n_sel)

    def block(args):
        xs, ids, gs = args
        u = jnp.take(u_tab, ids, axis=0)
        v = jnp.take(v_tab, ids, axis=0)
        act = jax.nn.gelu(jnp.einsum('td,ted->te', xs, u))
        return jnp.einsum('te,ted->td', gs * act, v)

    out = lax.map(block, (xb, ib, gb))
    return out.reshape(bsz, seq, dim)


def setup_inputs(seed: int = 0) -> dict:
    key = jax.random.key(seed)
    it = iter(jax.random.split(key, 40))
    f32 = jnp.float32

    def nrm(shape, scale):
        return scale * jax.random.normal(next(it), shape, f32)

    nl = DEPTH
    g_, p_, h_ = S5_GROUPS, S5_STATE, S5_CH
    x = nrm((BATCH, SEQ, D_MODEL), 1.0)
    w_in = nrm((nl, D_MODEL, IN_COLS), D_MODEL ** -0.5)
    s5_lam_re = -0.5 + nrm((nl, g_, p_), 0.01)
    s5_lam_im = jnp.pi * jnp.arange(p_, dtype=f32) + nrm((nl, g_, p_), 0.01)
    s5_log_step = jax.random.uniform(next(it), (nl, g_), f32, math.log(1e-3), math.log(1e-1))
    s5_b_re = nrm((nl, g_, p_, h_), (2.0 * h_) ** -0.5)
    s5_b_im = nrm((nl, g_, p_, h_), (2.0 * h_) ** -0.5)
    s5_c_re = nrm((nl, g_, h_, p_), (2.0 * p_) ** -0.5)
    s5_c_im = nrm((nl, g_, h_, p_), (2.0 * p_) ** -0.5)
    s5_d = nrm((nl, D_SSM), 1.0)
    s5_w_glu = nrm((nl, D_SSM, D_SSM), D_SSM ** -0.5)
    s5_b_glu = nrm((nl, D_SSM), 0.02)
    rw_mu = jax.random.uniform(next(it), (nl, RW_COLS), f32, 0.0, 1.0)
    ramp = (jnp.arange(D_RWKV, dtype=f32) / (D_RWKV - 1)) ** 0.7
    rw_w0 = -6.0 + 5.0 * ramp + nrm((nl, D_RWKV), 0.1)
    rw_w_up = nrm((nl, RW_W_LORA, D_RWKV), 0.1 * RW_W_LORA ** -0.5)
    rw_a0 = nrm((nl, D_RWKV), 0.1)
    rw_a_up = nrm((nl, RW_A_LORA, D_RWKV), 0.1 * RW_A_LORA ** -0.5)
    rw_g_up = nrm((nl, RW_G_LORA, D_RWKV), RW_G_LORA ** -0.5)
    rw_k_k = 0.85 + nrm((nl, D_RWKV), 0.02)
    rw_k_a = 1.0 + nrm((nl, D_RWKV), 0.02)
    rw_r_k = nrm((nl, RW_HEADS, RW_HEAD), 0.1)
    rw_gn_g = 1.0 + nrm((nl, D_RWKV), 0.02)
    rw_gn_b = nrm((nl, D_RWKV), 0.02)
    w_out = nrm((nl, D_MIX, D_MODEL), DEEPNORM_BETA * D_MIX ** -0.5)
    ln1_g = 1.0 + nrm((nl, D_MODEL), 0.02)
    ln1_b = nrm((nl, D_MODEL), 0.02)
    peer_w_q = nrm((nl, D_MODEL, PEER_HEADS * PEER_DKEY), D_MODEL ** -0.5)
    peer_keys1 = nrm((nl, PEER_HEADS, PEER_NKEYS, PEER_HALF), PEER_HALF ** -0.5)
    peer_keys2 = nrm((nl, PEER_HEADS, PEER_NKEYS, PEER_HALF), PEER_HALF ** -0.5)
    peer_u = nrm((nl, PEER_EXPERTS, D_MODEL), D_MODEL ** -0.5)
    peer_v = nrm((nl, PEER_EXPERTS, D_MODEL), DEEPNORM_BETA)
    ln2_g = 1.0 + nrm((nl, D_MODEL), 0.02)
    ln2_b = nrm((nl, D_MODEL), 0.02)
    return {'x': x, 'w_in': w_in,
            's5_lam_re': s5_lam_re, 's5_lam_im': s5_lam_im, 's5_log_step': s5_log_step,
            's5_b_re': s5_b_re, 's5_b_im': s5_b_im, 's5_c_re': s5_c_re, 's5_c_im': s5_c_im,
            's5_d': s5_d, 's5_w_glu': s5_w_glu, 's5_b_glu': s5_b_glu,
            'rw_mu': rw_mu, 'rw_w0': rw_w0, 'rw_w_up': rw_w_up, 'rw_a0': rw_a0,
            'rw_a_up': rw_a_up, 'rw_g_up': rw_g_up, 'rw_k_k': rw_k_k, 'rw_k_a': rw_k_a,
            'rw_r_k': rw_r_k, 'rw_gn_g': rw_gn_g, 'rw_gn_b': rw_gn_b,
            'w_out': w_out, 'ln1_g': ln1_g, 'ln1_b': ln1_b,
            'peer_w_q': peer_w_q, 'peer_keys1': peer_keys1, 'peer_keys2': peer_keys2,
            'peer_u': peer_u, 'peer_v': peer_v, 'ln2_g': ln2_g, 'ln2_b': ln2_b}


def reference(x, w_in,
              s5_lam_re, s5_lam_im, s5_log_step, s5_b_re, s5_b_im, s5_c_re, s5_c_im,
              s5_d, s5_w_glu, s5_b_glu,
              rw_mu, rw_w0, rw_w_up, rw_a0, rw_a_up, rw_g_up, rw_k_k, rw_k_a,
              rw_r_k, rw_gn_g, rw_gn_b,
              w_out, ln1_g, ln1_b,
              peer_w_q, peer_keys1, peer_keys2, peer_u, peer_v, ln2_g, ln2_b):
    h = x
    for l in range(DEPTH):
        proj = h @ w_in[l]
        y_ssm = s5_mixer(proj[..., :D_SSM], s5_lam_re[l], s5_lam_im[l], s5_log_step[l],
                         s5_b_re[l], s5_b_im[l], s5_c_re[l], s5_c_im[l],
                         s5_d[l], s5_w_glu[l], s5_b_glu[l])
        y_rw = rwkv7_mixer(proj[..., D_SSM:], rw_mu[l], rw_w0[l], rw_w_up[l], rw_a0[l],
                           rw_a_up[l], rw_g_up[l], rw_k_k[l], rw_k_a[l], rw_r_k[l],
                           rw_gn_g[l], rw_gn_b[l])
        mix = jnp.concatenate([y_ssm, y_rw], axis=-1) @ w_out[l]
        h = layer_norm(DEEPNORM_ALPHA * h + mix, ln1_g[l], ln1_b[l])
        ffn = peer_channel_mixer(h, peer_w_q[l], peer_keys1[l], peer_keys2[l],
                                 peer_u[l], peer_v[l])
        h = layer_norm(DEEPNORM_ALPHA * h + ffn, ln2_g[l], ln2_b[l])
    return h
```

```python
import functools
import math

import jax
import jax.numpy as jnp
from jax import lax
from jax.experimental import pallas as pl
from jax.experimental.pallas import tpu as pltpu

F32 = jnp.float32
BF16 = jnp.bfloat16

D_MODEL = 2048
D_SSM = 1024
D_RWKV = 1024
S5_CH = 16
S5_GROUPS = D_SSM // S5_CH
S5_STATE = 64
S5_CHUNK = 16
RW_HEAD = 64
RW_HEADS = D_RWKV // RW_HEAD
RW_W_LORA = 64
RW_A_LORA = 64
RW_G_LORA = 160
RW_LORA = RW_W_LORA + RW_A_LORA + RW_G_LORA
RW_LORA_PAD = 384
RW_CHUNK = 64
PEER_HEADS = 8
PEER_NKEYS = 128
PEER_EXPERTS = PEER_NKEYS * PEER_NKEYS
PEER_HALF = 128
PEER_TOPK = 16
DEPTH = 1
DEEPNORM_ALPHA = (2.0 * DEPTH) ** 0.25
LN_EPS = 1e-5
RW_GN_EPS = 64e-5
NEG_INF = float("-inf")

VMEM_LIMIT = 56 * 1024 * 1024


def _cparams(sem):
    return pltpu.CompilerParams(dimension_semantics=sem, vmem_limit_bytes=VMEM_LIMIT)


def _dot(a, b):
    return jnp.dot(a.astype(BF16), b.astype(BF16), preferred_element_type=F32)


def _dot_nt(a, b):
    return lax.dot_general(a.astype(BF16), b.astype(BF16), (((1,), (1,)), ((), ())),
                           preferred_element_type=F32)


def _dot_tn(a, b):
    return lax.dot_general(a.astype(BF16), b.astype(BF16), (((0,), (0,)), ((), ())),
                           preferred_element_type=F32)


def _gelu(x):
    c = math.sqrt(2.0 / math.pi)
    return 0.5 * x * (1.0 + jnp.tanh(c * (x + 0.044715 * (x * x * x))))


def _sigmoid(x):
    return 1.0 / (1.0 + jnp.exp(-x))


def _mm_kernel(a_ref, b_ref, o_ref):
    o_ref[...] = jnp.dot(a_ref[...], b_ref[...],
                         preferred_element_type=F32).astype(o_ref.dtype)


def _matmul(a, b, out_dtype, tm, tn):
    m, k = a.shape
    n = b.shape[1]
    tm = min(tm, m)
    return pl.pallas_call(
        _mm_kernel,
        grid=(m // tm, n // tn),
        in_specs=[pl.BlockSpec((tm, k), lambda i, j: (i, 0)),
                  pl.BlockSpec((k, tn), lambda i, j: (0, j))],
        out_specs=pl.BlockSpec((tm, tn), lambda i, j: (i, j)),
        out_shape=jax.ShapeDtypeStruct((m, n), out_dtype),
        compiler_params=_cparams(("parallel", "parallel")),
    )(a, b)


def _s5_kernel(u_ref, mt_ref, bc_ref, cc_ref, ar_ref, ai_ref, d_ref, o_ref, *, nc):
    u = u_ref[0]
    rows = u.shape[0]
    y = jnp.dot(u, mt_ref[0], preferred_element_type=F32)
    s = jnp.dot(u, bc_ref[0], preferred_element_type=F32)
    cidx = lax.broadcasted_iota(jnp.int32, (rows, 2 * S5_STATE), 0) & (nc - 1)
    d, k = 1, 0
    while d < nc:
        sh = jnp.where(cidx >= d, pltpu.roll(s, d, axis=0), 0.0)
        shs = pltpu.roll(sh, S5_STATE, axis=1)
        s = s + sh * ar_ref[0, k:k + 1, :] + shs * ai_ref[0, k:k + 1, :]
        d, k = d * 2, k + 1
    sp = jnp.where(cidx >= 1, pltpu.roll(s, 1, axis=0), 0.0)
    y = y + jnp.dot(sp.astype(BF16), cc_ref[0], preferred_element_type=F32)
    y = y + d_ref[0] * u.astype(F32)
    o_ref[0] = _gelu(y).astype(o_ref.dtype)


def _s5_params(lam_re, lam_im, log_step, b_re, b_im, c_re, c_im, d_skip, nc):
    hi = lax.Precision.HIGHEST
    c = S5_CHUNK
    step = jnp.exp(log_step)[:, None]
    a_re = jnp.exp(lam_re * step) * jnp.cos(lam_im * step)
    a_im = jnp.exp(lam_re * step) * jnp.sin(lam_im * step)
    den = lam_re * lam_re + lam_im * lam_im
    f_re = ((a_re - 1.0) * lam_re + a_im * lam_im) / den
    f_im = (a_im * lam_re - (a_re - 1.0) * lam_im) / den
    bb_re = f_re[..., None] * b_re - f_im[..., None] * b_im
    bb_im = f_re[..., None] * b_im + f_im[..., None] * b_re

    def power(j):
        jj = j[None, :, None]
        mag = jnp.exp(lam_re[:, None, :] * step[:, None, :] * jj)
        ang = lam_im[:, None, :] * step[:, None, :] * jj
        return mag * jnp.cos(ang), mag * jnp.sin(ang)

    pw_re, pw_im = power(jnp.arange(c + 1, dtype=F32))
    ce_re = c_re[:, None] * pw_re[:, :, None, :] - c_im[:, None] * pw_im[:, :, None, :]
    ce_im = c_re[:, None] * pw_im[:, :, None, :] + c_im[:, None] * pw_re[:, :, None, :]
    kern = (jnp.einsum('gjop,gpi->gjoi', ce_re[:, :c], bb_re, precision=hi)
            - jnp.einsum('gjop,gpi->gjoi', ce_im[:, :c], bb_im, precision=hi))
    t_idx = jnp.arange(c)
    lag = t_idx[None, :] - t_idx[:, None]
    kt = jnp.take(kern, jnp.clip(lag, 0, c - 1), axis=1)
    kt = jnp.where((lag >= 0)[None, :, :, None, None], kt, 0.0)
    g = lam_re.shape[0]
    mt = jnp.transpose(kt, (0, 1, 4, 2, 3)).reshape(g, c * S5_CH, c * S5_CH)
    rv_re, rv_im = pw_re[:, c - 1::-1][:, :c], pw_im[:, c - 1::-1][:, :c]
    bc_r = rv_re[:, :, None, :] * jnp.transpose(bb_re, (0, 2, 1))[:, None] \
        - rv_im[:, :, None, :] * jnp.transpose(bb_im, (0, 2, 1))[:, None]
    bc_i = rv_re[:, :, None, :] * jnp.transpose(bb_im, (0, 2, 1))[:, None] \
        + rv_im[:, :, None, :] * jnp.transpose(bb_re, (0, 2, 1))[:, None]
    bc = jnp.concatenate([bc_r, bc_i], axis=-1).reshape(g, c * S5_CH, 2 * S5_STATE)
    cc_r = jnp.transpose(ce_re[:, 1:], (0, 3, 1, 2)).reshape(g, S5_STATE, c * S5_CH)
    cc_i = -jnp.transpose(ce_im[:, 1:], (0, 3, 1, 2)).reshape(g, S5_STATE, c * S5_CH)
    cc = jnp.concatenate([cc_r, cc_i], axis=1)
    nlog = max(1, int(math.ceil(math.log2(nc))))
    sc_re, sc_im = power(c * (2.0 ** jnp.arange(8, dtype=F32)))
    ar = jnp.concatenate([sc_re, sc_re], axis=-1)
    ai = jnp.concatenate([-sc_im, sc_im], axis=-1)
    del nlog
    dflat = jnp.tile(d_skip.reshape(g, 1, S5_CH), (1, c, 1)).reshape(g, 1, c * S5_CH)
    return mt.astype(BF16), bc.astype(BF16), cc.astype(BF16), ar, ai, dflat


def _s5_scan(u_tok, params, bsz, seq):
    mt, bc, cc, ar, ai, dflat = params
    nc = seq // S5_CHUNK
    g = S5_GROUPS
    rows = bsz * nc
    w = S5_CHUNK * S5_CH
    uf = u_tok.reshape(bsz, nc, S5_CHUNK, g, S5_CH)
    uf = jnp.transpose(uf, (3, 0, 1, 2, 4)).reshape(g, rows, w)
    spec3 = lambda a, b: pl.BlockSpec((1, a, b), lambda i: (i, 0, 0))
    yf = pl.pallas_call(
        functools.partial(_s5_kernel, nc=nc),
        grid=(g,),
        in_specs=[spec3(rows, w), spec3(w, w), spec3(w, 2 * S5_STATE), spec3(2 * S5_STATE, w),
                  spec3(8, 2 * S5_STATE), spec3(8, 2 * S5_STATE), spec3(1, w)],
        out_specs=spec3(rows, w),
        out_shape=jax.ShapeDtypeStruct((g, rows, w), BF16),
        compiler_params=_cparams(("parallel",)),
    )(uf, mt, bc, cc, ar, ai, dflat)
    yf = yf.reshape(g, bsz, nc, S5_CHUNK, S5_CH)
    return jnp.transpose(yf, (1, 2, 3, 0, 4)).reshape(bsz * seq, D_SSM)


def _glu_kernel(y_ref, w_ref, b_ref, o_ref):
    y = y_ref[...]
    z = jnp.dot(y, w_ref[...], preferred_element_type=F32) + b_ref[...]
    o_ref[...] = (y.astype(F32) * _sigmoid(z)).astype(o_ref.dtype)


def _glu(y, w, b, tm=512):
    m, n = y.shape
    tm = min(tm, m)
    return pl.pallas_call(
        _glu_kernel,
        grid=(m // tm,),
        in_specs=[pl.BlockSpec((tm, n), lambda i: (i, 0)),
                  pl.BlockSpec((n, n), lambda i: (0, 0)),
                  pl.BlockSpec((1, n), lambda i: (0, 0))],
        out_specs=pl.BlockSpec((tm, n), lambda i: (i, 0)),
        out_shape=jax.ShapeDtypeStruct((m, n), BF16),
        compiler_params=_cparams(("parallel",)),
    )(y, w, b)


def _rw_prep_kernel(p_ref, pp_ref, l_ref, lp_ref, mu_ref, mul_ref, w0_ref, a0_ref,
                    kk_ref, ka_ref, wup_ref, aup_ref, gup_ref,
                    r_ref, k_ref, v_ref, kkr_ref, a_ref, lw_ref, g_ref, *, tiles_per_seq):
    first = (pl.program_id(0) % tiles_per_seq) == 0

    def shifted(cur_ref, prev_ref, mu):
        cur = cur_ref[...]
        prev_row = jnp.where(first, 0.0, prev_ref[7:8, :])
        row = lax.broadcasted_iota(jnp.int32, cur.shape, 0)
        prev = jnp.where(row == 0, prev_row, pltpu.roll(cur, 1, axis=0))
        return cur + mu * (prev - cur)

    p = shifted(p_ref, pp_ref, mu_ref[...])
    lo = shifted(l_ref, lp_ref, mul_ref[...])
    r = p[:, :D_RWKV]
    k = p[:, D_RWKV:2 * D_RWKV]
    v = p[:, 2 * D_RWKV:]
    w_pre = w0_ref[...] + _dot(jnp.tanh(lo), wup_ref[...])
    a = _sigmoid(a0_ref[...] + _dot(lo, aup_ref[...]))
    g = _dot(_sigmoid(lo), gup_ref[...])
    z = -w_pre
    softplus = jnp.maximum(z, 0.0) + jnp.log(1.0 + jnp.exp(-jnp.abs(z)))
    w = -softplus - 0.5
    r_ref[...] = r
    k_ref[...] = k * (1.0 + (a - 1.0) * ka_ref[...])
    v_ref[...] = v
    kkr_ref[...] = k * kk_ref[...]
    a_ref[...] = a
    lw_ref[...] = -jnp.exp(w)
    g_ref[...] = g


def _rw_prep(p_rkv, p_lora, mu_rkv, mu_lora, w0, a0, k_k, k_a, wup, aup, gup, seq, tm=256):
    t = p_rkv.shape[0]
    tm = min(tm, seq)
    n3 = 3 * D_RWKV
    row = lambda n: pl.BlockSpec((1, n), lambda i: (0, 0))
    full = lambda a, b: pl.BlockSpec((a, b), lambda i: (0, 0))
    prev = lambda n: pl.BlockSpec((8, n), lambda i: (jnp.maximum(i * (tm // 8) - 1, 0), 0))
    out = jax.ShapeDtypeStruct((t, D_RWKV), F32)
    ospec = pl.BlockSpec((tm, D_RWKV), lambda i: (i, 0))
    return pl.pallas_call(
        functools.partial(_rw_prep_kernel, tiles_per_seq=seq // tm),
        grid=(t // tm,),
        in_specs=[pl.BlockSpec((tm, n3), lambda i: (i, 0)), prev(n3),
                  pl.BlockSpec((tm, RW_LORA_PAD), lambda i: (i, 0)), prev(RW_LORA_PAD),
                  row(n3), row(RW_LORA_PAD), row(D_RWKV), row(D_RWKV), row(D_RWKV), row(D_RWKV),
                  full(RW_LORA_PAD, D_RWKV), full(RW_LORA_PAD, D_RWKV), full(RW_LORA_PAD, D_RWKV)],
        out_specs=[ospec] * 7,
        out_shape=[out] * 7,
        compiler_params=_cparams(("parallel",)),
    )(p_rkv, p_rkv, p_lora, p_lora, mu_rkv, mu_lora, w0, a0, k_k, k_a, wup, aup, gup)


def _rw_chunk_kernel(r_ref, k_ref, v_ref, kkr_ref, a_ref, lw_ref, rk_ref,
                     q_ref, oi_ref, bonus_ref, m_ref, n_ref):
    c = RW_CHUNK
    hd = RW_HEAD
    ri = lax.broadcasted_iota(jnp.int32, (c, c), 0)
    ci = lax.broadcasted_iota(jnp.int32, (c, c), 1)
    tri_incl = (ci <= ri)
    tri_strict = (ci < ri)
    ltri = tri_incl.astype(BF16)
    eye = (ci == ri).astype(F32)
    qs, ois, bos = [], [], []
    for h in range(r_ref.shape[1] // hd):
        sl = slice(h * hd, (h + 1) * hd)
        r = r_ref[:, sl]
        k = k_ref[:, sl]
        v = v_ref[:, sl]
        kkr = kkr_ref[:, sl]
        a = a_ref[:, sl]
        lw = lw_ref[:, sl]
        lw_hi = lw.astype(BF16)
        lw_lo = (lw - lw_hi.astype(F32)).astype(BF16)
        cl = (jnp.dot(ltri, lw_hi, preferred_element_type=F32)
              + jnp.dot(ltri, lw_lo, preferred_element_type=F32))
        cl_prev = cl - lw
        cl_end = cl[c - 1:c, :]
        e_pos = jnp.exp(cl)
        e_neg = jnp.exp(-cl)
        e_prev = jnp.exp(cl_prev)
        e_end = jnp.exp(cl_end - cl)
        nrm = jnp.sqrt(jnp.sum(kkr * kkr, axis=-1, keepdims=True))
        kk = kkr / jnp.maximum(nrm, 1e-12)
        b = kk * a
        rt = r * e_pos
        kt = k * e_neg
        bt = b * e_neg
        kkt = kk * e_prev
        a4 = _dot_nt(jnp.concatenate([kkt, rt], axis=0), jnp.concatenate([kt, bt], axis=0))
        a_kk = jnp.where(tri_strict, a4[:c, :c], 0.0)
        a_kb = jnp.where(tri_strict, a4[:c, c:], 0.0)
        a_rk = jnp.where(tri_incl, a4[c:, :c], 0.0)
        a_rb = jnp.where(tri_incl, a4[c:, c:], 0.0)
        x = eye - a_kb
        pw = _dot(a_kb, a_kb)
        n_sq = int(math.log2(c)) - 1
        for it in range(n_sq):
            if it + 1 < n_sq:
                xp = _dot(jnp.concatenate([x, pw], axis=0), pw)
                x = x + xp[:c]
                pw = xp[c:]
            else:
                x = x + _dot(x, pw)
        av = _dot(jnp.concatenate([a_kk, a_rk], axis=0), v)
        wu = _dot(x, jnp.concatenate([kkt, av[:c]], axis=1))
        rb = _dot(a_rb, wu)
        qs.append(rt - rb[:, :hd])
        ois.append(av[c:] - rb[:, hd:])
        kte = k * e_end
        bte = b * e_end
        wub = _dot_tn(wu, bte)
        ji = lax.broadcasted_iota(jnp.int32, (hd, hd), 0)
        jo = lax.broadcasted_iota(jnp.int32, (hd, hd), 1)
        gam = jnp.where(ji == jo, jnp.exp(cl_end), 0.0)
        m_ref[0, h] = gam - wub[:hd]
        n_ref[0, h] = _dot_tn(v, kte) - wub[hd:]
        bos.append(jnp.sum(r * k * rk_ref[0:1, sl], axis=-1, keepdims=True) * v)
    q_ref[...] = jnp.concatenate(qs, axis=1)
    oi_ref[...] = jnp.concatenate(ois, axis=1)
    bonus_ref[...] = jnp.concatenate(bos, axis=1)


def _rw_chunks(r, k, v, kkr, a, lw, r_k, hp=2):
    t = r.shape[0]
    c = RW_CHUNK
    nch = t // c
    wdt = hp * RW_HEAD
    blk = pl.BlockSpec((c, wdt), lambda i, j: (i, j))
    mat = pl.BlockSpec((1, hp, RW_HEAD, RW_HEAD), lambda i, j: (i, j, 0, 0))
    tok = jax.ShapeDtypeStruct((t, D_RWKV), F32)
    mshape = jax.ShapeDtypeStruct((nch, RW_HEADS, RW_HEAD, RW_HEAD), F32)
    return pl.pallas_call(
        _rw_chunk_kernel,
        grid=(nch, RW_HEADS // hp),
        in_specs=[blk] * 6 + [pl.BlockSpec((1, wdt), lambda i, j: (0, j))],
        out_specs=[blk, blk, blk, mat, mat],
        out_shape=[tok, tok, tok, mshape, mshape],
        compiler_params=_cparams(("parallel", "parallel")),
    )(r, k, v, kkr, a, lw, r_k)


def _rw_seq_kernel(q_ref, oi_ref, bonus_ref, g_ref, m_ref, n_ref, gg_ref, gb_ref,
                   o_ref, st_ref):
    @pl.when(pl.program_id(0) == 0)
    def _():
        st_ref[...] = jnp.zeros_like(st_ref)

    hd = RW_HEAD

    def per_batch(b, carry):
        for hp in range(RW_HEADS // 2):
            outs = []
            for h in (2 * hp, 2 * hp + 1):
                sl = slice(h * hd, (h + 1) * hd)
                s = st_ref[b, h]
                o = _dot_nt(q_ref[b, :, sl], s) + oi_ref[b, :, sl]
                st_ref[b, h] = _dot(s, m_ref[b, 0, h]) + n_ref[b, 0, h]
                mean = jnp.mean(o, axis=-1, keepdims=True)
                var = jnp.mean(jnp.square(o - mean), axis=-1, keepdims=True)
                o = (o - mean) * lax.rsqrt(var + RW_GN_EPS)
                o = o * gg_ref[0:1, sl] + gb_ref[0:1, sl] + bonus_ref[b, :, sl]
                outs.append(o * g_ref[b, :, sl])
            o_ref[b, :, 2 * hp * hd:(2 * hp + 2) * hd] = jnp.concatenate(outs, axis=1).astype(o_ref.dtype)
        return carry

    lax.fori_loop(0, q_ref.shape[0], per_batch, 0)


def _rw_seq(q, oi, bonus, g, m, n, gn_g, gn_b, bsz, seq):
    c = RW_CHUNK
    nch = seq // c
    r3 = lambda x: x.reshape(bsz, seq, D_RWKV)
    r5 = lambda x: x.reshape(bsz, nch, RW_HEADS, RW_HEAD, RW_HEAD)
    tok = pl.BlockSpec((bsz, c, D_RWKV), lambda i: (0, i, 0))
    mat = pl.BlockSpec((bsz, 1, RW_HEADS, RW_HEAD, RW_HEAD), lambda i: (0, i, 0, 0, 0))
    row = pl.BlockSpec((1, D_RWKV), lambda i: (0, 0))
    y = pl.pallas_call(
        _rw_seq_kernel,
        grid=(nch,),
        in_specs=[tok, tok, tok, tok, mat, mat, row, row],
        out_specs=tok,
        out_shape=jax.ShapeDtypeStruct((bsz, seq, D_RWKV), BF16),
        scratch_shapes=[pltpu.VMEM((bsz, RW_HEADS, RW_HEAD, RW_HEAD), F32)],
        compiler_params=_cparams(("arbitrary",)),
    )(r3(q), r3(oi), r3(bonus), r3(g), r5(m), r5(n), gn_g, gn_b)
    return y.reshape(bsz * seq, D_RWKV)


def _layer_norm(x, g, b):
    mu = jnp.mean(x, axis=-1, keepdims=True)
    var = jnp.mean(jnp.square(x - mu), axis=-1, keepdims=True)
    return (x - mu) * lax.rsqrt(var + LN_EPS) * g + b


def _outproj_kernel(ys_ref, yr_ref, w1_ref, w2_ref, x_ref, g_ref, b_ref, h_ref, hb_ref):
    mix = (jnp.dot(ys_ref[...], w1_ref[...], preferred_element_type=F32)
           + jnp.dot(yr_ref[...], w2_ref[...], preferred_element_type=F32))
    h = _layer_norm(DEEPNORM_ALPHA * x_ref[...] + mix, g_ref[...], b_ref[...])
    h_ref[...] = h
    hb_ref[...] = h.astype(BF16)


def _outproj(ys, yr, w_out, x, g, b, tm=256):
    t = x.shape[0]
    tm = min(tm, t)
    half = pl.BlockSpec((tm, D_SSM), lambda i: (i, 0))
    full = pl.BlockSpec((tm, D_MODEL), lambda i: (i, 0))
    row = pl.BlockSpec((1, D_MODEL), lambda i: (0, 0))
    return pl.pallas_call(
        _outproj_kernel,
        grid=(t // tm,),
        in_specs=[half, half,
                  pl.BlockSpec((D_SSM, D_MODEL), lambda i: (0, 0)),
                  pl.BlockSpec((D_RWKV, D_MODEL), lambda i: (1, 0)),
                  full, row, row],
        out_specs=[full, full],
        out_shape=[jax.ShapeDtypeStruct((t, D_MODEL), F32),
                   jax.ShapeDtypeStruct((t, D_MODEL), BF16)],
        compiler_params=_cparams(("parallel",)),
    )(ys, yr, w_out, w_out, x, g, b)


def _peer_scores_kernel(h_ref, wq_ref, keys_ref, s_ref):
    q = jnp.dot(h_ref[...], wq_ref[...], preferred_element_type=F32)
    for blk in range(2 * PEER_HEADS):
        qb = q[:, blk * PEER_HALF:(blk + 1) * PEER_HALF]
        s_ref[blk] = _dot_nt(keys_ref[blk], qb)


def _peer_scores(hb, wq, keys, tm=256):
    t = hb.shape[0]
    tm = min(tm, t)
    nb = 2 * PEER_HEADS
    return pl.pallas_call(
        _peer_scores_kernel,
        grid=(t // tm,),
        in_specs=[pl.BlockSpec((tm, D_MODEL), lambda i: (i, 0)),
                  pl.BlockSpec((D_MODEL, D_MODEL), lambda i: (0, 0)),
                  pl.BlockSpec((nb, PEER_NKEYS, PEER_HALF), lambda i: (0, 0, 0))],
        out_specs=pl.BlockSpec((nb, PEER_NKEYS, tm), lambda i: (0, 0, i)),
        out_shape=jax.ShapeDtypeStruct((nb, PEER_NKEYS, t), F32),
        compiler_params=_cparams(("parallel",)),
    )(hb, wq, keys)


def _extract_top(x, n):
    vals = []
    for _ in range(n):
        m = jnp.max(x, axis=0, keepdims=True)
        vals.append(m)
        x = jnp.where(x == m, NEG_INF, x)
    return vals


def _peer_topk_kernel(s_ref, tau_ref, e1_ref, e2_ref, top_ref):
    k = PEER_TOPK

    def per_block(blk, carry):
        top_ref[blk] = jnp.concatenate(_extract_top(s_ref[blk], k), axis=0)
        return carry

    lax.fori_loop(0, 2 * PEER_HEADS, per_block, 0)

    def per_head(h, carry):
        ta = top_ref[2 * h]
        tb = top_ref[2 * h + 1]
        cands = [ta[i:i + 1, :] + tb[0:k // (i + 1), :] for i in range(k)]
        n_c = sum(k // (i + 1) for i in range(k))
        pad = (-n_c) % 8
        if pad:
            cands.append(jnp.full((pad, ta.shape[1]), NEG_INF, F32))
        best = _extract_top(jnp.concatenate(cands, axis=0), k)
        m0 = best[0]
        z = jnp.zeros_like(m0)
        for bv in best:
            z = z + jnp.exp(bv - m0)
        tau_ref[h] = best[k - 1]
        e1_ref[h] = jnp.exp(s_ref[2 * h] - ta[0:1, :]) / z
        e2_ref[h] = jnp.exp(s_ref[2 * h + 1] - tb[0:1, :])
        return carry

    lax.fori_loop(0, PEER_HEADS, per_head, 0)


def _peer_topk(scores, tt=256):
    nb, nk, t = scores.shape
    tt = min(tt, t)
    big = pl.BlockSpec((PEER_HEADS, nk, tt), lambda i: (0, 0, i))
    return pl.pallas_call(
        _peer_topk_kernel,
        grid=(t // tt,),
        in_specs=[pl.BlockSpec((nb, nk, tt), lambda i: (0, 0, i))],
        out_specs=[pl.BlockSpec((PEER_HEADS, 1, tt), lambda i: (0, 0, i)), big, big],
        out_shape=[jax.ShapeDtypeStruct((PEER_HEADS, 1, t), F32),
                   jax.ShapeDtypeStruct((PEER_HEADS, nk, t), F32),
                   jax.ShapeDtypeStruct((PEER_HEADS, nk, t), F32)],
        scratch_shapes=[pltpu.VMEM((nb, PEER_TOPK, tt), F32)],
        compiler_params=_cparams(("parallel",)),
    )(scores)


def _peer_dense_kernel(ht_ref, u_ref, v_ref, s1_ref, s2_ref, e1_ref, e2_ref, tau_ref,
                       o_ref, hs_ref, *, q):
    @pl.when(pl.program_id(1) == 0)
    def _():
        o_ref[...] = jnp.zeros_like(o_ref)

    nk = PEER_NKEYS
    tm = ht_ref.shape[1]
    ht = ht_ref[...]
    for qi in range(q):
        act = jnp.dot(u_ref[qi * nk:(qi + 1) * nk, :], ht, preferred_element_type=F32)
        gate = jnp.zeros((nk, tm), F32)
        for h in range(PEER_HEADS):
            s1 = s1_ref[h, 0, 0, qi:qi + 1, :]
            e1 = e1_ref[h, 0, qi:qi + 1, :]
            sel = (s1 + s2_ref[h, 0]) >= tau_ref[h]
            gate = gate + jnp.where(sel, e1 * e2_ref[h], 0.0)
        hs_ref[qi * nk:(qi + 1) * nk, :] = (_gelu(act) * gate).astype(BF16)
    o_ref[...] += lax.dot_general(hs_ref[...], v_ref[...], (((0,), (0,)), ((), ())),
                                  preferred_element_type=F32)


def _peer_dense(ht, u_tab, v_tab, scores, e1, e2, tau, tm=512, te=512):
    t = ht.shape[1]
    tm = min(tm, t)
    q = te // PEER_NKEYS
    ng = PEER_NKEYS // q
    s5d = scores.reshape(PEER_HEADS, 2, ng, q, t)
    s4d = scores.reshape(PEER_HEADS, 2, PEER_NKEYS, t)
    e1g = e1.reshape(PEER_HEADS, ng, q, t)
    return pl.pallas_call(
        functools.partial(_peer_dense_kernel, q=q),
        grid=(t // tm, PEER_EXPERTS // te),
        in_specs=[pl.BlockSpec((D_MODEL, tm), lambda i, j: (0, i)),
                  pl.BlockSpec((te, D_MODEL), lambda i, j: (j, 0)),
                  pl.BlockSpec((te, D_MODEL), lambda i, j: (j, 0)),
                  pl.BlockSpec((PEER_HEADS, 1, 1, q, tm), lambda i, j: (0, 0, j, 0, i)),
                  pl.BlockSpec((PEER_HEADS, 1, PEER_NKEYS, tm), lambda i, j: (0, 1, 0, i)),
                  pl.BlockSpec((PEER_HEADS, 1, q, tm), lambda i, j: (0, j, 0, i)),
                  pl.BlockSpec((PEER_HEADS, PEER_NKEYS, tm), lambda i, j: (0, 0, i)),
                  pl.BlockSpec((PEER_HEADS, 1, tm), lambda i, j: (0, 0, i))],
        out_specs=pl.BlockSpec((tm, D_MODEL), lambda i, j: (i, 0)),
        out_shape=jax.ShapeDtypeStruct((t, D_MODEL), F32),
        scratch_shapes=[pltpu.VMEM((te, tm), BF16)],
        compiler_params=_cparams(("parallel", "arbitrary")),
    )(ht, u_tab, v_tab, s5d, s4d, e1g, e2, tau)


def _ln_res_kernel(h_ref, f_ref, g_ref, b_ref, o_ref):
    o_ref[...] = _layer_norm(DEEPNORM_ALPHA * h_ref[...] + f_ref[...], g_ref[...], b_ref[...])


def _ln_res(h, f, g, b, tm=256):
    t = h.shape[0]
    tm = min(tm, t)
    full = pl.BlockSpec((tm, D_MODEL), lambda i: (i, 0))
    row = pl.BlockSpec((1, D_MODEL), lambda i: (0, 0))
    return pl.pallas_call(
        _ln_res_kernel,
        grid=(t // tm,),
        in_specs=[full, full, row, row],
        out_specs=full,
        out_shape=jax.ShapeDtypeStruct((t, D_MODEL), F32),
        compiler_params=_cparams(("parallel",)),
    )(h, f, g, b)


def _layer(h, bsz, seq, w_in, s5, rw, w_out, ln1, peer, ln2):
    (lam_re, lam_im, log_step, b_re, b_im, c_re, c_im, d_skip, w_glu, b_glu) = s5
    (mu, w0, w_up, a0, a_up, g_up, k_k, k_a, r_k, gn_g, gn_b) = rw
    (w_q, keys1, keys2, u_tab, v_tab) = peer
    hb = h.astype(BF16)
    n_rkv = 3 * D_RWKV
    w_in_b = w_in.astype(BF16)
    pad = RW_LORA_PAD - RW_LORA

    u = _matmul(hb, w_in_b[:, :D_SSM], BF16, 512, 512)
    p_rkv = _matmul(hb, w_in_b[:, D_SSM:D_SSM + n_rkv], F32, 512, 512)
    p_lora = _matmul(hb, jnp.pad(w_in_b[:, D_SSM + n_rkv:], ((0, 0), (0, pad))), F32, 512, RW_LORA_PAD)

    yg = _s5_scan(u, _s5_params(lam_re, lam_im, log_step, b_re, b_im, c_re, c_im, d_skip,
                                seq // S5_CHUNK), bsz, seq)
    y_ssm = _glu(yg, w_glu.astype(BF16), b_glu[None, :])

    mu_rkv = mu[None, :n_rkv]
    mu_lora = jnp.pad(mu[n_rkv:], (0, pad))[None, :]
    zrow = lambda n: jnp.zeros((n, D_RWKV), F32)
    wup = jnp.concatenate([w_up, zrow(RW_LORA_PAD - RW_W_LORA)], axis=0).astype(BF16)
    aup = jnp.concatenate([zrow(RW_W_LORA), a_up, zrow(RW_LORA_PAD - RW_W_LORA - RW_A_LORA)],
                          axis=0).astype(BF16)
    gup = jnp.concatenate([zrow(RW_W_LORA + RW_A_LORA), g_up, zrow(pad)], axis=0).astype(BF16)
    r, k, v, kkr, a, lw, g = _rw_prep(p_rkv, p_lora, mu_rkv, mu_lora, w0[None], a0[None],
                                      k_k[None], k_a[None], wup, aup, gup, seq)
    q, oi, bonus, m, n = _rw_chunks(r, k, v, kkr, a, lw, r_k.reshape(1, D_RWKV))
    y_rw = _rw_seq(q, oi, bonus, g, m, n, gn_g[None], gn_b[None], bsz, seq)

    h1, h1b = _outproj(y_ssm, y_rw, w_out.astype(BF16), h, ln1[0][None], ln1[1][None])

    keys = jnp.stack([keys1, keys2], axis=1).reshape(2 * PEER_HEADS, PEER_NKEYS, PEER_HALF)
    scores = _peer_scores(h1b, w_q.astype(BF16), keys.astype(BF16))
    tau, e1, e2 = _peer_topk(scores)
    ffn = _peer_dense(h1b.T, u_tab.astype(BF16), v_tab.astype(BF16), scores, e1, e2, tau)
    return _ln_res(h1, ffn, ln2[0][None], ln2[1][None])


def kernel(x, w_in, s5_lam_re, s5_lam_im, s5_log_step, s5_b_re, s5_b_im, s5_c_re, s5_c_im, s5_d, s5_w_glu, s5_b_glu, rw_mu, rw_w0, rw_w_up, rw_a0, rw_a_up, rw_g_up, rw_k_k, rw_k_a, rw_r_k, rw_gn_g, rw_gn_b, w_out, ln1_g, ln1_b, peer_w_q, peer_keys1, peer_keys2, peer_u, peer_v, ln2_g, ln2_b):
    bsz, seq, dim = x.shape
    h = x.reshape(bsz * seq, dim)
    for l in range(w_in.shape[0]):
        h = _layer(
            h, bsz, seq, w_in[l],
            (s5_lam_re[l], s5_lam_im[l], s5_log_step[l], s5_b_re[l], s5_b_im[l], s5_c_re[l],
             s5_c_im[l], s5_d[l], s5_w_glu[l], s5_b_glu[l]),
            (rw_mu[l], rw_w0[l], rw_w_up[l], rw_a0[l], rw_a_up[l], rw_g_up[l], rw_k_k[l],
             rw_k_a[l], rw_r_k[l], rw_gn_g[l], rw_gn_b[l]),
            w_out[l], (ln1_g[l], ln1_b[l]),
            (peer_w_q[l], peer_keys1[l], peer_keys2[l], peer_u[l], peer_v[l]),
            (ln2_g[l], ln2_b[l]))
    return h.reshape(bsz, seq, dim)
```

```python
import functools
import math

import jax
import jax.numpy as jnp
from jax import lax
from jax.experimental import pallas as pl
from jax.experimental.pallas import tpu as pltpu

F32 = jnp.float32
BF16 = jnp.bfloat16

D_MODEL = 2048
D_SSM = 1024
D_RWKV = 1024
S5_CH = 16
S5_GROUPS = D_SSM // S5_CH
S5_STATE = 64
S5_CHUNK = 16
RW_HEAD = 64
RW_HEADS = D_RWKV // RW_HEAD
RW_W_LORA = 64
RW_A_LORA = 64
RW_G_LORA = 160
RW_LORA = RW_W_LORA + RW_A_LORA + RW_G_LORA
RW_LORA_PAD = 384
RW_CHUNK = 64
PEER_HEADS = 8
PEER_NKEYS = 128
PEER_EXPERTS = PEER_NKEYS * PEER_NKEYS
PEER_HALF = 128
PEER_TOPK = 16
DEPTH = 1
DEEPNORM_ALPHA = (2.0 * DEPTH) ** 0.25
LN_EPS = 1e-5
RW_GN_EPS = 64e-5
NEG_INF = float("-inf")

VMEM_LIMIT = 56 * 1024 * 1024


def _cparams(sem):
    return pltpu.CompilerParams(dimension_semantics=sem, vmem_limit_bytes=VMEM_LIMIT)


def _dot(a, b):
    return jnp.dot(a.astype(BF16), b.astype(BF16), preferred_element_type=F32)


def _dot_nt(a, b):
    return lax.dot_general(a.astype(BF16), b.astype(BF16), (((1,), (1,)), ((), ())),
                           preferred_element_type=F32)


def _dot_tn(a, b):
    return lax.dot_general(a.astype(BF16), b.astype(BF16), (((0,), (0,)), ((), ())),
                           preferred_element_type=F32)


def _gelu(x):
    c = math.sqrt(2.0 / math.pi)
    return 0.5 * x * (1.0 + jnp.tanh(c * (x + 0.044715 * (x * x * x))))


def _sigmoid(x):
    return 1.0 / (1.0 + jnp.exp(-x))


def _mm_kernel(a_ref, b_ref, o_ref):
    o_ref[...] = jnp.dot(a_ref[...], b_ref[...],
                         preferred_element_type=F32).astype(o_ref.dtype)


def _matmul(a, b, out_dtype, tm, tn):
    m, k = a.shape
    n = b.shape[1]
    tm = min(tm, m)
    return pl.pallas_call(
        _mm_kernel,
        grid=(m // tm, n // tn),
        in_specs=[pl.BlockSpec((tm, k), lambda i, j: (i, 0)),
                  pl.BlockSpec((k, tn), lambda i, j: (0, j))],
        out_specs=pl.BlockSpec((tm, tn), lambda i, j: (i, j)),
        out_shape=jax.ShapeDtypeStruct((m, n), out_dtype),
        compiler_params=_cparams(("parallel", "parallel")),
    )(a, b)


def _s5_kernel(u_ref, mt_ref, bc_ref, cc_ref, ar_ref, ai_ref, d_ref, o_ref, *, nc):
    u = u_ref[0]
    rows = u.shape[0]
    y = jnp.dot(u, mt_ref[0], preferred_element_type=F32)
    s = jnp.dot(u, bc_ref[0], preferred_element_type=F32)
    cidx = lax.broadcasted_iota(jnp.int32, (rows, 2 * S5_STATE), 0) & (nc - 1)
    d, k = 1, 0
    while d < nc:
        sh = jnp.where(cidx >= d, pltpu.roll(s, d, axis=0), 0.0)
        shs = pltpu.roll(sh, S5_STATE, axis=1)
        s = s + sh * ar_ref[0, k:k + 1, :] + shs * ai_ref[0, k:k + 1, :]
        d, k = d * 2, k + 1
    sp = jnp.where(cidx >= 1, pltpu.roll(s, 1, axis=0), 0.0)
    y = y + jnp.dot(sp.astype(BF16), cc_ref[0], preferred_element_type=F32)
    y = y + d_ref[0] * u.astype(F32)
    o_ref[0] = _gelu(y).astype(o_ref.dtype)


def _s5_params(lam_re, lam_im, log_step, b_re, b_im, c_re, c_im, d_skip, nc):
    hi = lax.Precision.HIGHEST
    c = S5_CHUNK
    step = jnp.exp(log_step)[:, None]
    a_re = jnp.exp(lam_re * step) * jnp.cos(lam_im * step)
    a_im = jnp.exp(lam_re * step) * jnp.sin(lam_im * step)
    den = lam_re * lam_re + lam_im * lam_im
    f_re = ((a_re - 1.0) * lam_re + a_im * lam_im) / den
    f_im = (a_im * lam_re - (a_re - 1.0) * lam_im) / den
    bb_re = f_re[..., None] * b_re - f_im[..., None] * b_im
    bb_im = f_re[..., None] * b_im + f_im[..., None] * b_re

    def power(j):
        jj = j[None, :, None]
        mag = jnp.exp(lam_re[:, None, :] * step[:, None, :] * jj)
        ang = lam_im[:, None, :] * step[:, None, :] * jj
        return mag * jnp.cos(ang), mag * jnp.sin(ang)

    pw_re, pw_im = power(jnp.arange(c + 1, dtype=F32))
    ce_re = c_re[:, None] * pw_re[:, :, None, :] - c_im[:, None] * pw_im[:, :, None, :]
    ce_im = c_re[:, None] * pw_im[:, :, None, :] + c_im[:, None] * pw_re[:, :, None, :]
    kern = (jnp.einsum('gjop,gpi->gjoi', ce_re[:, :c], bb_re, precision=hi)
            - jnp.einsum('gjop,gpi->gjoi', ce_im[:, :c], bb_im, precision=hi))
    t_idx = jnp.arange(c)
    lag = t_idx[None, :] - t_idx[:, None]
    kt = jnp.take(kern, jnp.clip(lag, 0, c - 1), axis=1)
    kt = jnp.where((lag >= 0)[None, :, :, None, None], kt, 0.0)
    g = lam_re.shape[0]
    mt = jnp.transpose(kt, (0, 1, 4, 2, 3)).reshape(g, c * S5_CH, c * S5_CH)
    rv_re, rv_im = pw_re[:, c - 1::-1][:, :c], pw_im[:, c - 1::-1][:, :c]
    bc_r = rv_re[:, :, None, :] * jnp.transpose(bb_re, (0, 2, 1))[:, None] \
        - rv_im[:, :, None, :] * jnp.transpose(bb_im, (0, 2, 1))[:, None]
    bc_i = rv_re[:, :, None, :] * jnp.transpose(bb_im, (0, 2, 1))[:, None] \
        + rv_im[:, :, None, :] * jnp.transpose(bb_re, (0, 2, 1))[:, None]
    bc = jnp.concatenate([bc_r, bc_i], axis=-1).reshape(g, c * S5_CH, 2 * S5_STATE)
    cc_r = jnp.transpose(ce_re[:, 1:], (0, 3, 1, 2)).reshape(g, S5_STATE, c * S5_CH)
    cc_i = -jnp.transpose(ce_im[:, 1:], (0, 3, 1, 2)).reshape(g, S5_STATE, c * S5_CH)
    cc = jnp.concatenate([cc_r, cc_i], axis=1)
    nlog = max(1, int(math.ceil(math.log2(nc))))
    sc_re, sc_im = power(c * (2.0 ** jnp.arange(8, dtype=F32)))
    ar = jnp.concatenate([sc_re, sc_re], axis=-1)
    ai = jnp.concatenate([-sc_im, sc_im], axis=-1)
    del nlog
    dflat = jnp.tile(d_skip.reshape(g, 1, S5_CH), (1, c, 1)).reshape(g, 1, c * S5_CH)
    return mt.astype(BF16), bc.astype(BF16), cc.astype(BF16), ar, ai, dflat


def _s5_scan(u_tok, params, bsz, seq):
    mt, bc, cc, ar, ai, dflat = params
    nc = seq // S5_CHUNK
    g = S5_GROUPS
    rows = bsz * nc
    w = S5_CHUNK * S5_CH
    uf = u_tok.reshape(bsz, nc, S5_CHUNK, g, S5_CH)
    uf = jnp.transpose(uf, (3, 0, 1, 2, 4)).reshape(g, rows, w)
    spec3 = lambda a, b: pl.BlockSpec((1, a, b), lambda i: (i, 0, 0))
    yf = pl.pallas_call(
        functools.partial(_s5_kernel, nc=nc),
        grid=(g,),
        in_specs=[spec3(rows, w), spec3(w, w), spec3(w, 2 * S5_STATE), spec3(2 * S5_STATE, w),
                  spec3(8, 2 * S5_STATE), spec3(8, 2 * S5_STATE), spec3(1, w)],
        out_specs=spec3(rows, w),
        out_shape=jax.ShapeDtypeStruct((g, rows, w), BF16),
        compiler_params=_cparams(("parallel",)),
    )(uf, mt, bc, cc, ar, ai, dflat)
    yf = yf.reshape(g, bsz, nc, S5_CHUNK, S5_CH)
    return jnp.transpose(yf, (1, 2, 3, 0, 4)).reshape(bsz * seq, D_SSM)


def _glu_kernel(y_ref, w_ref, b_ref, o_ref):
    y = y_ref[...]
    z = jnp.dot(y, w_ref[...], preferred_element_type=F32) + b_ref[...]
    o_ref[...] = (y.astype(F32) * _sigmoid(z)).astype(o_ref.dtype)


def _glu(y, w, b, tm=512):
    m, n = y.shape
    tm = min(tm, m)
    return pl.pallas_call(
        _glu_kernel,
        grid=(m // tm,),
        in_specs=[pl.BlockSpec((tm, n), lambda i: (i, 0)),
                  pl.BlockSpec((n, n), lambda i: (0, 0)),
                  pl.BlockSpec((1, n), lambda i: (0, 0))],
        out_specs=pl.BlockSpec((tm, n), lambda i: (i, 0)),
        out_shape=jax.ShapeDtypeStruct((m, n), BF16),
        compiler_params=_cparams(("parallel",)),
    )(y, w, b)


def _rw_prep_kernel(p_ref, pp_ref, l_ref, lp_ref, mu_ref, mul_ref, w0_ref, a0_ref,
                    kk_ref, ka_ref, wup_ref, aup_ref, gup_ref,
                    r_ref, k_ref, v_ref, kkr_ref, a_ref, lw_ref, g_ref, *, tiles_per_seq):
    first = (pl.program_id(0) % tiles_per_seq) == 0

    def shifted(cur_ref, prev_ref, mu):
        cur = cur_ref[...]
        prev_row = jnp.where(first, 0.0, prev_ref[7:8, :])
        row = lax.broadcasted_iota(jnp.int32, cur.shape, 0)
        prev = jnp.where(row == 0, prev_row, pltpu.roll(cur, 1, axis=0))
        return cur + mu * (prev - cur)

    p = shifted(p_ref, pp_ref, mu_ref[...])
    lo = shifted(l_ref, lp_ref, mul_ref[...])
    r = p[:, :D_RWKV]
    k = p[:, D_RWKV:2 * D_RWKV]
    v = p[:, 2 * D_RWKV:]
    w_pre = w0_ref[...] + _dot(jnp.tanh(lo), wup_ref[...])
    a = _sigmoid(a0_ref[...] + _dot(lo, aup_ref[...]))
    g = _dot(_sigmoid(lo), gup_ref[...])
    z = -w_pre
    softplus = jnp.maximum(z, 0.0) + jnp.log(1.0 + jnp.exp(-jnp.abs(z)))
    w = -softplus - 0.5
    r_ref[...] = r
    k_ref[...] = k * (1.0 + (a - 1.0) * ka_ref[...])
    v_ref[...] = v
    kkr_ref[...] = k * kk_ref[...]
    a_ref[...] = a
    lw_ref[...] = -jnp.exp(w)
    g_ref[...] = g


def _rw_prep(p_rkv, p_lora, mu_rkv, mu_lora, w0, a0, k_k, k_a, wup, aup, gup, seq, tm=256):
    t = p_rkv.shape[0]
    tm = min(tm, seq)
    n3 = 3 * D_RWKV
    row = lambda n: pl.BlockSpec((1, n), lambda i: (0, 0))
    full = lambda a, b: pl.BlockSpec((a, b), lambda i: (0, 0))
    prev = lambda n: pl.BlockSpec((8, n), lambda i: (jnp.maximum(i * (tm // 8) - 1, 0), 0))
    out = jax.ShapeDtypeStruct((t, D_RWKV), F32)
    ospec = pl.BlockSpec((tm, D_RWKV), lambda i: (i, 0))
    return pl.pallas_call(
        functools.partial(_rw_prep_kernel, tiles_per_seq=seq // tm),
        grid=(t // tm,),
        in_specs=[pl.BlockSpec((tm, n3), lambda i: (i, 0)), prev(n3),
                  pl.BlockSpec((tm, RW_LORA_PAD), lambda i: (i, 0)), prev(RW_LORA_PAD),
                  row(n3), row(RW_LORA_PAD), row(D_RWKV), row(D_RWKV), row(D_RWKV), row(D_RWKV),
                  full(RW_LORA_PAD, D_RWKV), full(RW_LORA_PAD, D_RWKV), full(RW_LORA_PAD, D_RWKV)],
        out_specs=[ospec] * 7,
        out_shape=[out] * 7,
        compiler_params=_cparams(("parallel",)),
    )(p_rkv, p_rkv, p_lora, p_lora, mu_rkv, mu_lora, w0, a0, k_k, k_a, wup, aup, gup)


def _rw_chunk_kernel(r_ref, k_ref, v_ref, kkr_ref, a_ref, lw_ref, rk_ref,
                     q_ref, oi_ref, bonus_ref, m_ref, n_ref):
    c = RW_CHUNK
    hd = RW_HEAD
    heads = range(r_ref.shape[1] // hd)
    ri = lax.broadcasted_iota(jnp.int32, (c, c), 0)
    ci = lax.broadcasted_iota(jnp.int32, (c, c), 1)
    tri_incl = (ci <= ri)
    tri_strict = (ci < ri)
    ltri = tri_incl.astype(BF16)
    eye = (ci == ri).astype(F32)
    lw = lw_ref[...]
    lw_hi = lw.astype(BF16)
    lw_lo = (lw - lw_hi.astype(F32)).astype(BF16)
    cl = (jnp.dot(ltri, lw_hi, preferred_element_type=F32)
          + jnp.dot(ltri, lw_lo, preferred_element_type=F32))
    cl_end = cl[c - 1:c, :]
    r = r_ref[...]
    k = k_ref[...]
    rt_d = r * jnp.exp(cl)
    e_neg_d = jnp.exp(-cl)
    kt_d = k * e_neg_d
    e_prev_d = jnp.exp(cl - lw)
    e_end_d = jnp.exp(cl_end - cl)
    kte_d = k * e_end_d
    gam_d = jnp.exp(cl_end)
    rk_d = r * k * rk_ref[0:1, :]
    kkr_d = kkr_ref[...]
    a_d = a_ref[...]
    sl = [slice(h * hd, (h + 1) * hd) for h in heads]
    v = [v_ref[:, s] for s in sl]
    rt = [rt_d[:, s] for s in sl]
    kkt, bt, bte = [], [], []
    for h in heads:
        kkr = kkr_d[:, sl[h]]
        nrm = jnp.sqrt(jnp.sum(kkr * kkr, axis=-1, keepdims=True))
        kk = kkr / jnp.maximum(nrm, 1e-12)
        b = kk * a_d[:, sl[h]]
        kkt.append(kk * e_prev_d[:, sl[h]])
        bt.append(b * e_neg_d[:, sl[h]])
        bte.append(b * e_end_d[:, sl[h]])
    a4 = [_dot_nt(jnp.concatenate([kkt[h], rt[h]], axis=0),
                  jnp.concatenate([kt_d[:, sl[h]], bt[h]], axis=0)) for h in heads]
    a_kb = [jnp.where(tri_strict, a4[h][:c, c:], 0.0) for h in heads]
    a_rb = [jnp.where(tri_incl, a4[h][c:, c:], 0.0) for h in heads]
    a_kr = [jnp.concatenate([jnp.where(tri_strict, a4[h][:c, :c], 0.0),
                             jnp.where(tri_incl, a4[h][c:, :c], 0.0)], axis=0) for h in heads]
    x = [eye - a_kb[h] for h in heads]
    pw = [_dot(a_kb[h], a_kb[h]) for h in heads]
    n_sq = int(math.log2(c)) - 1
    for it in range(n_sq):
        if it + 1 < n_sq:
            xp = [_dot(jnp.concatenate([x[h], pw[h]], axis=0), pw[h]) for h in heads]
            x = [x[h] + xp[h][:c] for h in heads]
            pw = [xp[h][c:] for h in heads]
        else:
            x = [x[h] + _dot(x[h], pw[h]) for h in heads]
    av = [_dot(a_kr[h], v[h]) for h in heads]
    wu = [_dot(x[h], jnp.concatenate([kkt[h], av[h][:c]], axis=1)) for h in heads]
    rb = [_dot(a_rb[h], wu[h]) for h in heads]
    wub = [_dot_tn(wu[h], bte[h]) for h in heads]
    vk = [_dot_tn(v[h], kte_d[:, sl[h]]) for h in heads]
    ji = lax.broadcasted_iota(jnp.int32, (hd, hd), 0)
    jo = lax.broadcasted_iota(jnp.int32, (hd, hd), 1)
    for h in heads:
        m_ref[0, h] = jnp.where(ji == jo, gam_d[:, sl[h]], 0.0) - wub[h][:hd]
        n_ref[0, h] = vk[h] - wub[h][hd:]
    q_ref[...] = jnp.concatenate([rt[h] - rb[h][:, :hd] for h in heads], axis=1)
    oi_ref[...] = jnp.concatenate([av[h][c:] - rb[h][:, hd:] for h in heads], axis=1)
    bonus_ref[...] = jnp.concatenate(
        [jnp.sum(rk_d[:, sl[h]], axis=-1, keepdims=True) * v[h] for h in heads], axis=1)


def _rw_chunks(r, k, v, kkr, a, lw, r_k, hp=RW_HEADS):
    t = r.shape[0]
    c = RW_CHUNK
    nch = t // c
    wdt = hp * RW_HEAD
    blk = pl.BlockSpec((c, wdt), lambda i, j: (i, j))
    mat = pl.BlockSpec((1, hp, RW_HEAD, RW_HEAD), lambda i, j: (i, j, 0, 0))
    tok = jax.ShapeDtypeStruct((t, D_RWKV), F32)
    mshape = jax.ShapeDtypeStruct((nch, RW_HEADS, RW_HEAD, RW_HEAD), F32)
    return pl.pallas_call(
        _rw_chunk_kernel,
        grid=(nch, RW_HEADS // hp),
        in_specs=[blk] * 6 + [pl.BlockSpec((1, wdt), lambda i, j: (0, j))],
        out_specs=[blk, blk, blk, mat, mat],
        out_shape=[tok, tok, tok, mshape, mshape],
        compiler_params=_cparams(("parallel", "parallel")),
    )(r, k, v, kkr, a, lw, r_k)


def _rw_seq_kernel(q_ref, oi_ref, bonus_ref, g_ref, m_ref, n_ref, gg_ref, gb_ref,
                   o_ref, st_ref):
    @pl.when(pl.program_id(0) == 0)
    def _():
        st_ref[...] = jnp.zeros_like(st_ref)

    hd = RW_HEAD
    heads = range(RW_HEADS)
    sl = [slice(h * hd, (h + 1) * hd) for h in heads]

    def per_batch(b, carry):
        s = [st_ref[b, h] for h in heads]
        q = q_ref[b]
        o = [_dot_nt(q[:, sl[h]], s[h]) for h in heads]
        for h in heads:
            st_ref[b, h] = _dot(s[h], m_ref[b, 0, h]) + n_ref[b, 0, h]
        oi = oi_ref[b]
        outs = []
        for h in heads:
            oh = o[h] + oi[:, sl[h]]
            mean = jnp.mean(oh, axis=-1, keepdims=True)
            ctr = oh - mean
            var = jnp.mean(jnp.square(ctr), axis=-1, keepdims=True)
            outs.append(ctr * lax.rsqrt(var + RW_GN_EPS))
        o = jnp.concatenate(outs, axis=1) * gg_ref[...] + gb_ref[...] + bonus_ref[b]
        o_ref[b] = (o * g_ref[b]).astype(o_ref.dtype)
        return carry

    lax.fori_loop(0, q_ref.shape[0], per_batch, 0)


def _rw_seq(q, oi, bonus, g, m, n, gn_g, gn_b, bsz, seq):
    c = RW_CHUNK
    nch = seq // c
    r3 = lambda x: x.reshape(bsz, seq, D_RWKV)
    r5 = lambda x: x.reshape(bsz, nch, RW_HEADS, RW_HEAD, RW_HEAD)
    tok = pl.BlockSpec((bsz, c, D_RWKV), lambda i: (0, i, 0))
    mat = pl.BlockSpec((bsz, 1, RW_HEADS, RW_HEAD, RW_HEAD), lambda i: (0, i, 0, 0, 0))
    row = pl.BlockSpec((1, D_RWKV), lambda i: (0, 0))
    y = pl.pallas_call(
        _rw_seq_kernel,
        grid=(nch,),
        in_specs=[tok, tok, tok, tok, mat, mat, row, row],
        out_specs=tok,
        out_shape=jax.ShapeDtypeStruct((bsz, seq, D_RWKV), BF16),
        scratch_shapes=[pltpu.VMEM((bsz, RW_HEADS, RW_HEAD, RW_HEAD), F32)],
        compiler_params=_cparams(("arbitrary",)),
    )(r3(q), r3(oi), r3(bonus), r3(g), r5(m), r5(n), gn_g, gn_b)
    return y.reshape(bsz * seq, D_RWKV)


def _layer_norm(x, g, b):
    mu = jnp.mean(x, axis=-1, keepdims=True)
    var = jnp.mean(jnp.square(x - mu), axis=-1, keepdims=True)
    return (x - mu) * lax.rsqrt(var + LN_EPS) * g + b


def _outproj_kernel(ys_ref, yr_ref, w1_ref, w2_ref, x_ref, g_ref, b_ref, h_ref, hb_ref):
    mix = (jnp.dot(ys_ref[...], w1_ref[...], preferred_element_type=F32)
           + jnp.dot(yr_ref[...], w2_ref[...], preferred_element_type=F32))
    h = _layer_norm(DEEPNORM_ALPHA * x_ref[...] + mix, g_ref[...], b_ref[...])
    h_ref[...] = h
    hb_ref[...] = h.astype(BF16)


def _outproj(ys, yr, w_out, x, g, b, tm=256):
    t = x.shape[0]
    tm = min(tm, t)
    half = pl.BlockSpec((tm, D_SSM), lambda i: (i, 0))
    full = pl.BlockSpec((tm, D_MODEL), lambda i: (i, 0))
    row = pl.BlockSpec((1, D_MODEL), lambda i: (0, 0))
    return pl.pallas_call(
        _outproj_kernel,
        grid=(t // tm,),
        in_specs=[half, half,
                  pl.BlockSpec((D_SSM, D_MODEL), lambda i: (0, 0)),
                  pl.BlockSpec((D_RWKV, D_MODEL), lambda i: (1, 0)),
                  full, row, row],
        out_specs=[full, full],
        out_shape=[jax.ShapeDtypeStruct((t, D_MODEL), F32),
                   jax.ShapeDtypeStruct((t, D_MODEL), BF16)],
        compiler_params=_cparams(("parallel",)),
    )(ys, yr, w_out, w_out, x, g, b)


def _peer_scores_kernel(h_ref, wq_ref, keys_ref, s_ref):
    q = jnp.dot(h_ref[...], wq_ref[...], preferred_element_type=F32)
    for blk in range(2 * PEER_HEADS):
        qb = q[:, blk * PEER_HALF:(blk + 1) * PEER_HALF]
        s_ref[blk] = _dot_nt(keys_ref[blk], qb)


def _peer_scores(hb, wq, keys, tm=256):
    t = hb.shape[0]
    tm = min(tm, t)
    nb = 2 * PEER_HEADS
    return pl.pallas_call(
        _peer_scores_kernel,
        grid=(t // tm,),
        in_specs=[pl.BlockSpec((tm, D_MODEL), lambda i: (i, 0)),
                  pl.BlockSpec((D_MODEL, D_MODEL), lambda i: (0, 0)),
                  pl.BlockSpec((nb, PEER_NKEYS, PEER_HALF), lambda i: (0, 0, 0))],
        out_specs=pl.BlockSpec((nb, PEER_NKEYS, tm), lambda i: (0, 0, i)),
        out_shape=jax.ShapeDtypeStruct((nb, PEER_NKEYS, t), F32),
        compiler_params=_cparams(("parallel",)),
    )(hb, wq, keys)


NO_RANK = 127.0


def _extract_top(x, n, want_rank=False):
    vals = []
    rank = jnp.full(x.shape, NO_RANK, F32) if want_rank else None
    for i in range(n):
        m = jnp.max(x, axis=0, keepdims=True)
        vals.append(m)
        hit = x == m
        if want_rank:
            rank = jnp.where(hit, float(i), rank)
        x = jnp.where(hit, NEG_INF, x)
    return vals, rank


def _peer_topk_kernel(s_ref, nsel_ref, e1_ref, rank2_ref, e2_ref):
    k = PEER_TOPK

    def per_head(h, carry):
        s1 = s_ref[2 * h]
        s2 = s_ref[2 * h + 1]
        ta, _ = _extract_top(s1, k)
        tb, rank2 = _extract_top(s2, k, want_rank=True)
        tbs = jnp.concatenate(tb, axis=0)
        cands = [ta[i] + tbs[0:k // (i + 1), :] for i in range(k)]
        n_c = sum(k // (i + 1) for i in range(k))
        pad = (-n_c) % 8
        padded = cands + ([jnp.full((pad, s1.shape[1]), NEG_INF, F32)] if pad else [])
        best, _ = _extract_top(jnp.concatenate(padded, axis=0), k)
        m0 = best[0]
        tau = best[k - 1]
        z = jnp.zeros_like(m0)
        for bv in best:
            z = z + jnp.exp(bv - m0)
        nsel = jnp.zeros_like(s1)
        for i in range(k):
            cnt = jnp.sum(jnp.where(cands[i] >= tau, 1.0, 0.0), axis=0, keepdims=True)
            nsel = jnp.where(s1 == ta[i], cnt, nsel)
        nsel_ref[h] = nsel
        e1_ref[h] = jnp.exp(s1 - ta[0]) / z
        rank2_ref[h] = rank2.astype(BF16)
        e2_ref[h] = jnp.exp(s2 - tb[0]).astype(BF16)
        return carry

    lax.fori_loop(0, PEER_HEADS, per_head, 0)


def _peer_topk(scores, tt=256):
    nb, nk, t = scores.shape
    tt = min(tt, t)
    big = pl.BlockSpec((PEER_HEADS, nk, tt), lambda i: (0, 0, i))
    f32 = jax.ShapeDtypeStruct((PEER_HEADS, nk, t), F32)
    b16 = jax.ShapeDtypeStruct((PEER_HEADS, nk, t), BF16)
    return pl.pallas_call(
        _peer_topk_kernel,
        grid=(t // tt,),
        in_specs=[pl.BlockSpec((nb, nk, tt), lambda i: (0, 0, i))],
        out_specs=[big, big, big, big],
        out_shape=[f32, f32, b16, b16],
        compiler_params=_cparams(("parallel",)),
    )(scores)


def _peer_dense_kernel(ht_ref, u_ref, va_ref, vb_ref, nsel_ref, e1_ref, rank2_ref, e2_ref,
                       o_ref, hsa_ref, hsb_ref, *, q, nj):
    j = pl.program_id(1)

    @pl.when(j == 0)
    def _():
        o_ref[...] = jnp.zeros_like(o_ref)
        hsb_ref[...] = jnp.zeros_like(hsb_ref)

    nk = PEER_NKEYS
    tm = ht_ref.shape[1]
    ht = ht_ref[...]
    live = jnp.where(j < nj, 1.0, 0.0)

    def hidden(half, dst_ref):
        for qi in range(q):
            row = half * q + qi
            act = jnp.dot(u_ref[row * nk:(row + 1) * nk, :], ht, preferred_element_type=F32)
            gate = jnp.zeros((nk, tm), BF16)
            for h in range(PEER_HEADS):
                nsel = (nsel_ref[h, 0, row:row + 1, :] * live).astype(BF16)
                e1 = e1_ref[h, 0, row:row + 1, :].astype(BF16)
                gate = gate + jnp.where(rank2_ref[h] < nsel, e1 * e2_ref[h], 0.0)
            dst_ref[qi * nk:(qi + 1) * nk, :] = _gelu(act.astype(BF16)) * gate

    def values(src_ref, v_ref):
        return lax.dot_general(src_ref[...], v_ref[...], (((0,), (0,)), ((), ())),
                               preferred_element_type=F32)

    acc_b = values(hsb_ref, vb_ref)
    hidden(0, hsa_ref)
    o_ref[...] += acc_b
    acc_a = values(hsa_ref, va_ref)
    hidden(1, hsb_ref)
    o_ref[...] += acc_a


def _peer_dense(ht, u_tab, v_tab, nsel, e1, rank2, e2, tm=512, te=1024):
    t = ht.shape[1]
    tm = min(tm, t)
    half = te // 2
    q = half // PEER_NKEYS
    ng = PEER_NKEYS // (2 * q)
    nj = PEER_EXPERTS // te
    rows = lambda a: a.reshape(PEER_HEADS, ng, 2 * q, t)
    cur = lambda j: jnp.minimum(j, nj - 1)
    row_spec = pl.BlockSpec((PEER_HEADS, 1, 2 * q, tm), lambda i, j: (0, cur(j), 0, i))
    key_spec = pl.BlockSpec((PEER_HEADS, PEER_NKEYS, tm), lambda i, j: (0, 0, i))
    return pl.pallas_call(
        functools.partial(_peer_dense_kernel, q=q, nj=nj),
        grid=(t // tm, nj + 1),
        in_specs=[pl.BlockSpec((D_MODEL, tm), lambda i, j: (0, i)),
                  pl.BlockSpec((te, D_MODEL), lambda i, j: (cur(j), 0)),
                  pl.BlockSpec((half, D_MODEL), lambda i, j: (2 * cur(j), 0)),
                  pl.BlockSpec((half, D_MODEL), lambda i, j: (jnp.maximum(2 * j - 1, 0), 0)),
                  row_spec, row_spec, key_spec, key_spec],
        out_specs=pl.BlockSpec((tm, D_MODEL), lambda i, j: (i, 0)),
        out_shape=jax.ShapeDtypeStruct((t, D_MODEL), F32),
        scratch_shapes=[pltpu.VMEM((half, tm), BF16), pltpu.VMEM((half, tm), BF16)],
        compiler_params=_cparams(("parallel", "arbitrary")),
    )(ht, u_tab, v_tab, v_tab, rows(nsel), rows(e1), rank2, e2)


def _ln_res_kernel(h_ref, f_ref, g_ref, b_ref, o_ref):
    o_ref[...] = _layer_norm(DEEPNORM_ALPHA * h_ref[...] + f_ref[...], g_ref[...], b_ref[...])


def _ln_res(h, f, g, b, tm=256):
    t = h.shape[0]
    tm = min(tm, t)
    full = pl.BlockSpec((tm, D_MODEL), lambda i: (i, 0))
    row = pl.BlockSpec((1, D_MODEL), lambda i: (0, 0))
    return pl.pallas_call(
        _ln_res_kernel,
        grid=(t // tm,),
        in_specs=[full, full, row, row],
        out_specs=full,
        out_shape=jax.ShapeDtypeStruct((t, D_MODEL), F32),
        compiler_params=_cparams(("parallel",)),
    )(h, f, g, b)


def _layer(h, bsz, seq, w_in, s5, rw, w_out, ln1, peer, ln2):
    (lam_re, lam_im, log_step, b_re, b_im, c_re, c_im, d_skip, w_glu, b_glu) = s5
    (mu, w0, w_up, a0, a_up, g_up, k_k, k_a, r_k, gn_g, gn_b) = rw
    (w_q, keys1, keys2, u_tab, v_tab) = peer
    hb = h.astype(BF16)
    n_rkv = 3 * D_RWKV
    w_in_b = w_in.astype(BF16)
    pad = RW_LORA_PAD - RW_LORA

    u = _matmul(hb, w_in_b[:, :D_SSM], BF16, 512, 512)
    p_rkv = _matmul(hb, w_in_b[:, D_SSM:D_SSM + n_rkv], F32, 512, 512)
    p_lora = _matmul(hb, jnp.pad(w_in_b[:, D_SSM + n_rkv:], ((0, 0), (0, pad))), F32, 512, RW_LORA_PAD)

    yg = _s5_scan(u, _s5_params(lam_re, lam_im, log_step, b_re, b_im, c_re, c_im, d_skip,
                                seq // S5_CHUNK), bsz, seq)
    y_ssm = _glu(yg, w_glu.astype(BF16), b_glu[None, :])

    mu_rkv = mu[None, :n_rkv]
    mu_lora = jnp.pad(mu[n_rkv:], (0, pad))[None, :]
    zrow = lambda n: jnp.zeros((n, D_RWKV), F32)
    wup = jnp.concatenate([w_up, zrow(RW_LORA_PAD - RW_W_LORA)], axis=0).astype(BF16)
    aup = jnp.concatenate([zrow(RW_W_LORA), a_up, zrow(RW_LORA_PAD - RW_W_LORA - RW_A_LORA)],
                          axis=0).astype(BF16)
    gup = jnp.concatenate([zrow(RW_W_LORA + RW_A_LORA), g_up, zrow(pad)], axis=0).astype(BF16)
    r, k, v, kkr, a, lw, g = _rw_prep(p_rkv, p_lora, mu_rkv, mu_lora, w0[None], a0[None],
                                      k_k[None], k_a[None], wup, aup, gup, seq)
    q, oi, bonus, m, n = _rw_chunks(r, k, v, kkr, a, lw, r_k.reshape(1, D_RWKV))
    y_rw = _rw_seq(q, oi, bonus, g, m, n, gn_g[None], gn_b[None], bsz, seq)

    h1, h1b = _outproj(y_ssm, y_rw, w_out.astype(BF16), h, ln1[0][None], ln1[1][None])

    keys = jnp.stack([keys1, keys2], axis=1).reshape(2 * PEER_HEADS, PEER_NKEYS, PEER_HALF)
    scores = _peer_scores(h1b, w_q.astype(BF16), keys.astype(BF16))
    nsel, e1, rank2, e2 = _peer_topk(scores)
    ffn = _peer_dense(h1b.T, u_tab.astype(BF16), v_tab.astype(BF16), nsel, e1, rank2, e2)
    return _ln_res(h1, ffn, ln2[0][None], ln2[1][None])


def kernel(x, w_in, s5_lam_re, s5_lam_im, s5_log_step, s5_b_re, s5_b_im, s5_c_re, s5_c_im, s5_d, s5_w_glu, s5_b_glu, rw_mu, rw_w0, rw_w_up, rw_a0, rw_a_up, rw_g_up, rw_k_k, rw_k_a, rw_r_k, rw_gn_g, rw_gn_b, w_out, ln1_g, ln1_b, peer_w_q, peer_keys1, peer_keys2, peer_u, peer_v, ln2_g, ln2_b):
    bsz, seq, dim = x.shape
    h = x.reshape(bsz * seq, dim)
    for l in range(w_in.shape[0]):
        h = _layer(
            h, bsz, seq, w_in[l],
            (s5_lam_re[l], s5_lam_im[l], s5_log_step[l], s5_b_re[l], s5_b_im[l], s5_c_re[l],
             s5_c_im[l], s5_d[l], s5_w_glu[l], s5_b_glu[l]),
            (rw_mu[l], rw_w0[l], rw_w_up[l], rw_a0[l], rw_a_up[l], rw_g_up[l], rw_k_k[l],
             rw_k_a[l], rw_r_k[l], rw_gn_g[l], rw_gn_b[l]),
            w_out[l], (ln1_g[l], ln1_b[l]),
            (peer_w_q[l], peer_keys1[l], peer_keys2[l], peer_u[l], peer_v[l]),
            (ln2_g[l], ln2_b[l]))
    return h.reshape(bsz, seq, dim)
```

```python
import functools
import math

import jax
import jax.numpy as jnp
from jax import lax
from jax.experimental import pallas as pl
from jax.experimental.pallas import tpu as pltpu

F32 = jnp.float32
BF16 = jnp.bfloat16

D_MODEL = 2048
D_SSM = 1024
D_RWKV = 1024
S5_CH = 16
S5_GROUPS = D_SSM // S5_CH
S5_STATE = 64
S5_CHUNK = 16
RW_HEAD = 64
RW_HEADS = D_RWKV // RW_HEAD
RW_W_LORA = 64
RW_A_LORA = 64
RW_G_LORA = 160
RW_LORA = RW_W_LORA + RW_A_LORA + RW_G_LORA
RW_LORA_PAD = 384
RW_CHUNK = 64
PEER_HEADS = 8
PEER_NKEYS = 128
PEER_EXPERTS = PEER_NKEYS * PEER_NKEYS
PEER_HALF = 128
PEER_TOPK = 16
DEPTH = 1
DEEPNORM_ALPHA = (2.0 * DEPTH) ** 0.25
LN_EPS = 1e-5
RW_GN_EPS = 64e-5
NEG_INF = float("-inf")

VMEM_LIMIT = 56 * 1024 * 1024


def _cparams(sem):
    return pltpu.CompilerParams(dimension_semantics=sem, vmem_limit_bytes=VMEM_LIMIT)


def _dot(a, b):
    return jnp.dot(a.astype(BF16), b.astype(BF16), preferred_element_type=F32)


def _dot_nt(a, b):
    return lax.dot_general(a.astype(BF16), b.astype(BF16), (((1,), (1,)), ((), ())),
                           preferred_element_type=F32)


def _dot_tn(a, b):
    return lax.dot_general(a.astype(BF16), b.astype(BF16), (((0,), (0,)), ((), ())),
                           preferred_element_type=F32)


def _gelu(x):
    c = math.sqrt(2.0 / math.pi)
    return 0.5 * x * (1.0 + jnp.tanh(c * (x + 0.044715 * (x * x * x))))


def _sigmoid(x):
    return 1.0 / (1.0 + jnp.exp(-x))


def _mm_kernel(a_ref, b_ref, o_ref):
    o_ref[...] = jnp.dot(a_ref[...], b_ref[...],
                         preferred_element_type=F32).astype(o_ref.dtype)


def _matmul(a, b, out_dtype, tm, tn):
    m, k = a.shape
    n = b.shape[1]
    tm = min(tm, m)
    return pl.pallas_call(
        _mm_kernel,
        grid=(m // tm, n // tn),
        in_specs=[pl.BlockSpec((tm, k), lambda i, j: (i, 0)),
                  pl.BlockSpec((k, tn), lambda i, j: (0, j))],
        out_specs=pl.BlockSpec((tm, tn), lambda i, j: (i, j)),
        out_shape=jax.ShapeDtypeStruct((m, n), out_dtype),
        compiler_params=_cparams(("parallel", "parallel")),
    )(a, b)


def _s5_kernel(u_ref, mt_ref, bc_ref, cc_ref, ar_ref, ai_ref, d_ref, o_ref, *, nc):
    c = S5_CHUNK
    rows = u_ref.shape[0] // c
    groups = u_ref.shape[1] // S5_CH
    xs = [u_ref[pl.ds(tl, rows, stride=c), :] for tl in range(c)]
    cidx = lax.broadcasted_iota(jnp.int32, (rows, 2 * S5_STATE), 0) & (nc - 1)
    ys = []
    for g in range(groups):
        lanes = slice(g * S5_CH, (g + 1) * S5_CH)
        u = jnp.concatenate([x[:, lanes] for x in xs], axis=1).astype(BF16)
        y = jnp.dot(u, mt_ref[g], preferred_element_type=F32)
        s = jnp.dot(u, bc_ref[g], preferred_element_type=F32)
        d, k = 1, 0
        while d < nc:
            sh = jnp.where(cidx >= d, pltpu.roll(s, d, axis=0), 0.0)
            shs = pltpu.roll(sh, S5_STATE, axis=1)
            s = s + sh * ar_ref[g, k:k + 1, :] + shs * ai_ref[g, k:k + 1, :]
            d, k = d * 2, k + 1
        sp = jnp.where(cidx >= 1, pltpu.roll(s, 1, axis=0), 0.0)
        ys.append(y + jnp.dot(sp.astype(BF16), cc_ref[g], preferred_element_type=F32))
    for tl in range(c):
        y = jnp.concatenate([yg[:, tl * S5_CH:(tl + 1) * S5_CH] for yg in ys], axis=1)
        o_ref[pl.ds(tl, rows, stride=c), :] = _gelu(y + d_ref[...] * xs[tl])


def _s5_params(lam_re, lam_im, log_step, b_re, b_im, c_re, c_im, d_skip, nc):
    hi = lax.Precision.HIGHEST
    c = S5_CHUNK
    step = jnp.exp(log_step)[:, None]
    a_re = jnp.exp(lam_re * step) * jnp.cos(lam_im * step)
    a_im = jnp.exp(lam_re * step) * jnp.sin(lam_im * step)
    den = lam_re * lam_re + lam_im * lam_im
    f_re = ((a_re - 1.0) * lam_re + a_im * lam_im) / den
    f_im = (a_im * lam_re - (a_re - 1.0) * lam_im) / den
    bb_re = f_re[..., None] * b_re - f_im[..., None] * b_im
    bb_im = f_re[..., None] * b_im + f_im[..., None] * b_re

    def power(j):
        jj = j[None, :, None]
        mag = jnp.exp(lam_re[:, None, :] * step[:, None, :] * jj)
        ang = lam_im[:, None, :] * step[:, None, :] * jj
        return mag * jnp.cos(ang), mag * jnp.sin(ang)

    pw_re, pw_im = power(jnp.arange(c + 1, dtype=F32))
    ce_re = c_re[:, None] * pw_re[:, :, None, :] - c_im[:, None] * pw_im[:, :, None, :]
    ce_im = c_re[:, None] * pw_im[:, :, None, :] + c_im[:, None] * pw_re[:, :, None, :]
    kern = (jnp.einsum('gjop,gpi->gjoi', ce_re[:, :c], bb_re, precision=hi)
            - jnp.einsum('gjop,gpi->gjoi', ce_im[:, :c], bb_im, precision=hi))
    t_idx = jnp.arange(c)
    lag = t_idx[None, :] - t_idx[:, None]
    kt = jnp.take(kern, jnp.clip(lag, 0, c - 1), axis=1)
    kt = jnp.where((lag >= 0)[None, :, :, None, None], kt, 0.0)
    g = lam_re.shape[0]
    mt = jnp.transpose(kt, (0, 1, 4, 2, 3)).reshape(g, c * S5_CH, c * S5_CH)
    rv_re, rv_im = pw_re[:, c - 1::-1][:, :c], pw_im[:, c - 1::-1][:, :c]
    bc_r = rv_re[:, :, None, :] * jnp.transpose(bb_re, (0, 2, 1))[:, None] \
        - rv_im[:, :, None, :] * jnp.transpose(bb_im, (0, 2, 1))[:, None]
    bc_i = rv_re[:, :, None, :] * jnp.transpose(bb_im, (0, 2, 1))[:, None] \
        + rv_im[:, :, None, :] * jnp.transpose(bb_re, (0, 2, 1))[:, None]
    bc = jnp.concatenate([bc_r, bc_i], axis=-1).reshape(g, c * S5_CH, 2 * S5_STATE)
    cc_r = jnp.transpose(ce_re[:, 1:], (0, 3, 1, 2)).reshape(g, S5_STATE, c * S5_CH)
    cc_i = -jnp.transpose(ce_im[:, 1:], (0, 3, 1, 2)).reshape(g, S5_STATE, c * S5_CH)
    cc = jnp.concatenate([cc_r, cc_i], axis=1)
    sc_re, sc_im = power(c * (2.0 ** jnp.arange(8, dtype=F32)))
    ar = jnp.concatenate([sc_re, sc_re], axis=-1)
    ai = jnp.concatenate([-sc_im, sc_im], axis=-1)
    return mt.astype(BF16), bc.astype(BF16), cc.astype(BF16), ar, ai, d_skip[None, :]


def _s5_scan(u_tok, params, bsz, seq, gpb=8):
    mt, bc, cc, ar, ai, d_row = params
    nc = seq // S5_CHUNK
    t = bsz * seq
    w = S5_CHUNK * S5_CH
    lanes = gpb * S5_CH
    tok = pl.BlockSpec((t, lanes), lambda i: (0, i))
    spec3 = lambda a, b: pl.BlockSpec((gpb, a, b), lambda i: (i, 0, 0))
    return pl.pallas_call(
        functools.partial(_s5_kernel, nc=nc),
        grid=(S5_GROUPS // gpb,),
        in_specs=[tok, spec3(w, w), spec3(w, 2 * S5_STATE), spec3(2 * S5_STATE, w),
                  spec3(8, 2 * S5_STATE), spec3(8, 2 * S5_STATE),
                  pl.BlockSpec((1, lanes), lambda i: (0, i))],
        out_specs=tok,
        out_shape=jax.ShapeDtypeStruct((t, D_SSM), F32),
        compiler_params=_cparams(("parallel",)),
    )(u_tok, mt, bc, cc, ar, ai, d_row)


def _glu_kernel(y_ref, w_ref, b_ref, o_ref):
    y = y_ref[...]
    z = jnp.dot(y.astype(BF16), w_ref[...], preferred_element_type=F32) + b_ref[...]
    o_ref[...] = (y * _sigmoid(z)).astype(o_ref.dtype)


def _glu(y, w, b, tm=512):
    m, n = y.shape
    tm = min(tm, m)
    return pl.pallas_call(
        _glu_kernel,
        grid=(m // tm,),
        in_specs=[pl.BlockSpec((tm, n), lambda i: (i, 0)),
                  pl.BlockSpec((n, n), lambda i: (0, 0)),
                  pl.BlockSpec((1, n), lambda i: (0, 0))],
        out_specs=pl.BlockSpec((tm, n), lambda i: (i, 0)),
        out_shape=jax.ShapeDtypeStruct((m, n), BF16),
        compiler_params=_cparams(("parallel",)),
    )(y, w, b)


def _rw_prep_kernel(p_ref, pp_ref, l_ref, lp_ref, mu_ref, mul_ref, w0_ref, a0_ref,
                    kk_ref, ka_ref, wup_ref, aup_ref, gup_ref,
                    r_ref, k_ref, v_ref, kkr_ref, a_ref, lw_ref, g_ref, *, tiles_per_seq):
    first = (pl.program_id(0) % tiles_per_seq) == 0

    def shifted(cur_ref, prev_ref, mu):
        cur = cur_ref[...].astype(F32)
        prev_row = jnp.where(first, 0.0, prev_ref[15:16, :].astype(F32))
        row = lax.broadcasted_iota(jnp.int32, cur.shape, 0)
        prev = jnp.where(row == 0, prev_row, pltpu.roll(cur, 1, axis=0))
        return cur + mu * (prev - cur)

    p = shifted(p_ref, pp_ref, mu_ref[...])
    lo = shifted(l_ref, lp_ref, mul_ref[...])
    r = p[:, :D_RWKV]
    k = p[:, D_RWKV:2 * D_RWKV]
    v = p[:, 2 * D_RWKV:]
    w_pre = w0_ref[...] + _dot(jnp.tanh(lo), wup_ref[...])
    a = _sigmoid(a0_ref[...] + _dot(lo, aup_ref[...]))
    g = _dot(_sigmoid(lo), gup_ref[...])
    z = -w_pre
    softplus = jnp.maximum(z, 0.0) + jnp.log(1.0 + jnp.exp(-jnp.abs(z)))
    w = -softplus - 0.5
    r_ref[...] = r.astype(r_ref.dtype)
    k_ref[...] = (k * (1.0 + (a - 1.0) * ka_ref[...])).astype(k_ref.dtype)
    v_ref[...] = v.astype(v_ref.dtype)
    kkr_ref[...] = (k * kk_ref[...]).astype(kkr_ref.dtype)
    a_ref[...] = a.astype(a_ref.dtype)
    lw_ref[...] = -jnp.exp(w)
    g_ref[...] = g.astype(g_ref.dtype)


def _rw_prep(p_rkv, p_lora, mu_rkv, mu_lora, w0, a0, k_k, k_a, wup, aup, gup, seq, tm=256):
    t = p_rkv.shape[0]
    tm = min(tm, seq)
    n3 = 3 * D_RWKV
    row = lambda n: pl.BlockSpec((1, n), lambda i: (0, 0))
    full = lambda a, b: pl.BlockSpec((a, b), lambda i: (0, 0))
    prev = lambda n: pl.BlockSpec((16, n), lambda i: (jnp.maximum(i * (tm // 16) - 1, 0), 0))
    out = lambda dt: jax.ShapeDtypeStruct((t, D_RWKV), dt)
    ospec = pl.BlockSpec((tm, D_RWKV), lambda i: (i, 0))
    return pl.pallas_call(
        functools.partial(_rw_prep_kernel, tiles_per_seq=seq // tm),
        grid=(t // tm,),
        in_specs=[pl.BlockSpec((tm, n3), lambda i: (i, 0)), prev(n3),
                  pl.BlockSpec((tm, RW_LORA_PAD), lambda i: (i, 0)), prev(RW_LORA_PAD),
                  row(n3), row(RW_LORA_PAD), row(D_RWKV), row(D_RWKV), row(D_RWKV), row(D_RWKV),
                  full(RW_LORA_PAD, D_RWKV), full(RW_LORA_PAD, D_RWKV), full(RW_LORA_PAD, D_RWKV)],
        out_specs=[ospec] * 7,
        out_shape=[out(BF16)] * 5 + [out(F32), out(BF16)],
        compiler_params=_cparams(("parallel",)),
    )(p_rkv, p_rkv, p_lora, p_lora, mu_rkv, mu_lora, w0, a0, k_k, k_a, wup, aup, gup)


def _rw_chunk_kernel(r_ref, k_ref, v_ref, kkr_ref, a_ref, lw_ref, rk_ref,
                     q_ref, oi_ref, bonus_ref, m_ref, n_ref):
    c = RW_CHUNK
    hd = RW_HEAD
    heads = range(r_ref.shape[1] // hd)
    ri = lax.broadcasted_iota(jnp.int32, (c, c), 0)
    ci = lax.broadcasted_iota(jnp.int32, (c, c), 1)
    tri_incl = (ci <= ri)
    tri_strict = (ci < ri)
    ltri = tri_incl.astype(BF16)
    eye = (ci == ri).astype(F32)
    lw = lw_ref[...]
    lw_hi = lw.astype(BF16)
    lw_lo = (lw - lw_hi.astype(F32)).astype(BF16)
    cl = (jnp.dot(ltri, lw_hi, preferred_element_type=F32)
          + jnp.dot(ltri, lw_lo, preferred_element_type=F32))
    cl_end = cl[c - 1:c, :]
    r = r_ref[...].astype(F32)
    k = k_ref[...].astype(F32)
    rt_d = r * jnp.exp(cl)
    e_neg_d = jnp.exp(-cl)
    kt_d = k * e_neg_d
    e_prev_d = jnp.exp(cl - lw)
    e_end_d = jnp.exp(cl_end - cl)
    kte_d = k * e_end_d
    gam_d = jnp.exp(cl_end)
    rk_d = r * k * rk_ref[0:1, :]
    kkr_d = kkr_ref[...].astype(F32)
    a_d = a_ref[...].astype(F32)
    sl = [slice(h * hd, (h + 1) * hd) for h in heads]
    v_d = v_ref[...]
    v = [v_d[:, s] for s in sl]
    rt = [rt_d[:, s] for s in sl]
    kkt, bt, bte = [], [], []
    for h in heads:
        kkr = kkr_d[:, sl[h]]
        nrm = jnp.sqrt(jnp.sum(kkr * kkr, axis=-1, keepdims=True))
        kk = kkr / jnp.maximum(nrm, 1e-12)
        b = kk * a_d[:, sl[h]]
        kkt.append(kk * e_prev_d[:, sl[h]])
        bt.append(b * e_neg_d[:, sl[h]])
        bte.append(b * e_end_d[:, sl[h]])
    a4 = [_dot_nt(jnp.concatenate([kkt[h], rt[h]], axis=0),
                  jnp.concatenate([kt_d[:, sl[h]], bt[h]], axis=0)) for h in heads]
    a_kb = [jnp.where(tri_strict, a4[h][:c, c:], 0.0) for h in heads]
    a_rb = [jnp.where(tri_incl, a4[h][c:, c:], 0.0) for h in heads]
    a_kr = [jnp.concatenate([jnp.where(tri_strict, a4[h][:c, :c], 0.0),
                             jnp.where(tri_incl, a4[h][c:, :c], 0.0)], axis=0) for h in heads]
    x = [eye - a_kb[h] for h in heads]
    pw = [_dot(a_kb[h], a_kb[h]) for h in heads]
    n_sq = int(math.log2(c)) - 1
    for it in range(n_sq):
        if it + 1 < n_sq:
            xp = [_dot(jnp.concatenate([x[h], pw[h]], axis=0), pw[h]) for h in heads]
            x = [x[h] + xp[h][:c] for h in heads]
            pw = [xp[h][c:] for h in heads]
        else:
            x = [x[h] + _dot(x[h], pw[h]) for h in heads]
    av = [_dot(a_kr[h], v[h]) for h in heads]
    wu = [_dot(x[h], jnp.concatenate([kkt[h], av[h][:c]], axis=1)) for h in heads]
    rb = [_dot(a_rb[h], wu[h]) for h in heads]
    wub = [_dot_tn(wu[h], bte[h]) for h in heads]
    vk = [_dot_tn(v[h], kte_d[:, sl[h]]) for h in heads]
    ji = lax.broadcasted_iota(jnp.int32, (hd, hd), 0)
    jo = lax.broadcasted_iota(jnp.int32, (hd, hd), 1)
    for h in heads:
        m_ref[0, h] = jnp.where(ji == jo, gam_d[:, sl[h]], 0.0) - wub[h][:hd]
        n_ref[0, h] = vk[h] - wub[h][hd:]
    q_ref[...] = jnp.concatenate([rt[h] - rb[h][:, :hd] for h in heads], axis=1)
    oi_ref[...] = jnp.concatenate([av[h][c:] - rb[h][:, hd:] for h in heads], axis=1)
    bonus_ref[...] = jnp.concatenate(
        [jnp.sum(rk_d[:, sl[h]], axis=-1, keepdims=True) * v[h] for h in heads], axis=1)


def _rw_chunks(r, k, v, kkr, a, lw, r_k, hp=RW_HEADS):
    t = r.shape[0]
    c = RW_CHUNK
    nch = t // c
    wdt = hp * RW_HEAD
    blk = pl.BlockSpec((c, wdt), lambda i, j: (i, j))
    mat = pl.BlockSpec((1, hp, RW_HEAD, RW_HEAD), lambda i, j: (i, j, 0, 0))
    tok = jax.ShapeDtypeStruct((t, D_RWKV), F32)
    mshape = jax.ShapeDtypeStruct((nch, RW_HEADS, RW_HEAD, RW_HEAD), F32)
    return pl.pallas_call(
        _rw_chunk_kernel,
        grid=(nch, RW_HEADS // hp),
        in_specs=[blk] * 6 + [pl.BlockSpec((1, wdt), lambda i, j: (0, j))],
        out_specs=[blk, blk, blk, mat, mat],
        out_shape=[tok, tok, tok, mshape, mshape],
        compiler_params=_cparams(("parallel", "parallel")),
    )(r, k, v, kkr, a, lw, r_k)


def _rw_seq_kernel(q_ref, oi_ref, bonus_ref, g_ref, m_ref, n_ref, gg_ref, gb_ref,
                   o_ref, st_ref):
    @pl.when(pl.program_id(0) == 0)
    def _():
        st_ref[...] = jnp.zeros_like(st_ref)

    hd = RW_HEAD
    heads = range(RW_HEADS)
    sl = [slice(h * hd, (h + 1) * hd) for h in heads]

    def per_batch(b, carry):
        s = [st_ref[b, h] for h in heads]
        q = q_ref[b]
        o = [_dot_nt(q[:, sl[h]], s[h]) for h in heads]
        for h in heads:
            st_ref[b, h] = _dot(s[h], m_ref[b, 0, h]) + n_ref[b, 0, h]
        oi = oi_ref[b]
        outs = []
        for h in heads:
            oh = o[h] + oi[:, sl[h]]
            mean = jnp.mean(oh, axis=-1, keepdims=True)
            ctr = oh - mean
            var = jnp.mean(jnp.square(ctr), axis=-1, keepdims=True)
            outs.append(ctr * lax.rsqrt(var + RW_GN_EPS))
        o = jnp.concatenate(outs, axis=1) * gg_ref[...] + gb_ref[...] + bonus_ref[b]
        o_ref[b] = (o * g_ref[b].astype(F32)).astype(o_ref.dtype)
        return carry

    lax.fori_loop(0, q_ref.shape[0], per_batch, 0)


def _rw_seq(q, oi, bonus, g, m, n, gn_g, gn_b, bsz, seq):
    c = RW_CHUNK
    nch = seq // c
    r3 = lambda x: x.reshape(bsz, seq, D_RWKV)
    r5 = lambda x: x.reshape(bsz, nch, RW_HEADS, RW_HEAD, RW_HEAD)
    tok = pl.BlockSpec((bsz, c, D_RWKV), lambda i: (0, i, 0))
    mat = pl.BlockSpec((bsz, 1, RW_HEADS, RW_HEAD, RW_HEAD), lambda i: (0, i, 0, 0, 0))
    row = pl.BlockSpec((1, D_RWKV), lambda i: (0, 0))
    y = pl.pallas_call(
        _rw_seq_kernel,
        grid=(nch,),
        in_specs=[tok, tok, tok, tok, mat, mat, row, row],
        out_specs=tok,
        out_shape=jax.ShapeDtypeStruct((bsz, seq, D_RWKV), BF16),
        scratch_shapes=[pltpu.VMEM((bsz, RW_HEADS, RW_HEAD, RW_HEAD), F32)],
        compiler_params=_cparams(("arbitrary",)),
    )(r3(q), r3(oi), r3(bonus), r3(g), r5(m), r5(n), gn_g, gn_b)
    return y.reshape(bsz * seq, D_RWKV)


def _layer_norm(x, g, b):
    mu = jnp.mean(x, axis=-1, keepdims=True)
    var = jnp.mean(jnp.square(x - mu), axis=-1, keepdims=True)
    return (x - mu) * lax.rsqrt(var + LN_EPS) * g + b


def _outproj_kernel(ys_ref, yr_ref, w1_ref, w2_ref, x_ref, g_ref, b_ref, h_ref, hb_ref):
    mix = (jnp.dot(ys_ref[...], w1_ref[...], preferred_element_type=F32)
           + jnp.dot(yr_ref[...], w2_ref[...], preferred_element_type=F32))
    h = _layer_norm(DEEPNORM_ALPHA * x_ref[...] + mix, g_ref[...], b_ref[...])
    h_ref[...] = h
    hb_ref[...] = h.astype(BF16)


def _outproj(ys, yr, w_out, x, g, b, tm=256):
    t = x.shape[0]
    tm = min(tm, t)
    half = pl.BlockSpec((tm, D_SSM), lambda i: (i, 0))
    full = pl.BlockSpec((tm, D_MODEL), lambda i: (i, 0))
    row = pl.BlockSpec((1, D_MODEL), lambda i: (0, 0))
    return pl.pallas_call(
        _outproj_kernel,
        grid=(t // tm,),
        in_specs=[half, half,
                  pl.BlockSpec((D_SSM, D_MODEL), lambda i: (0, 0)),
                  pl.BlockSpec((D_RWKV, D_MODEL), lambda i: (1, 0)),
                  full, row, row],
        out_specs=[full, full],
        out_shape=[jax.ShapeDtypeStruct((t, D_MODEL), F32),
                   jax.ShapeDtypeStruct((t, D_MODEL), BF16)],
        compiler_params=_cparams(("parallel",)),
    )(ys, yr, w_out, w_out, x, g, b)


def _peer_scores_kernel(h_ref, wq_ref, keys_ref, s_ref):
    q = jnp.dot(h_ref[...], wq_ref[...], preferred_element_type=F32)
    for blk in range(2 * PEER_HEADS):
        qb = q[:, blk * PEER_HALF:(blk + 1) * PEER_HALF]
        s_ref[blk] = _dot_nt(keys_ref[blk], qb)


def _peer_scores(hb, wq, keys, tm=256):
    t = hb.shape[0]
    tm = min(tm, t)
    nb = 2 * PEER_HEADS
    return pl.pallas_call(
        _peer_scores_kernel,
        grid=(t // tm,),
        in_specs=[pl.BlockSpec((tm, D_MODEL), lambda i: (i, 0)),
                  pl.BlockSpec((D_MODEL, D_MODEL), lambda i: (0, 0)),
                  pl.BlockSpec((nb, PEER_NKEYS, PEER_HALF), lambda i: (0, 0, 0))],
        out_specs=pl.BlockSpec((nb, PEER_NKEYS, tm), lambda i: (0, 0, i)),
        out_shape=jax.ShapeDtypeStruct((nb, PEER_NKEYS, t), F32),
        compiler_params=_cparams(("parallel",)),
    )(hb, wq, keys)


NO_RANK = 127.0


def _extract_top(x, n, want_rank=False):
    vals = []
    rank = jnp.full(x.shape, NO_RANK, F32) if want_rank else None
    for i in range(n):
        m = jnp.max(x, axis=0, keepdims=True)
        vals.append(m)
        hit = x == m
        if want_rank:
            rank = jnp.where(hit, float(i), rank)
        x = jnp.where(hit, NEG_INF, x)
    return vals, rank


def _peer_topk_kernel(s_ref, nsel_ref, e1_ref, rank2_ref, e2_ref):
    k = PEER_TOPK

    def per_head(h, carry):
        s1 = s_ref[2 * h]
        s2 = s_ref[2 * h + 1]
        ta, _ = _extract_top(s1, k)
        tb, rank2 = _extract_top(s2, k, want_rank=True)
        tbs = jnp.concatenate(tb, axis=0)
        cands = [ta[i] + tbs[0:k // (i + 1), :] for i in range(k)]
        n_c = sum(k // (i + 1) for i in range(k))
        pad = (-n_c) % 8
        padded = cands + ([jnp.full((pad, s1.shape[1]), NEG_INF, F32)] if pad else [])
        best, _ = _extract_top(jnp.concatenate(padded, axis=0), k)
        m0 = best[0]
        tau = best[k - 1]
        z = jnp.zeros_like(m0)
        for bv in best:
            z = z + jnp.exp(bv - m0)
        nsel = jnp.zeros_like(s1)
        for i in range(k):
            cnt = jnp.sum(jnp.where(cands[i] >= tau, 1.0, 0.0), axis=0, keepdims=True)
            nsel = jnp.where(s1 == ta[i], cnt, nsel)
        nsel_ref[h] = nsel
        e1_ref[h] = jnp.exp(s1 - ta[0]) / z
        rank2_ref[h] = rank2.astype(BF16)
        e2_ref[h] = jnp.exp(s2 - tb[0]).astype(BF16)
        return carry

    lax.fori_loop(0, PEER_HEADS, per_head, 0)


def _peer_topk(scores, tt=256):
    nb, nk, t = scores.shape
    tt = min(tt, t)
    big = pl.BlockSpec((PEER_HEADS, nk, tt), lambda i: (0, 0, i))
    f32 = jax.ShapeDtypeStruct((PEER_HEADS, nk, t), F32)
    b16 = jax.ShapeDtypeStruct((PEER_HEADS, nk, t), BF16)
    return pl.pallas_call(
        _peer_topk_kernel,
        grid=(t // tt,),
        in_specs=[pl.BlockSpec((nb, nk, tt), lambda i: (0, 0, i))],
        out_specs=[big, big, big, big],
        out_shape=[f32, f32, b16, b16],
        compiler_params=_cparams(("parallel",)),
    )(scores)


def _peer_dense_kernel(ht_ref, u_ref, va_ref, vb_ref, nsel_ref, e1_ref, rank2_ref, e2_ref,
                       o_ref, hsa_ref, hsb_ref, *, q, nj):
    j = pl.program_id(1)

    @pl.when(j == 0)
    def _():
        o_ref[...] = jnp.zeros_like(o_ref)
        hsb_ref[...] = jnp.zeros_like(hsb_ref)

    nk = PEER_NKEYS
    tm = ht_ref.shape[1]
    ht = ht_ref[...]
    live = jnp.where(j < nj, 1.0, 0.0)

    def phase(half, dst_ref, src_ref, v_ref):
        wv = D_MODEL // q
        for qi in range(q):
            row = half * q + qi
            cols = slice(qi * wv, (qi + 1) * wv)
            o_ref[:, cols] += jnp.dot(src_ref[...], v_ref[:, cols], preferred_element_type=F32)
            act = jnp.dot(u_ref[row * nk:(row + 1) * nk, :], ht, preferred_element_type=F32)
            gate = jnp.zeros((nk, tm), BF16)
            for h in range(PEER_HEADS):
                nsel = (nsel_ref[h, 0, row:row + 1, :] * live).astype(BF16)
                e1 = e1_ref[h, 0, row:row + 1, :].astype(BF16)
                gate = gate + jnp.where(rank2_ref[h] < nsel, e1 * e2_ref[h], 0.0)
            c0 = math.sqrt(2.0 / math.pi)
            inner = act * (c0 + (c0 * 0.044715) * (act * act))
            half_act = 0.5 * act.astype(BF16)
            hid = (half_act + half_act * jnp.tanh(inner.astype(BF16))) * gate
            dst_ref[:, qi * nk:(qi + 1) * nk] = hid.T

    phase(0, hsa_ref, hsb_ref, vb_ref)
    phase(1, hsb_ref, hsa_ref, va_ref)


def _peer_dense(ht, u_tab, v_tab, nsel, e1, rank2, e2, tm=512, te=1024):
    t = ht.shape[1]
    tm = min(tm, t)
    half = te // 2
    q = half // PEER_NKEYS
    ng = PEER_NKEYS // (2 * q)
    nj = PEER_EXPERTS // te
    rows = lambda a: a.reshape(PEER_HEADS, ng, 2 * q, t)
    cur = lambda j: jnp.minimum(j, nj - 1)
    row_spec = pl.BlockSpec((PEER_HEADS, 1, 2 * q, tm), lambda i, j: (0, cur(j), 0, i))
    key_spec = pl.BlockSpec((PEER_HEADS, PEER_NKEYS, tm), lambda i, j: (0, 0, i))
    return pl.pallas_call(
        functools.partial(_peer_dense_kernel, q=q, nj=nj),
        grid=(t // tm, nj + 1),
        in_specs=[pl.BlockSpec((D_MODEL, tm), lambda i, j: (0, i)),
                  pl.BlockSpec((te, D_MODEL), lambda i, j: (cur(j), 0)),
                  pl.BlockSpec((half, D_MODEL), lambda i, j: (2 * cur(j), 0)),
                  pl.BlockSpec((half, D_MODEL), lambda i, j: (jnp.maximum(2 * j - 1, 0), 0)),
                  row_spec, row_spec, key_spec, key_spec],
        out_specs=pl.BlockSpec((tm, D_MODEL), lambda i, j: (i, 0)),
        out_shape=jax.ShapeDtypeStruct((t, D_MODEL), F32),
        scratch_shapes=[pltpu.VMEM((tm, half), BF16), pltpu.VMEM((tm, half), BF16)],
        compiler_params=_cparams(("parallel", "arbitrary")),
    )(ht, u_tab, v_tab, v_tab, rows(nsel), rows(e1), rank2, e2)


def _ln_res_kernel(h_ref, f_ref, g_ref, b_ref, o_ref):
    o_ref[...] = _layer_norm(DEEPNORM_ALPHA * h_ref[...] + f_ref[...], g_ref[...], b_ref[...])


def _ln_res(h, f, g, b, tm=256):
    t = h.shape[0]
    tm = min(tm, t)
    full = pl.BlockSpec((tm, D_MODEL), lambda i: (i, 0))
    row = pl.BlockSpec((1, D_MODEL), lambda i: (0, 0))
    return pl.pallas_call(
        _ln_res_kernel,
        grid=(t // tm,),
        in_specs=[full, full, row, row],
        out_specs=full,
        out_shape=jax.ShapeDtypeStruct((t, D_MODEL), F32),
        compiler_params=_cparams(("parallel",)),
    )(h, f, g, b)


def _layer(h, bsz, seq, w_in, s5, rw, w_out, ln1, peer, ln2):
    (lam_re, lam_im, log_step, b_re, b_im, c_re, c_im, d_skip, w_glu, b_glu) = s5
    (mu, w0, w_up, a0, a_up, g_up, k_k, k_a, r_k, gn_g, gn_b) = rw
    (w_q, keys1, keys2, u_tab, v_tab) = peer
    hb = h.astype(BF16)
    n_rkv = 3 * D_RWKV
    w_in_b = w_in.astype(BF16)
    pad = RW_LORA_PAD - RW_LORA

    u = _matmul(hb, w_in_b[:, :D_SSM], F32, 512, 512)
    p_rkv = _matmul(hb, w_in_b[:, D_SSM:D_SSM + n_rkv], BF16, 512, 512)
    p_lora = _matmul(hb, jnp.pad(w_in_b[:, D_SSM + n_rkv:], ((0, 0), (0, pad))), F32, 512, RW_LORA_PAD)

    yg = _s5_scan(u, _s5_params(lam_re, lam_im, log_step, b_re, b_im, c_re, c_im, d_skip,
                                seq // S5_CHUNK), bsz, seq)
    y_ssm = _glu(yg, w_glu.astype(BF16), b_glu[None, :])

    mu_rkv = mu[None, :n_rkv]
    mu_lora = jnp.pad(mu[n_rkv:], (0, pad))[None, :]
    zrow = lambda n: jnp.zeros((n, D_RWKV), F32)
    wup = jnp.concatenate([w_up, zrow(RW_LORA_PAD - RW_W_LORA)], axis=0).astype(BF16)
    aup = jnp.concatenate([zrow(RW_W_LORA), a_up, zrow(RW_LORA_PAD - RW_W_LORA - RW_A_LORA)],
                          axis=0).astype(BF16)
    gup = jnp.concatenate([zrow(RW_W_LORA + RW_A_LORA), g_up, zrow(pad)], axis=0).astype(BF16)
    r, k, v, kkr, a, lw, g = _rw_prep(p_rkv, p_lora, mu_rkv, mu_lora, w0[None], a0[None],
                                      k_k[None], k_a[None], wup, aup, gup, seq)
    q, oi, bonus, m, n = _rw_chunks(r, k, v, kkr, a, lw, r_k.reshape(1, D_RWKV))
    y_rw = _rw_seq(q, oi, bonus, g, m, n, gn_g[None], gn_b[None], bsz, seq)

    h1, h1b = _outproj(y_ssm, y_rw, w_out.astype(BF16), h, ln1[0][None], ln1[1][None])

    keys = jnp.stack([keys1, keys2], axis=1).reshape(2 * PEER_HEADS, PEER_NKEYS, PEER_HALF)
    scores = _peer_scores(h1b, w_q.astype(BF16), keys.astype(BF16))
    nsel, e1, rank2, e2 = _peer_topk(scores)
    ffn = _peer_dense(h1b.T, u_tab.astype(BF16), v_tab.astype(BF16), nsel, e1, rank2, e2)
    return _ln_res(h1, ffn, ln2[0][None], ln2[1][None])


def kernel(x, w_in, s5_lam_re, s5_lam_im, s5_log_step, s5_b_re, s5_b_im, s5_c_re, s5_c_im, s5_d, s5_w_glu, s5_b_glu, rw_mu, rw_w0, rw_w_up, rw_a0, rw_a_up, rw_g_up, rw_k_k, rw_k_a, rw_r_k, rw_gn_g, rw_gn_b, w_out, ln1_g, ln1_b, peer_w_q, peer_keys1, peer_keys2, peer_u, peer_v, ln2_g, ln2_b):
    bsz, seq, dim = x.shape
    h = x.reshape(bsz * seq, dim)
    for l in range(w_in.shape[0]):
        h = _layer(
            h, bsz, seq, w_in[l],
            (s5_lam_re[l], s5_lam_im[l], s5_log_step[l], s5_b_re[l], s5_b_im[l], s5_c_re[l],
             s5_c_im[l], s5_d[l], s5_w_glu[l], s5_b_glu[l]),
            (rw_mu[l], rw_w0[l], rw_w_up[l], rw_a0[l], rw_a_up[l], rw_g_up[l], rw_k_k[l],
             rw_k_a[l], rw_r_k[l], rw_gn_g[l], rw_gn_b[l]),
            w_out[l], (ln1_g[l], ln1_b[l]),
            (peer_w_q[l], peer_keys1[l], peer_keys2[l], peer_u[l], peer_v[l]),
            (ln2_g[l], ln2_b[l]))
    return h.reshape(bsz, seq, dim)
```

```python
import functools
import math

import jax
import jax.numpy as jnp
from jax import lax
from jax.experimental import pallas as pl
from jax.experimental.pallas import tpu as pltpu

F32 = jnp.float32
BF16 = jnp.bfloat16

D_MODEL = 2048
D_SSM = 1024
D_RWKV = 1024
S5_CH = 16
S5_GROUPS = D_SSM // S5_CH
S5_STATE = 64
S5_CHUNK = 16
RW_HEAD = 64
RW_HEADS = D_RWKV // RW_HEAD
RW_W_LORA = 64
RW_A_LORA = 64
RW_G_LORA = 160
RW_LORA = RW_W_LORA + RW_A_LORA + RW_G_LORA
RW_LORA_PAD = 384
RW_CHUNK = 64
PEER_HEADS = 8
PEER_NKEYS = 128
PEER_EXPERTS = PEER_NKEYS * PEER_NKEYS
PEER_HALF = 128
PEER_TOPK = 16
DEPTH = 1
DEEPNORM_ALPHA = (2.0 * DEPTH) ** 0.25
LN_EPS = 1e-5
RW_GN_EPS = 64e-5
NEG_INF = float("-inf")

VMEM_LIMIT = 56 * 1024 * 1024


def _cparams(sem):
    return pltpu.CompilerParams(dimension_semantics=sem, vmem_limit_bytes=VMEM_LIMIT)


def _dot(a, b):
    return jnp.dot(a.astype(BF16), b.astype(BF16), preferred_element_type=F32)


def _dot_nt(a, b):
    return lax.dot_general(a.astype(BF16), b.astype(BF16), (((1,), (1,)), ((), ())),
                           preferred_element_type=F32)


def _dot_tn(a, b):
    return lax.dot_general(a.astype(BF16), b.astype(BF16), (((0,), (0,)), ((), ())),
                           preferred_element_type=F32)


def _gelu(x):
    c = math.sqrt(2.0 / math.pi)
    return 0.5 * x * (1.0 + jnp.tanh(c * (x + 0.044715 * (x * x * x))))


def _sigmoid(x):
    return 1.0 / (1.0 + jnp.exp(-x))


def _head_sum(x, ones_ref):
    x_hi = x.astype(BF16)
    x_lo = (x - x_hi.astype(F32)).astype(BF16)
    w = ones_ref.shape[0]
    return jnp.concatenate(
        [jnp.dot(x_hi[:, i:i + w], ones_ref[...], preferred_element_type=F32)
         + jnp.dot(x_lo[:, i:i + w], ones_ref[...], preferred_element_type=F32)
         for i in range(0, x.shape[1], w)], axis=1)


def _head_ones():
    hid = jnp.arange(4 * RW_HEAD) // RW_HEAD
    return (hid[:, None] == hid[None, :]).astype(BF16)


def _mm_kernel(a_ref, b_ref, o_ref):
    o_ref[...] = jnp.dot(a_ref[...], b_ref[...],
                         preferred_element_type=F32).astype(o_ref.dtype)


def _matmul(a, b, out_dtype, tm, tn):
    m, k = a.shape
    n = b.shape[1]
    tm = min(tm, m)
    return pl.pallas_call(
        _mm_kernel,
        grid=(m // tm, n // tn),
        in_specs=[pl.BlockSpec((tm, k), lambda i, j: (i, 0)),
                  pl.BlockSpec((k, tn), lambda i, j: (0, j))],
        out_specs=pl.BlockSpec((tm, tn), lambda i, j: (i, j)),
        out_shape=jax.ShapeDtypeStruct((m, n), out_dtype),
        compiler_params=_cparams(("parallel", "parallel")),
    )(a, b)


def _s5_kernel(u_ref, mt_ref, bc_ref, cc_ref, ar_ref, ai_ref, d_ref, o_ref, *, nc):
    c = S5_CHUNK
    rows = u_ref.shape[0] // c
    groups = u_ref.shape[1] // S5_CH
    xs = [u_ref[pl.ds(tl, rows, stride=c), :] for tl in range(c)]
    cidx = lax.broadcasted_iota(jnp.int32, (rows, 2 * S5_STATE), 0) & (nc - 1)
    ys = []
    for g in range(groups):
        lanes = slice(g * S5_CH, (g + 1) * S5_CH)
        u = jnp.concatenate([x[:, lanes] for x in xs], axis=1).astype(BF16)
        y = jnp.dot(u, mt_ref[g], preferred_element_type=F32)
        s = jnp.dot(u, bc_ref[g], preferred_element_type=F32)
        d, k = 1, 0
        while d < nc:
            sh = jnp.where(cidx >= d, pltpu.roll(s, d, axis=0), 0.0)
            shs = pltpu.roll(sh, S5_STATE, axis=1)
            s = s + sh * ar_ref[g, k:k + 1, :] + shs * ai_ref[g, k:k + 1, :]
            d, k = d * 2, k + 1
        sp = jnp.where(cidx >= 1, pltpu.roll(s, 1, axis=0), 0.0)
        ys.append(y + jnp.dot(sp.astype(BF16), cc_ref[g], preferred_element_type=F32))
    for tl in range(c):
        y = jnp.concatenate([yg[:, tl * S5_CH:(tl + 1) * S5_CH] for yg in ys], axis=1)
        o_ref[pl.ds(tl, rows, stride=c), :] = _gelu(y + d_ref[...] * xs[tl])


def _s5_params(lam_re, lam_im, log_step, b_re, b_im, c_re, c_im, d_skip, nc):
    hi = lax.Precision.HIGHEST
    c = S5_CHUNK
    step = jnp.exp(log_step)[:, None]
    a_re = jnp.exp(lam_re * step) * jnp.cos(lam_im * step)
    a_im = jnp.exp(lam_re * step) * jnp.sin(lam_im * step)
    den = lam_re * lam_re + lam_im * lam_im
    f_re = ((a_re - 1.0) * lam_re + a_im * lam_im) / den
    f_im = (a_im * lam_re - (a_re - 1.0) * lam_im) / den
    bb_re = f_re[..., None] * b_re - f_im[..., None] * b_im
    bb_im = f_re[..., None] * b_im + f_im[..., None] * b_re

    def power(j):
        jj = j[None, :, None]
        mag = jnp.exp(lam_re[:, None, :] * step[:, None, :] * jj)
        ang = lam_im[:, None, :] * step[:, None, :] * jj
        return mag * jnp.cos(ang), mag * jnp.sin(ang)

    pw_re, pw_im = power(jnp.arange(c + 1, dtype=F32))
    ce_re = c_re[:, None] * pw_re[:, :, None, :] - c_im[:, None] * pw_im[:, :, None, :]
    ce_im = c_re[:, None] * pw_im[:, :, None, :] + c_im[:, None] * pw_re[:, :, None, :]
    kern = (jnp.einsum('gjop,gpi->gjoi', ce_re[:, :c], bb_re, precision=hi)
            - jnp.einsum('gjop,gpi->gjoi', ce_im[:, :c], bb_im, precision=hi))
    t_idx = jnp.arange(c)
    lag = t_idx[None, :] - t_idx[:, None]
    kt = jnp.take(kern, jnp.clip(lag, 0, c - 1), axis=1)
    kt = jnp.where((lag >= 0)[None, :, :, None, None], kt, 0.0)
    g = lam_re.shape[0]
    mt = jnp.transpose(kt, (0, 1, 4, 2, 3)).reshape(g, c * S5_CH, c * S5_CH)
    rv_re, rv_im = pw_re[:, c - 1::-1][:, :c], pw_im[:, c - 1::-1][:, :c]
    bc_r = rv_re[:, :, None, :] * jnp.transpose(bb_re, (0, 2, 1))[:, None] \
        - rv_im[:, :, None, :] * jnp.transpose(bb_im, (0, 2, 1))[:, None]
    bc_i = rv_re[:, :, None, :] * jnp.transpose(bb_im, (0, 2, 1))[:, None] \
        + rv_im[:, :, None, :] * jnp.transpose(bb_re, (0, 2, 1))[:, None]
    bc = jnp.concatenate([bc_r, bc_i], axis=-1).reshape(g, c * S5_CH, 2 * S5_STATE)
    cc_r = jnp.transpose(ce_re[:, 1:], (0, 3, 1, 2)).reshape(g, S5_STATE, c * S5_CH)
    cc_i = -jnp.transpose(ce_im[:, 1:], (0, 3, 1, 2)).reshape(g, S5_STATE, c * S5_CH)
    cc = jnp.concatenate([cc_r, cc_i], axis=1)
    sc_re, sc_im = power(c * (2.0 ** jnp.arange(8, dtype=F32)))
    ar = jnp.concatenate([sc_re, sc_re], axis=-1)
    ai = jnp.concatenate([-sc_im, sc_im], axis=-1)
    return mt.astype(BF16), bc.astype(BF16), cc.astype(BF16), ar, ai, d_skip[None, :]


def _s5_scan(u_tok, params, bsz, seq, gpb=8):
    mt, bc, cc, ar, ai, d_row = params
    nc = seq // S5_CHUNK
    t = bsz * seq
    w = S5_CHUNK * S5_CH
    lanes = gpb * S5_CH
    tok = pl.BlockSpec((t, lanes), lambda i: (0, i))
    spec3 = lambda a, b: pl.BlockSpec((gpb, a, b), lambda i: (i, 0, 0))
    return pl.pallas_call(
        functools.partial(_s5_kernel, nc=nc),
        grid=(S5_GROUPS // gpb,),
        in_specs=[tok, spec3(w, w), spec3(w, 2 * S5_STATE), spec3(2 * S5_STATE, w),
                  spec3(8, 2 * S5_STATE), spec3(8, 2 * S5_STATE),
                  pl.BlockSpec((1, lanes), lambda i: (0, i))],
        out_specs=tok,
        out_shape=jax.ShapeDtypeStruct((t, D_SSM), F32),
        compiler_params=_cparams(("parallel",)),
    )(u_tok, mt, bc, cc, ar, ai, d_row)


def _glu_kernel(y_ref, w_ref, b_ref, o_ref):
    y = y_ref[...]
    z = jnp.dot(y.astype(BF16), w_ref[...], preferred_element_type=F32) + b_ref[...]
    o_ref[...] = (y * _sigmoid(z)).astype(o_ref.dtype)


def _glu(y, w, b, tm=512):
    m, n = y.shape
    tm = min(tm, m)
    return pl.pallas_call(
        _glu_kernel,
        grid=(m // tm,),
        in_specs=[pl.BlockSpec((tm, n), lambda i: (i, 0)),
                  pl.BlockSpec((n, n), lambda i: (0, 0)),
                  pl.BlockSpec((1, n), lambda i: (0, 0))],
        out_specs=pl.BlockSpec((tm, n), lambda i: (i, 0)),
        out_shape=jax.ShapeDtypeStruct((m, n), BF16),
        compiler_params=_cparams(("parallel",)),
    )(y, w, b)


def _rw_prep_kernel(p_ref, pp_ref, l_ref, lp_ref, mu_ref, mul_ref, w0_ref, a0_ref,
                    kk_ref, ka_ref, wup_ref, aup_ref, gup_ref,
                    r_ref, k_ref, v_ref, kkr_ref, a_ref, lw_ref, g_ref, *, tiles_per_seq):
    first = (pl.program_id(0) % tiles_per_seq) == 0

    def shifted(cur_ref, prev_ref, mu):
        cur = cur_ref[...].astype(F32)
        prev_row = jnp.where(first, 0.0, prev_ref[15:16, :].astype(F32))
        row = lax.broadcasted_iota(jnp.int32, cur.shape, 0)
        prev = jnp.where(row == 0, prev_row, pltpu.roll(cur, 1, axis=0))
        return cur + mu * (prev - cur)

    p = shifted(p_ref, pp_ref, mu_ref[...])
    lo = shifted(l_ref, lp_ref, mul_ref[...])
    r = p[:, :D_RWKV]
    k = p[:, D_RWKV:2 * D_RWKV]
    v = p[:, 2 * D_RWKV:]
    w_pre = w0_ref[...] + _dot(jnp.tanh(lo), wup_ref[...])
    a = _sigmoid(a0_ref[...] + _dot(lo, aup_ref[...]))
    g = _dot(_sigmoid(lo), gup_ref[...])
    z = -w_pre
    softplus = jnp.maximum(z, 0.0) + jnp.log(1.0 + jnp.exp(-jnp.abs(z)))
    w = -softplus - 0.5
    r_ref[...] = r.astype(r_ref.dtype)
    k_ref[...] = (k * (1.0 + (a - 1.0) * ka_ref[...])).astype(k_ref.dtype)
    v_ref[...] = v.astype(v_ref.dtype)
    kkr_ref[...] = (k * kk_ref[...]).astype(kkr_ref.dtype)
    a_ref[...] = a.astype(a_ref.dtype)
    lw_ref[...] = -jnp.exp(w)
    g_ref[...] = g.astype(g_ref.dtype)


def _rw_prep(p_rkv, p_lora, mu_rkv, mu_lora, w0, a0, k_k, k_a, wup, aup, gup, seq, tm=256):
    t = p_rkv.shape[0]
    tm = min(tm, seq)
    n3 = 3 * D_RWKV
    row = lambda n: pl.BlockSpec((1, n), lambda i: (0, 0))
    full = lambda a, b: pl.BlockSpec((a, b), lambda i: (0, 0))
    prev = lambda n: pl.BlockSpec((16, n), lambda i: (jnp.maximum(i * (tm // 16) - 1, 0), 0))
    out = lambda dt: jax.ShapeDtypeStruct((t, D_RWKV), dt)
    ospec = pl.BlockSpec((tm, D_RWKV), lambda i: (i, 0))
    return pl.pallas_call(
        functools.partial(_rw_prep_kernel, tiles_per_seq=seq // tm),
        grid=(t // tm,),
        in_specs=[pl.BlockSpec((tm, n3), lambda i: (i, 0)), prev(n3),
                  pl.BlockSpec((tm, RW_LORA_PAD), lambda i: (i, 0)), prev(RW_LORA_PAD),
                  row(n3), row(RW_LORA_PAD), row(D_RWKV), row(D_RWKV), row(D_RWKV), row(D_RWKV),
                  full(RW_LORA_PAD, D_RWKV), full(RW_LORA_PAD, D_RWKV), full(RW_LORA_PAD, D_RWKV)],
        out_specs=[ospec] * 7,
        out_shape=[out(BF16)] * 5 + [out(F32), out(BF16)],
        compiler_params=_cparams(("parallel",)),
    )(p_rkv, p_rkv, p_lora, p_lora, mu_rkv, mu_lora, w0, a0, k_k, k_a, wup, aup, gup)


def _rw_chunk_kernel(r_ref, k_ref, v_ref, kkr_ref, a_ref, lw_ref, rk_ref, ones_ref,
                     q_ref, oi_ref, bonus_ref, m_ref, n_ref):
    c = RW_CHUNK
    hd = RW_HEAD
    heads = range(r_ref.shape[1] // hd)
    ri = lax.broadcasted_iota(jnp.int32, (c, c), 0)
    ci = lax.broadcasted_iota(jnp.int32, (c, c), 1)
    tri_incl = (ci <= ri)
    tri_strict = (ci < ri)
    ltri = tri_incl.astype(BF16)
    eye = (ci == ri).astype(F32)
    lw = lw_ref[...]
    lw_hi = lw.astype(BF16)
    lw_lo = (lw - lw_hi.astype(F32)).astype(BF16)
    cl = (jnp.dot(ltri, lw_hi, preferred_element_type=F32)
          + jnp.dot(ltri, lw_lo, preferred_element_type=F32))
    cl_end = cl[c - 1:c, :]
    r = r_ref[...].astype(F32)
    k = k_ref[...].astype(F32)
    rt_d = r * jnp.exp(cl)
    e_neg_d = jnp.exp(-cl)
    kt_d = k * e_neg_d
    e_prev_d = jnp.exp(cl - lw)
    e_end_d = jnp.exp(cl_end - cl)
    kte_d = k * e_end_d
    gam_d = jnp.exp(cl_end)
    kkr_d = kkr_ref[...].astype(F32)

    nrm = jnp.sqrt(_head_sum(kkr_d * kkr_d, ones_ref))
    kk_d = kkr_d / jnp.maximum(nrm, 1e-12)
    b_d = kk_d * a_ref[...].astype(F32)
    bte_d = b_d * e_end_d
    kr_d = jnp.concatenate([kk_d * e_prev_d, rt_d], axis=0)
    kb_d = jnp.concatenate([kt_d, b_d * e_neg_d], axis=0)
    v_d = v_ref[...]
    bonus_ref[...] = _head_sum(r * k * rk_ref[0:1, :], ones_ref) * v_d.astype(F32)
    sl = [slice(h * hd, (h + 1) * hd) for h in heads]
    v = [v_d[:, s] for s in sl]
    rt = [rt_d[:, s] for s in sl]
    kkt = [kr_d[:c, s] for s in sl]
    bte = [bte_d[:, s] for s in sl]
    a4 = [_dot_nt(kr_d[:, s], kb_d[:, s]) for s in sl]
    a_kb = [jnp.where(tri_strict, a4[h][:c, c:], 0.0) for h in heads]
    a_rb = [jnp.where(tri_incl, a4[h][c:, c:], 0.0) for h in heads]
    a_kr = [jnp.concatenate([jnp.where(tri_strict, a4[h][:c, :c], 0.0),
                             jnp.where(tri_incl, a4[h][c:, :c], 0.0)], axis=0) for h in heads]
    x = [eye - a_kb[h] for h in heads]
    pw = [_dot(a_kb[h], a_kb[h]) for h in heads]
    n_sq = int(math.log2(c)) - 1
    for it in range(n_sq):
        if it + 1 < n_sq:
            xp = [_dot(jnp.concatenate([x[h], pw[h]], axis=0), pw[h]) for h in heads]
            x = [x[h] + xp[h][:c] for h in heads]
            pw = [xp[h][c:] for h in heads]
        else:
            x = [x[h] + _dot(x[h], pw[h]) for h in heads]
    av = [_dot(a_kr[h], v[h]) for h in heads]
    wu = [_dot(x[h], jnp.concatenate([kkt[h], av[h][:c]], axis=1)) for h in heads]
    rb = [_dot(a_rb[h], wu[h]) for h in heads]
    wub = [_dot_tn(wu[h], bte[h]) for h in heads]
    vk = [_dot_tn(v[h], kte_d[:, sl[h]]) for h in heads]
    ji = lax.broadcasted_iota(jnp.int32, (hd, hd), 0)
    jo = lax.broadcasted_iota(jnp.int32, (hd, hd), 1)
    for h in heads:
        m_ref[0, h] = jnp.where(ji == jo, gam_d[:, sl[h]], 0.0) - wub[h][:hd]
        n_ref[0, h] = vk[h] - wub[h][hd:]
    q_ref[...] = jnp.concatenate([rt[h] - rb[h][:, :hd] for h in heads], axis=1)
    oi_ref[...] = jnp.concatenate([av[h][c:] - rb[h][:, hd:] for h in heads], axis=1)


def _rw_chunks(r, k, v, kkr, a, lw, r_k, hp=RW_HEADS):
    t = r.shape[0]
    c = RW_CHUNK
    nch = t // c
    wdt = hp * RW_HEAD
    blk = pl.BlockSpec((c, wdt), lambda i, j: (i, j))
    mat = pl.BlockSpec((1, hp, RW_HEAD, RW_HEAD), lambda i, j: (i, j, 0, 0))
    tok = jax.ShapeDtypeStruct((t, D_RWKV), F32)
    mshape = jax.ShapeDtypeStruct((nch, RW_HEADS, RW_HEAD, RW_HEAD), F32)
    return pl.pallas_call(
        _rw_chunk_kernel,
        grid=(nch, RW_HEADS // hp),
        in_specs=[blk] * 6 + [pl.BlockSpec((1, wdt), lambda i, j: (0, j)),
                  pl.BlockSpec((4 * RW_HEAD, 4 * RW_HEAD), lambda i, j: (0, 0))],
        out_specs=[blk, blk, blk, mat, mat],
        out_shape=[tok, tok, tok, mshape, mshape],
        compiler_params=_cparams(("parallel", "parallel")),
    )(r, k, v, kkr, a, lw, r_k, _head_ones())


def _rw_seq_kernel(q_ref, oi_ref, bonus_ref, g_ref, m_ref, n_ref, gg_ref, gb_ref, ones_ref,
                   o_ref, st_ref):
    @pl.when(pl.program_id(0) == 0)
    def _():
        st_ref[...] = jnp.zeros_like(st_ref)

    hd = RW_HEAD
    sl = [slice(h * hd, (h + 1) * hd) for h in range(RW_HEADS)]
    streams = [(b, h) for b in range(q_ref.shape[0]) for h in range(RW_HEADS)]
    s = {bh: st_ref[bh[0], bh[1]] for bh in streams}
    q = [q_ref[b] for b in range(q_ref.shape[0])]
    o = {(b, h): _dot_nt(q[b][:, sl[h]], s[b, h]) for b, h in streams}
    for b, h in streams:
        st_ref[b, h] = _dot(s[b, h], m_ref[b, 0, h]) + n_ref[b, 0, h]
    inv_n = 1.0 / hd
    for b in range(q_ref.shape[0]):
        ob = jnp.concatenate([o[b, h] for h in range(RW_HEADS)], axis=1) + oi_ref[b]
        ctr = ob - _head_sum(ob, ones_ref) * inv_n
        var = _head_sum(ctr * ctr, ones_ref) * inv_n
        y = ctr * lax.rsqrt(var + RW_GN_EPS) * gg_ref[...] + gb_ref[...] + bonus_ref[b]
        o_ref[b] = (y * g_ref[b].astype(F32)).astype(o_ref.dtype)


def _rw_seq(q, oi, bonus, g, m, n, gn_g, gn_b, bsz, seq):
    c = RW_CHUNK
    nch = seq // c
    r3 = lambda x: x.reshape(bsz, seq, D_RWKV)
    r5 = lambda x: x.reshape(bsz, nch, RW_HEADS, RW_HEAD, RW_HEAD)
    tok = pl.BlockSpec((bsz, c, D_RWKV), lambda i: (0, i, 0))
    mat = pl.BlockSpec((bsz, 1, RW_HEADS, RW_HEAD, RW_HEAD), lambda i: (0, i, 0, 0, 0))
    row = pl.BlockSpec((1, D_RWKV), lambda i: (0, 0))
    y = pl.pallas_call(
        _rw_seq_kernel,
        grid=(nch,),
        in_specs=[tok, tok, tok, tok, mat, mat, row, row,
                  pl.BlockSpec((4 * RW_HEAD, 4 * RW_HEAD), lambda i: (0, 0))],
        out_specs=tok,
        out_shape=jax.ShapeDtypeStruct((bsz, seq, D_RWKV), BF16),
        scratch_shapes=[pltpu.VMEM((bsz, RW_HEADS, RW_HEAD, RW_HEAD), F32)],
        compiler_params=_cparams(("arbitrary",)),
    )(r3(q), r3(oi), r3(bonus), r3(g), r5(m), r5(n), gn_g, gn_b, _head_ones())
    return y.reshape(bsz * seq, D_RWKV)


def _layer_norm(x, g, b):
    mu = jnp.mean(x, axis=-1, keepdims=True)
    var = jnp.mean(jnp.square(x - mu), axis=-1, keepdims=True)
    return (x - mu) * lax.rsqrt(var + LN_EPS) * g + b


def _outproj_kernel(ys_ref, yr_ref, w1_ref, w2_ref, x_ref, g_ref, b_ref, h_ref, hb_ref):
    mix = (jnp.dot(ys_ref[...], w1_ref[...], preferred_element_type=F32)
           + jnp.dot(yr_ref[...], w2_ref[...], preferred_element_type=F32))
    h = _layer_norm(DEEPNORM_ALPHA * x_ref[...] + mix, g_ref[...], b_ref[...])
    h_ref[...] = h
    hb_ref[...] = h.astype(BF16)


def _outproj(ys, yr, w_out, x, g, b, tm=256):
    t = x.shape[0]
    tm = min(tm, t)
    half = pl.BlockSpec((tm, D_SSM), lambda i: (i, 0))
    full = pl.BlockSpec((tm, D_MODEL), lambda i: (i, 0))
    row = pl.BlockSpec((1, D_MODEL), lambda i: (0, 0))
    return pl.pallas_call(
        _outproj_kernel,
        grid=(t // tm,),
        in_specs=[half, half,
                  pl.BlockSpec((D_SSM, D_MODEL), lambda i: (0, 0)),
                  pl.BlockSpec((D_RWKV, D_MODEL), lambda i: (1, 0)),
                  full, row, row],
        out_specs=[full, full],
        out_shape=[jax.ShapeDtypeStruct((t, D_MODEL), F32),
                   jax.ShapeDtypeStruct((t, D_MODEL), BF16)],
        compiler_params=_cparams(("parallel",)),
    )(ys, yr, w_out, w_out, x, g, b)


def _peer_scores_kernel(h_ref, wq_ref, keys_ref, s_ref):
    q = jnp.dot(h_ref[...], wq_ref[...], preferred_element_type=F32)
    for blk in range(2 * PEER_HEADS):
        qb = q[:, blk * PEER_HALF:(blk + 1) * PEER_HALF]
        s_ref[blk] = _dot_nt(keys_ref[blk], qb)


def _peer_scores(hb, wq, keys, tm=256):
    t = hb.shape[0]
    tm = min(tm, t)
    nb = 2 * PEER_HEADS
    return pl.pallas_call(
        _peer_scores_kernel,
        grid=(t // tm,),
        in_specs=[pl.BlockSpec((tm, D_MODEL), lambda i: (i, 0)),
                  pl.BlockSpec((D_MODEL, D_MODEL), lambda i: (0, 0)),
                  pl.BlockSpec((nb, PEER_NKEYS, PEER_HALF), lambda i: (0, 0, 0))],
        out_specs=pl.BlockSpec((nb, PEER_NKEYS, tm), lambda i: (0, 0, i)),
        out_shape=jax.ShapeDtypeStruct((nb, PEER_NKEYS, t), F32),
        compiler_params=_cparams(("parallel",)),
    )(hb, wq, keys)


NO_RANK = 127.0


def _extract_top(x, n, want_rank=False):
    vals = []
    rank = jnp.full(x.shape, NO_RANK, F32) if want_rank else None
    for i in range(n):
        m = jnp.max(x, axis=0, keepdims=True)
        vals.append(m)
        hit = x == m
        if want_rank:
            rank = jnp.where(hit, float(i), rank)
        x = jnp.where(hit, NEG_INF, x)
    return vals, rank


def _peer_topk_kernel(s_ref, nsel_ref, e1_ref, rank2_ref, e2_ref):
    k = PEER_TOPK

    def per_head(h, carry):
        s1 = s_ref[2 * h]
        s2 = s_ref[2 * h + 1]
        ta, _ = _extract_top(s1, k)
        tb, rank2 = _extract_top(s2, k, want_rank=True)
        tbs = jnp.concatenate(tb, axis=0)
        cands = [ta[i] + tbs[0:k // (i + 1), :] for i in range(k)]
        n_c = sum(k // (i + 1) for i in range(k))
        pad = (-n_c) % 8
        padded = cands + ([jnp.full((pad, s1.shape[1]), NEG_INF, F32)] if pad else [])
        best, _ = _extract_top(jnp.concatenate(padded, axis=0), k)
        m0 = best[0]
        tau = best[k - 1]
        z = jnp.zeros_like(m0)
        for bv in best:
            z = z + jnp.exp(bv - m0)
        nsel = jnp.zeros_like(s1)
        for i in range(k):
            cnt = jnp.sum(jnp.where(cands[i] >= tau, 1.0, 0.0), axis=0, keepdims=True)
            nsel = jnp.where(s1 == ta[i], cnt, nsel)
        nsel_ref[h] = nsel
        e1_ref[h] = jnp.exp(s1 - ta[0]) / z
        rank2_ref[h] = rank2.astype(BF16)
        e2_ref[h] = jnp.exp(s2 - tb[0]).astype(BF16)
        return carry

    lax.fori_loop(0, PEER_HEADS, per_head, 0)


def _peer_topk(scores, tt=256):
    nb, nk, t = scores.shape
    tt = min(tt, t)
    big = pl.BlockSpec((PEER_HEADS, nk, tt), lambda i: (0, 0, i))
    f32 = jax.ShapeDtypeStruct((PEER_HEADS, nk, t), F32)
    b16 = jax.ShapeDtypeStruct((PEER_HEADS, nk, t), BF16)
    return pl.pallas_call(
        _peer_topk_kernel,
        grid=(t // tt,),
        in_specs=[pl.BlockSpec((nb, nk, tt), lambda i: (0, 0, i))],
        out_specs=[big, big, big, big],
        out_shape=[f32, f32, b16, b16],
        compiler_params=_cparams(("parallel",)),
    )(scores)


def _peer_dense_kernel(ht_ref, u0_ref, ub_ref, un_ref, v_ref, ns0_ref, e10_ref, nsc_ref, e1c_ref,
                       nsn_ref, e1n_ref, rank2_ref, e2_ref, h1_ref, lg_ref, lb_ref,
                       o_ref, hsa_ref, hsb_ref, *, q, nj):
    j = pl.program_id(1)
    nk = PEER_NKEYS
    tm = ht_ref.shape[1]
    half = q * nk
    ht = ht_ref[...]

    def hidden_block(u_ref, ns_ref, e1_ref, row, qi, live, dst_ref):
        act = jnp.dot(u_ref[qi * nk:(qi + 1) * nk, :], ht, preferred_element_type=F32)
        gate = jnp.zeros((nk, tm), BF16)
        for h in range(PEER_HEADS):
            nsel = ns_ref[h, 0, row:row + 1, :]
            nsel = (nsel if live is None else nsel * live).astype(BF16)
            e1 = e1_ref[h, 0, row:row + 1, :].astype(BF16)
            gate = gate + jnp.where(rank2_ref[h] < nsel, e1 * e2_ref[h], 0.0)
        c0 = math.sqrt(2.0 / math.pi)
        inner = act * (c0 + (c0 * 0.044715) * (act * act))
        half_act = 0.5 * act.astype(BF16)
        hid = (half_act + half_act * jnp.tanh(inner.astype(BF16))) * gate
        dst_ref[:, qi * nk:(qi + 1) * nk] = hid.T

    @pl.when(j == 0)
    def _():
        o_ref[...] = jnp.zeros_like(o_ref)
        for qi in range(q):
            hidden_block(u0_ref, ns0_ref, e10_ref, qi, qi, None, hsa_ref)

    def phase(src_ref, v_rows, u_ref, ns_ref, e1_ref, row0, live, dst_ref):
        wv = D_MODEL // q
        for qi in range(q):
            cols = slice(qi * wv, (qi + 1) * wv)
            o_ref[:, cols] += jnp.dot(src_ref[...], v_ref[v_rows, cols], preferred_element_type=F32)
            hidden_block(u_ref, ns_ref, e1_ref, row0 + qi, qi, live, dst_ref)

    phase(hsa_ref, slice(0, half), ub_ref, nsc_ref, e1c_ref, q, None, hsb_ref)
    more = jnp.where(j + 1 < nj, 1.0, 0.0)
    phase(hsb_ref, slice(half, 2 * half), un_ref, nsn_ref, e1n_ref, 0, more, hsa_ref)

    @pl.when(j == nj - 1)
    def _():
        o_ref[...] = _layer_norm(DEEPNORM_ALPHA * h1_ref[...] + o_ref[...], lg_ref[...], lb_ref[...])


def _peer_dense(ht, u_tab, v_tab, nsel, e1, rank2, e2, h1, ln_g, ln_b, tm=512, te=1024):
    t = ht.shape[1]
    tm = min(tm, t)
    half = te // 2
    q = half // PEER_NKEYS
    ng = PEER_NKEYS // (2 * q)
    nj = PEER_EXPERTS // te
    rows = lambda a: a.reshape(PEER_HEADS, ng, 2 * q, t)
    nxt = lambda j: jnp.minimum(j + 1, nj - 1)
    u_spec = lambda f: pl.BlockSpec((half, D_MODEL), lambda i, j: (f(j), 0))
    row_spec = lambda f: pl.BlockSpec((PEER_HEADS, 1, 2 * q, tm), lambda i, j: (0, f(j), 0, i))
    key_spec = pl.BlockSpec((PEER_HEADS, PEER_NKEYS, tm), lambda i, j: (0, 0, i))
    vec_spec = pl.BlockSpec((1, D_MODEL), lambda i, j: (0, 0))
    first, cur = (lambda j: 0), (lambda j: j)
    return pl.pallas_call(
        functools.partial(_peer_dense_kernel, q=q, nj=nj),
        grid=(t // tm, nj),
        in_specs=[pl.BlockSpec((D_MODEL, tm), lambda i, j: (0, i)),
                  u_spec(first), u_spec(lambda j: 2 * j + 1), u_spec(lambda j: 2 * nxt(j)),
                  pl.BlockSpec((te, D_MODEL), lambda i, j: (j, 0)),
                  row_spec(first), row_spec(first), row_spec(cur), row_spec(cur),
                  row_spec(nxt), row_spec(nxt), key_spec, key_spec,
                  pl.BlockSpec((tm, D_MODEL), lambda i, j: (i, 0)), vec_spec, vec_spec],
        out_specs=pl.BlockSpec((tm, D_MODEL), lambda i, j: (i, 0)),
        out_shape=jax.ShapeDtypeStruct((t, D_MODEL), F32),
        scratch_shapes=[pltpu.VMEM((tm, half), BF16), pltpu.VMEM((tm, half), BF16)],
        compiler_params=_cparams(("parallel", "arbitrary")),
    )(ht, u_tab, u_tab, u_tab, v_tab, rows(nsel), rows(e1), rows(nsel), rows(e1),
      rows(nsel), rows(e1), rank2, e2, h1, ln_g, ln_b)


def _layer(h, bsz, seq, w_in, s5, rw, w_out, ln1, peer, ln2):
    (lam_re, lam_im, log_step, b_re, b_im, c_re, c_im, d_skip, w_glu, b_glu) = s5
    (mu, w0, w_up, a0, a_up, g_up, k_k, k_a, r_k, gn_g, gn_b) = rw
    (w_q, keys1, keys2, u_tab, v_tab) = peer
    hb = h.astype(BF16)
    n_rkv = 3 * D_RWKV
    w_in_b = w_in.astype(BF16)
    pad = RW_LORA_PAD - RW_LORA

    u = _matmul(hb, w_in_b[:, :D_SSM], F32, 1024, 1024)
    p_rkv = _matmul(hb, w_in_b[:, D_SSM:D_SSM + n_rkv], BF16, 1024, 1024)
    p_lora = _matmul(hb, jnp.pad(w_in_b[:, D_SSM + n_rkv:], ((0, 0), (0, pad))), F32, 512, RW_LORA_PAD)

    yg = _s5_scan(u, _s5_params(lam_re, lam_im, log_step, b_re, b_im, c_re, c_im, d_skip,
                                seq // S5_CHUNK), bsz, seq)
    y_ssm = _glu(yg, w_glu.astype(BF16), b_glu[None, :])

    mu_rkv = mu[None, :n_rkv]
    mu_lora = jnp.pad(mu[n_rkv:], (0, pad))[None, :]
    zrow = lambda n: jnp.zeros((n, D_RWKV), F32)
    wup = jnp.concatenate([w_up, zrow(RW_LORA_PAD - RW_W_LORA)], axis=0).astype(BF16)
    aup = jnp.concatenate([zrow(RW_W_LORA), a_up, zrow(RW_LORA_PAD - RW_W_LORA - RW_A_LORA)],
                          axis=0).astype(BF16)
    gup = jnp.concatenate([zrow(RW_W_LORA + RW_A_LORA), g_up, zrow(pad)], axis=0).astype(BF16)
    r, k, v, kkr, a, lw, g = _rw_prep(p_rkv, p_lora, mu_rkv, mu_lora, w0[None], a0[None],
                                      k_k[None], k_a[None], wup, aup, gup, seq)
    q, oi, bonus, m, n = _rw_chunks(r, k, v, kkr, a, lw, r_k.reshape(1, D_RWKV))
    y_rw = _rw_seq(q, oi, bonus, g, m, n, gn_g[None], gn_b[None], bsz, seq)

    h1, h1b = _outproj(y_ssm, y_rw, w_out.astype(BF16), h, ln1[0][None], ln1[1][None])

    keys = jnp.stack([keys1, keys2], axis=1).reshape(2 * PEER_HEADS, PEER_NKEYS, PEER_HALF)
    scores = _peer_scores(h1b, w_q.astype(BF16), keys.astype(BF16))
    nsel, e1, rank2, e2 = _peer_topk(scores)
    return _peer_dense(h1b.T, u_tab.astype(BF16), v_tab.astype(BF16), nsel, e1, rank2, e2,
                       h1, ln2[0][None], ln2[1][None])


def kernel(x, w_in, s5_lam_re, s5_lam_im, s5_log_step, s5_b_re, s5_b_im, s5_c_re, s5_c_im, s5_d, s5_w_glu, s5_b_glu, rw_mu, rw_w0, rw_w_up, rw_a0, rw_a_up, rw_g_up, rw_k_k, rw_k_a, rw_r_k, rw_gn_g, rw_gn_b, w_out, ln1_g, ln1_b, peer_w_q, peer_keys1, peer_keys2, peer_u, peer_v, ln2_g, ln2_b):
    bsz, seq, dim = x.shape
    h = x.reshape(bsz * seq, dim)
    for l in range(w_in.shape[0]):
        h = _layer(
            h, bsz, seq, w_in[l],
            (s5_lam_re[l], s5_lam_im[l], s5_log_step[l], s5_b_re[l], s5_b_im[l], s5_c_re[l],
             s5_c_im[l], s5_d[l], s5_w_glu[l], s5_b_glu[l]),
            (rw_mu[l], rw_w0[l], rw_w_up[l], rw_a0[l], rw_a_up[l], rw_g_up[l], rw_k_k[l],
             rw_k_a[l], rw_r_k[l], rw_gn_g[l], rw_gn_b[l]),
            w_out[l], (ln1_g[l], ln1_b[l]),
            (peer_w_q[l], peer_keys1[l], peer_keys2[l], peer_u[l], peer_v[l]),
            (ln2_g[l], ln2_b[l]))
    return h.reshape(bsz, seq, dim)
```

```python
import functools
import math

import jax
import jax.numpy as jnp
from jax import lax
from jax.experimental import pallas as pl
from jax.experimental.pallas import tpu as pltpu

F32 = jnp.float32
BF16 = jnp.bfloat16

D_MODEL = 2048
D_SSM = 1024
D_RWKV = 1024
S5_CH = 16
S5_GROUPS = D_SSM // S5_CH
S5_STATE = 64
S5_CHUNK = 16
RW_HEAD = 64
RW_HEADS = D_RWKV // RW_HEAD
RW_W_LORA = 64
RW_A_LORA = 64
RW_G_LORA = 160
RW_LORA = RW_W_LORA + RW_A_LORA + RW_G_LORA
RW_LORA_PAD = 384
RW_CHUNK = 64
PEER_HEADS = 8
PEER_NKEYS = 128
PEER_EXPERTS = PEER_NKEYS * PEER_NKEYS
PEER_HALF = 128
PEER_TOPK = 16
DEPTH = 1
DEEPNORM_ALPHA = (2.0 * DEPTH) ** 0.25
LN_EPS = 1e-5
RW_GN_EPS = 64e-5
NEG_INF = float("-inf")

VMEM_LIMIT = 56 * 1024 * 1024


def _cparams(sem):
    return pltpu.CompilerParams(dimension_semantics=sem, vmem_limit_bytes=VMEM_LIMIT)


def _dot(a, b):
    return jnp.dot(a.astype(BF16), b.astype(BF16), preferred_element_type=F32)


def _dot_nt(a, b):
    return lax.dot_general(a.astype(BF16), b.astype(BF16), (((1,), (1,)), ((), ())),
                           preferred_element_type=F32)


def _dot_tn(a, b):
    return lax.dot_general(a.astype(BF16), b.astype(BF16), (((0,), (0,)), ((), ())),
                           preferred_element_type=F32)


def _gelu(x):
    c = math.sqrt(2.0 / math.pi)
    return 0.5 * x * (1.0 + jnp.tanh(c * (x + 0.044715 * (x * x * x))))


def _sigmoid(x):
    return 1.0 / (1.0 + jnp.exp(-x))


def _head_sum(x, ones_ref):
    x_hi = x.astype(BF16)
    x_lo = (x - x_hi.astype(F32)).astype(BF16)
    w = ones_ref.shape[0]
    return jnp.concatenate(
        [jnp.dot(x_hi[:, i:i + w], ones_ref[...], preferred_element_type=F32)
         + jnp.dot(x_lo[:, i:i + w], ones_ref[...], preferred_element_type=F32)
         for i in range(0, x.shape[1], w)], axis=1)


def _head_ones():
    hid = jnp.arange(4 * RW_HEAD) // RW_HEAD
    return (hid[:, None] == hid[None, :]).astype(BF16)


def _mm_kernel(a_ref, b_ref, o_ref):
    o_ref[...] = jnp.dot(a_ref[...], b_ref[...],
                         preferred_element_type=F32).astype(o_ref.dtype)


def _matmul(a, b, out_dtype, tm, tn):
    m, k = a.shape
    n = b.shape[1]
    tm = min(tm, m)
    return pl.pallas_call(
        _mm_kernel,
        grid=(m // tm, n // tn),
        in_specs=[pl.BlockSpec((tm, k), lambda i, j: (i, 0)),
                  pl.BlockSpec((k, tn), lambda i, j: (0, j))],
        out_specs=pl.BlockSpec((tm, tn), lambda i, j: (i, j)),
        out_shape=jax.ShapeDtypeStruct((m, n), out_dtype),
        compiler_params=_cparams(("parallel", "parallel")),
    )(a, b)


def _s5_kernel(u_ref, mt_ref, bc_ref, cc_ref, ar_ref, ai_ref, d_ref, o_ref, *, nc):
    c = S5_CHUNK
    rows = u_ref.shape[0] // c
    groups = u_ref.shape[1] // S5_CH
    xs = [u_ref[pl.ds(tl, rows, stride=c), :] for tl in range(c)]
    cidx = lax.broadcasted_iota(jnp.int32, (rows, 2 * S5_STATE), 0) & (nc - 1)
    ys = []
    for g in range(groups):
        lanes = slice(g * S5_CH, (g + 1) * S5_CH)
        u = jnp.concatenate([x[:, lanes] for x in xs], axis=1).astype(BF16)
        y = jnp.dot(u, mt_ref[g], preferred_element_type=F32)
        s = jnp.dot(u, bc_ref[g], preferred_element_type=F32)
        d, k = 1, 0
        while d < nc:
            sh = jnp.where(cidx >= d, pltpu.roll(s, d, axis=0), 0.0)
            shs = pltpu.roll(sh, S5_STATE, axis=1)
            s = s + sh * ar_ref[g, k:k + 1, :] + shs * ai_ref[g, k:k + 1, :]
            d, k = d * 2, k + 1
        sp = jnp.where(cidx >= 1, pltpu.roll(s, 1, axis=0), 0.0)
        ys.append(y + jnp.dot(sp.astype(BF16), cc_ref[g], preferred_element_type=F32))
    for tl in range(c):
        y = jnp.concatenate([yg[:, tl * S5_CH:(tl + 1) * S5_CH] for yg in ys], axis=1)
        o_ref[pl.ds(tl, rows, stride=c), :] = _gelu(y + d_ref[...] * xs[tl])


def _s5_params(lam_re, lam_im, log_step, b_re, b_im, c_re, c_im, d_skip, nc):
    hi = lax.Precision.HIGHEST
    c = S5_CHUNK
    step = jnp.exp(log_step)[:, None]
    a_re = jnp.exp(lam_re * step) * jnp.cos(lam_im * step)
    a_im = jnp.exp(lam_re * step) * jnp.sin(lam_im * step)
    den = lam_re * lam_re + lam_im * lam_im
    f_re = ((a_re - 1.0) * lam_re + a_im * lam_im) / den
    f_im = (a_im * lam_re - (a_re - 1.0) * lam_im) / den
    bb_re = f_re[..., None] * b_re - f_im[..., None] * b_im
    bb_im = f_re[..., None] * b_im + f_im[..., None] * b_re

    def power(j):
        jj = j[None, :, None]
        mag = jnp.exp(lam_re[:, None, :] * step[:, None, :] * jj)
        ang = lam_im[:, None, :] * step[:, None, :] * jj
        return mag * jnp.cos(ang), mag * jnp.sin(ang)

    pw_re, pw_im = power(jnp.arange(c + 1, dtype=F32))
    ce_re = c_re[:, None] * pw_re[:, :, None, :] - c_im[:, None] * pw_im[:, :, None, :]
    ce_im = c_re[:, None] * pw_im[:, :, None, :] + c_im[:, None] * pw_re[:, :, None, :]
    kern = (jnp.einsum('gjop,gpi->gjoi', ce_re[:, :c], bb_re, precision=hi)
            - jnp.einsum('gjop,gpi->gjoi', ce_im[:, :c], bb_im, precision=hi))
    t_idx = jnp.arange(c)
    lag = t_idx[None, :] - t_idx[:, None]
    place = (lag[:, :, None] == t_idx[None, None, :]).astype(F32)
    g = lam_re.shape[0]
    mt = jnp.einsum('stj,gjoi->gsito', place, kern, precision=hi)
    mt = mt.reshape(g, c * S5_CH, c * S5_CH)
    rv_re, rv_im = pw_re[:, c - 1::-1][:, :c], pw_im[:, c - 1::-1][:, :c]
    bc_r = rv_re[:, :, None, :] * jnp.transpose(bb_re, (0, 2, 1))[:, None] \
        - rv_im[:, :, None, :] * jnp.transpose(bb_im, (0, 2, 1))[:, None]
    bc_i = rv_re[:, :, None, :] * jnp.transpose(bb_im, (0, 2, 1))[:, None] \
        + rv_im[:, :, None, :] * jnp.transpose(bb_re, (0, 2, 1))[:, None]
    bc = jnp.concatenate([bc_r, bc_i], axis=-1).reshape(g, c * S5_CH, 2 * S5_STATE)
    cc_r = jnp.transpose(ce_re[:, 1:], (0, 3, 1, 2)).reshape(g, S5_STATE, c * S5_CH)
    cc_i = -jnp.transpose(ce_im[:, 1:], (0, 3, 1, 2)).reshape(g, S5_STATE, c * S5_CH)
    cc = jnp.concatenate([cc_r, cc_i], axis=1)
    sc_re, sc_im = power(c * (2.0 ** jnp.arange(8, dtype=F32)))
    ar = jnp.concatenate([sc_re, sc_re], axis=-1)
    ai = jnp.concatenate([-sc_im, sc_im], axis=-1)
    return mt.astype(BF16), bc.astype(BF16), cc.astype(BF16), ar, ai, d_skip[None, :]


def _s5_scan(u_tok, params, bsz, seq, gpb=8):
    mt, bc, cc, ar, ai, d_row = params
    nc = seq // S5_CHUNK
    t = bsz * seq
    w = S5_CHUNK * S5_CH
    lanes = gpb * S5_CH
    tok = pl.BlockSpec((t, lanes), lambda i: (0, i))
    spec3 = lambda a, b: pl.BlockSpec((gpb, a, b), lambda i: (i, 0, 0))
    return pl.pallas_call(
        functools.partial(_s5_kernel, nc=nc),
        grid=(S5_GROUPS // gpb,),
        in_specs=[tok, spec3(w, w), spec3(w, 2 * S5_STATE), spec3(2 * S5_STATE, w),
                  spec3(8, 2 * S5_STATE), spec3(8, 2 * S5_STATE),
                  pl.BlockSpec((1, lanes), lambda i: (0, i))],
        out_specs=tok,
        out_shape=jax.ShapeDtypeStruct((t, D_SSM), F32),
        compiler_params=_cparams(("parallel",)),
    )(u_tok, mt, bc, cc, ar, ai, d_row)


def _glu_kernel(y_ref, w_ref, b_ref, o_ref):
    y = y_ref[...]
    z = jnp.dot(y.astype(BF16), w_ref[...], preferred_element_type=F32) + b_ref[...]
    o_ref[...] = (y * _sigmoid(z)).astype(o_ref.dtype)


def _glu(y, w, b, tm=512):
    m, n = y.shape
    tm = min(tm, m)
    return pl.pallas_call(
        _glu_kernel,
        grid=(m // tm,),
        in_specs=[pl.BlockSpec((tm, n), lambda i: (i, 0)),
                  pl.BlockSpec((n, n), lambda i: (0, 0)),
                  pl.BlockSpec((1, n), lambda i: (0, 0))],
        out_specs=pl.BlockSpec((tm, n), lambda i: (i, 0)),
        out_shape=jax.ShapeDtypeStruct((m, n), BF16),
        compiler_params=_cparams(("parallel",)),
    )(y, w, b)


def _rw_prep_kernel(p_ref, pp_ref, l_ref, lp_ref, mu_ref, mul_ref, w0_ref, a0_ref,
                    kk_ref, ka_ref, wup_ref, aup_ref, gup_ref,
                    r_ref, k_ref, v_ref, kkr_ref, a_ref, lw_ref, g_ref, *, tiles_per_seq):
    first = (pl.program_id(0) % tiles_per_seq) == 0

    def shifted(cur_ref, prev_ref, mu):
        cur = cur_ref[...].astype(F32)
        prev_row = jnp.where(first, 0.0, prev_ref[15:16, :].astype(F32))
        row = lax.broadcasted_iota(jnp.int32, cur.shape, 0)
        prev = jnp.where(row == 0, prev_row, pltpu.roll(cur, 1, axis=0))
        return cur + mu * (prev - cur)

    p = shifted(p_ref, pp_ref, mu_ref[...])
    lo = shifted(l_ref, lp_ref, mul_ref[...])
    r = p[:, :D_RWKV]
    k = p[:, D_RWKV:2 * D_RWKV]
    v = p[:, 2 * D_RWKV:]
    w_pre = w0_ref[...] + _dot(jnp.tanh(lo), wup_ref[...])
    a = _sigmoid(a0_ref[...] + _dot(lo, aup_ref[...]))
    g = _dot(_sigmoid(lo), gup_ref[...])
    z = -w_pre
    softplus = jnp.maximum(z, 0.0) + jnp.log(1.0 + jnp.exp(-jnp.abs(z)))
    w = -softplus - 0.5
    r_ref[...] = r.astype(r_ref.dtype)
    k_ref[...] = (k * (1.0 + (a - 1.0) * ka_ref[...])).astype(k_ref.dtype)
    v_ref[...] = v.astype(v_ref.dtype)
    kkr_ref[...] = (k * kk_ref[...]).astype(kkr_ref.dtype)
    a_ref[...] = a.astype(a_ref.dtype)
    lw_ref[...] = -jnp.exp(w)
    g_ref[...] = g.astype(g_ref.dtype)


def _rw_prep(p_rkv, p_lora, mu_rkv, mu_lora, w0, a0, k_k, k_a, wup, aup, gup, seq, tm=256):
    t = p_rkv.shape[0]
    tm = min(tm, seq)
    n3 = 3 * D_RWKV
    row = lambda n: pl.BlockSpec((1, n), lambda i: (0, 0))
    full = lambda a, b: pl.BlockSpec((a, b), lambda i: (0, 0))
    prev = lambda n: pl.BlockSpec((16, n), lambda i: (jnp.maximum(i * (tm // 16) - 1, 0), 0))
    out = lambda dt: jax.ShapeDtypeStruct((t, D_RWKV), dt)
    ospec = pl.BlockSpec((tm, D_RWKV), lambda i: (i, 0))
    return pl.pallas_call(
        functools.partial(_rw_prep_kernel, tiles_per_seq=seq // tm),
        grid=(t // tm,),
        in_specs=[pl.BlockSpec((tm, n3), lambda i: (i, 0)), prev(n3),
                  pl.BlockSpec((tm, RW_LORA_PAD), lambda i: (i, 0)), prev(RW_LORA_PAD),
                  row(n3), row(RW_LORA_PAD), row(D_RWKV), row(D_RWKV), row(D_RWKV), row(D_RWKV),
                  full(RW_LORA_PAD, D_RWKV), full(RW_LORA_PAD, D_RWKV), full(RW_LORA_PAD, D_RWKV)],
        out_specs=[ospec] * 7,
        out_shape=[out(BF16)] * 5 + [out(F32), out(BF16)],
        compiler_params=_cparams(("parallel",)),
    )(p_rkv, p_rkv, p_lora, p_lora, mu_rkv, mu_lora, w0, a0, k_k, k_a, wup, aup, gup)


def _rw_chunk_kernel(r_ref, k_ref, v_ref, kkr_ref, a_ref, lw_ref, rk_ref, ones_ref,
                     q_ref, oi_ref, bonus_ref, m_ref, n_ref):
    c = RW_CHUNK
    hd = RW_HEAD
    heads = range(r_ref.shape[1] // hd)
    ri = lax.broadcasted_iota(jnp.int32, (c, c), 0)
    ci = lax.broadcasted_iota(jnp.int32, (c, c), 1)
    tri_incl = (ci <= ri)
    tri_strict = (ci < ri)
    ltri = tri_incl.astype(BF16)
    eye = (ci == ri).astype(F32)
    lw = lw_ref[...]
    lw_hi = lw.astype(BF16)
    lw_lo = (lw - lw_hi.astype(F32)).astype(BF16)
    cl = (jnp.dot(ltri, lw_hi, preferred_element_type=F32)
          + jnp.dot(ltri, lw_lo, preferred_element_type=F32))
    cl_end = cl[c - 1:c, :]
    r = r_ref[...].astype(F32)
    k = k_ref[...].astype(F32)
    rt_d = r * jnp.exp(cl)
    e_neg_d = jnp.exp(-cl)
    kt_d = k * e_neg_d
    e_prev_d = jnp.exp(cl - lw)
    e_end_d = jnp.exp(cl_end - cl)
    kte_d = k * e_end_d
    gam_d = jnp.exp(cl_end)
    kkr_d = kkr_ref[...].astype(F32)

    nrm = jnp.sqrt(_head_sum(kkr_d * kkr_d, ones_ref))
    kk_d = kkr_d / jnp.maximum(nrm, 1e-12)
    b_d = kk_d * a_ref[...].astype(F32)
    bte_d = b_d * e_end_d
    kr_d = jnp.concatenate([kk_d * e_prev_d, rt_d], axis=0)
    kb_d = jnp.concatenate([kt_d, b_d * e_neg_d], axis=0)
    v_d = v_ref[...]
    bonus_ref[...] = _head_sum(r * k * rk_ref[0:1, :], ones_ref) * v_d.astype(F32)
    sl = [slice(h * hd, (h + 1) * hd) for h in heads]
    v = [v_d[:, s] for s in sl]
    rt = [rt_d[:, s] for s in sl]
    kkt = [kr_d[:c, s] for s in sl]
    bte = [bte_d[:, s] for s in sl]
    a4 = [_dot_nt(kr_d[:, s], kb_d[:, s]) for s in sl]
    a_kb = [jnp.where(tri_strict, a4[h][:c, c:], 0.0) for h in heads]
    a_rb = [jnp.where(tri_incl, a4[h][c:, c:], 0.0) for h in heads]
    a_kr = [jnp.concatenate([jnp.where(tri_strict, a4[h][:c, :c], 0.0),
                             jnp.where(tri_incl, a4[h][c:, :c], 0.0)], axis=0) for h in heads]
    x = [eye - a_kb[h] for h in heads]
    pw = [_dot(a_kb[h], a_kb[h]) for h in heads]
    n_sq = int(math.log2(c)) - 1
    for it in range(n_sq):
        if it + 1 < n_sq:
            xp = [_dot(jnp.concatenate([x[h], pw[h]], axis=0), pw[h]) for h in heads]
            x = [x[h] + xp[h][:c] for h in heads]
            pw = [xp[h][c:] for h in heads]
        else:
            x = [x[h] + _dot(x[h], pw[h]) for h in heads]
    av = [_dot(a_kr[h], v[h]) for h in heads]
    wu = [_dot(x[h], jnp.concatenate([kkt[h], av[h][:c]], axis=1)) for h in heads]
    rb = [_dot(a_rb[h], wu[h]) for h in heads]
    wub = [_dot_tn(wu[h], bte[h]) for h in heads]
    vk = [_dot_tn(v[h], kte_d[:, sl[h]]) for h in heads]
    ji = lax.broadcasted_iota(jnp.int32, (hd, hd), 0)
    jo = lax.broadcasted_iota(jnp.int32, (hd, hd), 1)
    for h in heads:
        m_ref[0, h] = jnp.where(ji == jo, gam_d[:, sl[h]], 0.0) - wub[h][:hd]
        n_ref[0, h] = vk[h] - wub[h][hd:]
    q_ref[...] = jnp.concatenate([rt[h] - rb[h][:, :hd] for h in heads], axis=1)
    oi_ref[...] = jnp.concatenate([av[h][c:] - rb[h][:, hd:] for h in heads], axis=1)


def _rw_chunks(r, k, v, kkr, a, lw, r_k, hp=RW_HEADS):
    t = r.shape[0]
    c = RW_CHUNK
    nch = t // c
    wdt = hp * RW_HEAD
    blk = pl.BlockSpec((c, wdt), lambda i, j: (i, j))
    mat = pl.BlockSpec((1, hp, RW_HEAD, RW_HEAD), lambda i, j: (i, j, 0, 0))
    tok = jax.ShapeDtypeStruct((t, D_RWKV), F32)
    mshape = jax.ShapeDtypeStruct((nch, RW_HEADS, RW_HEAD, RW_HEAD), F32)
    return pl.pallas_call(
        _rw_chunk_kernel,
        grid=(nch, RW_HEADS // hp),
        in_specs=[blk] * 6 + [pl.BlockSpec((1, wdt), lambda i, j: (0, j)),
                  pl.BlockSpec((4 * RW_HEAD, 4 * RW_HEAD), lambda i, j: (0, 0))],
        out_specs=[blk, blk, blk, mat, mat],
        out_shape=[tok, tok, tok, mshape, mshape],
        compiler_params=_cparams(("parallel", "parallel")),
    )(r, k, v, kkr, a, lw, r_k, _head_ones())


def _rw_seq_kernel(q_ref, oi_ref, bonus_ref, g_ref, m_ref, n_ref, gg_ref, gb_ref, ones_ref,
                   o_ref, st_ref):
    @pl.when(pl.program_id(0) == 0)
    def _():
        st_ref[...] = jnp.zeros_like(st_ref)

    hd = RW_HEAD
    sl = [slice(h * hd, (h + 1) * hd) for h in range(RW_HEADS)]
    streams = [(b, h) for b in range(q_ref.shape[0]) for h in range(RW_HEADS)]
    s = {bh: st_ref[bh[0], bh[1]] for bh in streams}
    q = [q_ref[b] for b in range(q_ref.shape[0])]
    o = {(b, h): _dot_nt(q[b][:, sl[h]], s[b, h]) for b, h in streams}
    for b, h in streams:
        st_ref[b, h] = _dot(s[b, h], m_ref[b, 0, h]) + n_ref[b, 0, h]
    inv_n = 1.0 / hd
    for b in range(q_ref.shape[0]):
        ob = jnp.concatenate([o[b, h] for h in range(RW_HEADS)], axis=1) + oi_ref[b]
        ctr = ob - _head_sum(ob, ones_ref) * inv_n
        var = _head_sum(ctr * ctr, ones_ref) * inv_n
        y = ctr * lax.rsqrt(var + RW_GN_EPS) * gg_ref[...] + gb_ref[...] + bonus_ref[b]
        o_ref[b] = (y * g_ref[b].astype(F32)).astype(o_ref.dtype)


def _rw_seq(q, oi, bonus, g, m, n, gn_g, gn_b, bsz, seq):
    c = RW_CHUNK
    nch = seq // c
    r3 = lambda x: x.reshape(bsz, seq, D_RWKV)
    r5 = lambda x: x.reshape(bsz, nch, RW_HEADS, RW_HEAD, RW_HEAD)
    tok = pl.BlockSpec((bsz, c, D_RWKV), lambda i: (0, i, 0))
    mat = pl.BlockSpec((bsz, 1, RW_HEADS, RW_HEAD, RW_HEAD), lambda i: (0, i, 0, 0, 0))
    row = pl.BlockSpec((1, D_RWKV), lambda i: (0, 0))
    y = pl.pallas_call(
        _rw_seq_kernel,
        grid=(nch,),
        in_specs=[tok, tok, tok, tok, mat, mat, row, row,
                  pl.BlockSpec((4 * RW_HEAD, 4 * RW_HEAD), lambda i: (0, 0))],
        out_specs=tok,
        out_shape=jax.ShapeDtypeStruct((bsz, seq, D_RWKV), BF16),
        scratch_shapes=[pltpu.VMEM((bsz, RW_HEADS, RW_HEAD, RW_HEAD), F32)],
        compiler_params=_cparams(("arbitrary",)),
    )(r3(q), r3(oi), r3(bonus), r3(g), r5(m), r5(n), gn_g, gn_b, _head_ones())
    return y.reshape(bsz * seq, D_RWKV)


def _layer_norm(x, g, b):
    mu = jnp.mean(x, axis=-1, keepdims=True)
    var = jnp.mean(jnp.square(x - mu), axis=-1, keepdims=True)
    return (x - mu) * lax.rsqrt(var + LN_EPS) * g + b


def _outproj_kernel(ys_ref, yr_ref, w1_ref, w2_ref, x_ref, g_ref, b_ref, h_ref, hb_ref, ht_ref):
    mix = (jnp.dot(ys_ref[...], w1_ref[...], preferred_element_type=F32)
           + jnp.dot(yr_ref[...], w2_ref[...], preferred_element_type=F32))
    h = _layer_norm(DEEPNORM_ALPHA * x_ref[...] + mix, g_ref[...], b_ref[...])
    h_ref[...] = h
    hb = h.astype(BF16)
    hb_ref[...] = hb
    ht_ref[...] = hb.T


def _outproj(ys, yr, w_out, x, g, b, tm=512):
    t = x.shape[0]
    tm = min(tm, t)
    half = pl.BlockSpec((tm, D_SSM), lambda i: (i, 0))
    full = pl.BlockSpec((tm, D_MODEL), lambda i: (i, 0))
    row = pl.BlockSpec((1, D_MODEL), lambda i: (0, 0))
    return pl.pallas_call(
        _outproj_kernel,
        grid=(t // tm,),
        in_specs=[half, half,
                  pl.BlockSpec((D_SSM, D_MODEL), lambda i: (0, 0)),
                  pl.BlockSpec((D_RWKV, D_MODEL), lambda i: (1, 0)),
                  full, row, row],
        out_specs=[full, full, pl.BlockSpec((D_MODEL, tm), lambda i: (0, i))],
        out_shape=[jax.ShapeDtypeStruct((t, D_MODEL), F32),
                   jax.ShapeDtypeStruct((t, D_MODEL), BF16),
                   jax.ShapeDtypeStruct((D_MODEL, t), BF16)],
        compiler_params=_cparams(("parallel",)),
    )(ys, yr, w_out, w_out, x, g, b)


def _peer_scores_kernel(h_ref, wq_ref, keys_ref, s_ref):
    q = jnp.dot(h_ref[...], wq_ref[...], preferred_element_type=F32)
    for blk in range(2 * PEER_HEADS):
        qb = q[:, blk * PEER_HALF:(blk + 1) * PEER_HALF]
        s_ref[blk] = _dot_nt(keys_ref[blk], qb)


def _peer_scores(hb, wq, keys, tm=512):
    t = hb.shape[0]
    tm = min(tm, t)
    nb = 2 * PEER_HEADS
    return pl.pallas_call(
        _peer_scores_kernel,
        grid=(t // tm,),
        in_specs=[pl.BlockSpec((tm, D_MODEL), lambda i: (i, 0)),
                  pl.BlockSpec((D_MODEL, D_MODEL), lambda i: (0, 0)),
                  pl.BlockSpec((nb, PEER_NKEYS, PEER_HALF), lambda i: (0, 0, 0))],
        out_specs=pl.BlockSpec((nb, PEER_NKEYS, tm), lambda i: (0, 0, i)),
        out_shape=jax.ShapeDtypeStruct((nb, PEER_NKEYS, t), F32),
        compiler_params=_cparams(("parallel",)),
    )(hb, wq, keys)


NO_RANK = 127.0


def _extract_top(x, n, want_rank=False):
    vals = []
    rank = jnp.full(x.shape, NO_RANK, F32) if want_rank else None
    for i in range(n):
        m = jnp.max(x, axis=0, keepdims=True)
        vals.append(m)
        hit = x == m
        if want_rank:
            rank = jnp.where(hit, float(i), rank)
        x = jnp.where(hit, NEG_INF, x)
    return vals, rank


def _peer_topk_kernel(s_ref, nsel_ref, e1_ref, rank2_ref, e2_ref):
    k = PEER_TOPK

    def per_head(h, carry):
        s1 = s_ref[2 * h]
        s2 = s_ref[2 * h + 1]
        ta, _ = _extract_top(s1, k)
        tb, rank2 = _extract_top(s2, k, want_rank=True)
        tbs = jnp.concatenate(tb, axis=0)
        cands = [ta[i] + tbs[0:k // (i + 1), :] for i in range(k)]
        n_c = sum(k // (i + 1) for i in range(k))
        pad = (-n_c) % 8
        padded = cands + ([jnp.full((pad, s1.shape[1]), NEG_INF, F32)] if pad else [])
        best, _ = _extract_top(jnp.concatenate(padded, axis=0), k)
        m0 = best[0]
        tau = best[k - 1]
        z = jnp.zeros_like(m0)
        for bv in best:
            z = z + jnp.exp(bv - m0)
        nsel = jnp.zeros_like(s1)
        for i in range(k):
            cnt = jnp.sum(jnp.where(cands[i] >= tau, 1.0, 0.0), axis=0, keepdims=True)
            nsel = jnp.where(s1 == ta[i], cnt, nsel)
        nsel_ref[h] = nsel
        e1_ref[h] = jnp.exp(s1 - ta[0]) / z
        rank2_ref[h] = rank2.astype(BF16)
        e2_ref[h] = jnp.exp(s2 - tb[0]).astype(BF16)
        return carry

    lax.fori_loop(0, PEER_HEADS, per_head, 0)


def _peer_topk(scores, tt=256):
    nb, nk, t = scores.shape
    tt = min(tt, t)
    big = pl.BlockSpec((PEER_HEADS, nk, tt), lambda i: (0, 0, i))
    f32 = jax.ShapeDtypeStruct((PEER_HEADS, nk, t), F32)
    b16 = jax.ShapeDtypeStruct((PEER_HEADS, nk, t), BF16)
    return pl.pallas_call(
        _peer_topk_kernel,
        grid=(t // tt,),
        in_specs=[pl.BlockSpec((nb, nk, tt), lambda i: (0, 0, i))],
        out_specs=[big, big, big, big],
        out_shape=[f32, f32, b16, b16],
        compiler_params=_cparams(("parallel",)),
    )(scores)


def _peer_dense_kernel(ht_ref, u0_ref, ub_ref, un_ref, v_ref, ns0_ref, e10_ref, nsc_ref, e1c_ref,
                       nsn_ref, e1n_ref, rank2_ref, e2_ref, h1_ref, lg_ref, lb_ref,
                       o_ref, hsa_ref, hsb_ref, *, q, nj):
    j = pl.program_id(1)
    nk = PEER_NKEYS
    tm = ht_ref.shape[1]
    half = q * nk
    ht = ht_ref[...]

    def gate_part(gate, heads, ns_ref, e1_ref, row, live):
        for h in heads:
            nsel = ns_ref[h, 0, row:row + 1, :]
            nsel = nsel if live is None else nsel * live
            nsel = pltpu.repeat(jnp.broadcast_to(nsel, (16, tm)).astype(BF16), nk // 16, axis=0)
            e1 = pltpu.repeat(jnp.broadcast_to(e1_ref[h, 0, row:row + 1, :], (16, tm)).astype(BF16),
                              nk // 16, axis=0)
            gate = gate + e1 * jnp.where(rank2_ref[h] < nsel, e2_ref[h], 0.0)
        return gate

    def hidden_block(u_ref, ns_ref, e1_ref, row, qi, live, dst_ref, values=None):
        gate = gate_part(jnp.zeros((nk, tm), BF16), range(0, PEER_HEADS // 2), ns_ref, e1_ref, row, live)
        if values is not None:
            values()
        gate = gate_part(gate, range(PEER_HEADS // 2, PEER_HEADS), ns_ref, e1_ref, row, live)
        act = jnp.dot(u_ref[qi * nk:(qi + 1) * nk, :], ht, preferred_element_type=F32)
        c0 = math.sqrt(2.0 / math.pi)
        inner = act * (c0 + (c0 * 0.044715) * (act * act))
        half_act = 0.5 * act.astype(BF16)
        hid = (half_act + half_act * jnp.tanh(inner.astype(BF16))) * gate
        dst_ref[:, qi * nk:(qi + 1) * nk] = hid.T

    @pl.when(j == 0)
    def _():
        o_ref[...] = jnp.zeros_like(o_ref)
        for qi in range(q):
            hidden_block(u0_ref, ns0_ref, e10_ref, qi, qi, None, hsa_ref)

    def phase(src_ref, v_rows, u_ref, ns_ref, e1_ref, row0, live, dst_ref):
        wv = D_MODEL // q
        for qi in range(q):
            cols = slice(qi * wv, (qi + 1) * wv)

            def values():
                o_ref[:, cols] += jnp.dot(src_ref[...], v_ref[v_rows, cols], preferred_element_type=F32)

            hidden_block(u_ref, ns_ref, e1_ref, row0 + qi, qi, live, dst_ref, values)

    phase(hsa_ref, slice(0, half), ub_ref, nsc_ref, e1c_ref, q, None, hsb_ref)
    more = jnp.where(j + 1 < nj, 1.0, 0.0)
    phase(hsb_ref, slice(half, 2 * half), un_ref, nsn_ref, e1n_ref, 0, more, hsa_ref)

    @pl.when(j == nj - 1)
    def _():
        o_ref[...] = _layer_norm(DEEPNORM_ALPHA * h1_ref[...] + o_ref[...], lg_ref[...], lb_ref[...])


def _peer_dense(ht, u_tab, v_tab, nsel, e1, rank2, e2, h1, ln_g, ln_b, tm=512, te=1024):
    t = ht.shape[1]
    tm = min(tm, t)
    half = te // 2
    q = half // PEER_NKEYS
    ng = PEER_NKEYS // (2 * q)
    nj = PEER_EXPERTS // te
    rows = lambda a: a.reshape(PEER_HEADS, ng, 2 * q, t)
    nxt = lambda j: jnp.minimum(j + 1, nj - 1)
    u_spec = lambda f: pl.BlockSpec((half, D_MODEL), lambda i, j: (f(j), 0))
    row_spec = lambda f: pl.BlockSpec((PEER_HEADS, 1, 2 * q, tm), lambda i, j: (0, f(j), 0, i))
    key_spec = pl.BlockSpec((PEER_HEADS, PEER_NKEYS, tm), lambda i, j: (0, 0, i))
    vec_spec = pl.BlockSpec((1, D_MODEL), lambda i, j: (0, 0))
    first, cur = (lambda j: 0), (lambda j: j)
    return pl.pallas_call(
        functools.partial(_peer_dense_kernel, q=q, nj=nj),
        grid=(t // tm, nj),
        in_specs=[pl.BlockSpec((D_MODEL, tm), lambda i, j: (0, i)),
                  u_spec(first), u_spec(lambda j: 2 * j + 1), u_spec(lambda j: 2 * nxt(j)),
                  pl.BlockSpec((te, D_MODEL), lambda i, j: (j, 0)),
                  row_spec(first), row_spec(first), row_spec(cur), row_spec(cur),
                  row_spec(nxt), row_spec(nxt), key_spec, key_spec,
                  pl.BlockSpec((tm, D_MODEL), lambda i, j: (i, 0)), vec_spec, vec_spec],
        out_specs=pl.BlockSpec((tm, D_MODEL), lambda i, j: (i, 0)),
        out_shape=jax.ShapeDtypeStruct((t, D_MODEL), F32),
        scratch_shapes=[pltpu.VMEM((tm, half), BF16), pltpu.VMEM((tm, half), BF16)],
        compiler_params=_cparams(("parallel", "arbitrary")),
    )(ht, u_tab, u_tab, u_tab, v_tab, rows(nsel), rows(e1), rows(nsel), rows(e1),
      rows(nsel), rows(e1), rank2, e2, h1, ln_g, ln_b)


def _layer(h, bsz, seq, w_in, s5, rw, w_out, ln1, peer, ln2):
    (lam_re, lam_im, log_step, b_re, b_im, c_re, c_im, d_skip, w_glu, b_glu) = s5
    (mu, w0, w_up, a0, a_up, g_up, k_k, k_a, r_k, gn_g, gn_b) = rw
    (w_q, keys1, keys2, u_tab, v_tab) = peer
    hb = h.astype(BF16)
    n_rkv = 3 * D_RWKV
    w_in_b = w_in.astype(BF16)
    pad = RW_LORA_PAD - RW_LORA

    u = _matmul(hb, w_in_b[:, :D_SSM], F32, 1024, 1024)
    p_rkv = _matmul(hb, w_in_b[:, D_SSM:D_SSM + n_rkv], BF16, 1024, 1024)
    p_lora = _matmul(hb, jnp.pad(w_in_b[:, D_SSM + n_rkv:], ((0, 0), (0, pad))), F32, 512, RW_LORA_PAD)

    yg = _s5_scan(u, _s5_params(lam_re, lam_im, log_step, b_re, b_im, c_re, c_im, d_skip,
                                seq // S5_CHUNK), bsz, seq)
    y_ssm = _glu(yg, w_glu.astype(BF16), b_glu[None, :])

    mu_rkv = mu[None, :n_rkv]
    mu_lora = jnp.pad(mu[n_rkv:], (0, pad))[None, :]
    zrow = lambda n: jnp.zeros((n, D_RWKV), F32)
    wup = jnp.concatenate([w_up, zrow(RW_LORA_PAD - RW_W_LORA)], axis=0).astype(BF16)
    aup = jnp.concatenate([zrow(RW_W_LORA), a_up, zrow(RW_LORA_PAD - RW_W_LORA - RW_A_LORA)],
                          axis=0).astype(BF16)
    gup = jnp.concatenate([zrow(RW_W_LORA + RW_A_LORA), g_up, zrow(pad)], axis=0).astype(BF16)
    r, k, v, kkr, a, lw, g = _rw_prep(p_rkv, p_lora, mu_rkv, mu_lora, w0[None], a0[None],
                                      k_k[None], k_a[None], wup, aup, gup, seq)
    q, oi, bonus, m, n = _rw_chunks(r, k, v, kkr, a, lw, r_k.reshape(1, D_RWKV))
    y_rw = _rw_seq(q, oi, bonus, g, m, n, gn_g[None], gn_b[None], bsz, seq)

    h1, h1b, h1t = _outproj(y_ssm, y_rw, w_out.astype(BF16), h, ln1[0][None], ln1[1][None])

    keys = jnp.stack([keys1, keys2], axis=1).reshape(2 * PEER_HEADS, PEER_NKEYS, PEER_HALF)
    scores = _peer_scores(h1b, w_q.astype(BF16), keys.astype(BF16))
    nsel, e1, rank2, e2 = _peer_topk(scores)
    return _peer_dense(h1t, u_tab.astype(BF16), v_tab.astype(BF16), nsel, e1, rank2, e2,
                       h1, ln2[0][None], ln2[1][None])


def kernel(x, w_in, s5_lam_re, s5_lam_im, s5_log_step, s5_b_re, s5_b_im, s5_c_re, s5_c_im, s5_d, s5_w_glu, s5_b_glu, rw_mu, rw_w0, rw_w_up, rw_a0, rw_a_up, rw_g_up, rw_k_k, rw_k_a, rw_r_k, rw_gn_g, rw_gn_b, w_out, ln1_g, ln1_b, peer_w_q, peer_keys1, peer_keys2, peer_u, peer_v, ln2_g, ln2_b):
    bsz, seq, dim = x.shape
    h = x.reshape(bsz * seq, dim)
    for l in range(w_in.shape[0]):
        h = _layer(
            h, bsz, seq, w_in[l],
            (s5_lam_re[l], s5_lam_im[l], s5_log_step[l], s5_b_re[l], s5_b_im[l], s5_c_re[l],
             s5_c_im[l], s5_d[l], s5_w_glu[l], s5_b_glu[l]),
            (rw_mu[l], rw_w0[l], rw_w_up[l], rw_a0[l], rw_a_up[l], rw_g_up[l], rw_k_k[l],
             rw_k_a[l], rw_r_k[l], rw_gn_g[l], rw_gn_b[l]),
            w_out[l], (ln1_g[l], ln1_b[l]),
            (peer_w_q[l], peer_keys1[l], peer_keys2[l], peer_u[l], peer_v[l]),
            (ln2_g[l], ln2_b[l]))
    return h.reshape(bsz, seq, dim)
```

```python
import functools
import math

import jax
import jax.numpy as jnp
from jax import lax
from jax.experimental import pallas as pl
from jax.experimental.pallas import tpu as pltpu

F32 = jnp.float32
BF16 = jnp.bfloat16

D_MODEL = 2048
D_SSM = 1024
D_RWKV = 1024
S5_CH = 16
S5_GROUPS = D_SSM // S5_CH
S5_STATE = 64
S5_CHUNK = 16
RW_HEAD = 64
RW_HEADS = D_RWKV // RW_HEAD
RW_W_LORA = 64
RW_A_LORA = 64
RW_G_LORA = 160
RW_LORA = RW_W_LORA + RW_A_LORA + RW_G_LORA
RW_LORA_PAD = 384
RW_CHUNK = 64
PEER_HEADS = 8
PEER_NKEYS = 128
PEER_EXPERTS = PEER_NKEYS * PEER_NKEYS
PEER_HALF = 128
PEER_TOPK = 16
DEPTH = 1
DEEPNORM_ALPHA = (2.0 * DEPTH) ** 0.25
LN_EPS = 1e-5
RW_GN_EPS = 64e-5
NEG_INF = float("-inf")

VMEM_LIMIT = 56 * 1024 * 1024


def _cparams(sem):
    return pltpu.CompilerParams(dimension_semantics=sem, vmem_limit_bytes=VMEM_LIMIT)


def _dot(a, b):
    return jnp.dot(a.astype(BF16), b.astype(BF16), preferred_element_type=F32)


def _dot_nt(a, b):
    return lax.dot_general(a.astype(BF16), b.astype(BF16), (((1,), (1,)), ((), ())),
                           preferred_element_type=F32)


def _dot_tn(a, b):
    return lax.dot_general(a.astype(BF16), b.astype(BF16), (((0,), (0,)), ((), ())),
                           preferred_element_type=F32)


def _gelu(x):
    c = math.sqrt(2.0 / math.pi)
    return 0.5 * x * (1.0 + jnp.tanh(c * (x + 0.044715 * (x * x * x))))


def _sigmoid(x):
    return 1.0 / (1.0 + jnp.exp(-x))


def _head_sum(x, ones_ref):
    x_hi = x.astype(BF16)
    x_lo = (x - x_hi.astype(F32)).astype(BF16)
    w = ones_ref.shape[0]
    return jnp.concatenate(
        [jnp.dot(x_hi[:, i:i + w], ones_ref[...], preferred_element_type=F32)
         + jnp.dot(x_lo[:, i:i + w], ones_ref[...], preferred_element_type=F32)
         for i in range(0, x.shape[1], w)], axis=1)


def _head_ones():
    hid = jnp.arange(4 * RW_HEAD) // RW_HEAD
    return (hid[:, None] == hid[None, :]).astype(BF16)


def _mm_kernel(a_ref, b_ref, o_ref):
    o_ref[...] = jnp.dot(a_ref[...], b_ref[...],
                         preferred_element_type=F32).astype(o_ref.dtype)


def _matmul(a, b, out_dtype, tm, tn, col0=0, n=None):
    m, k = a.shape
    n = b.shape[1] if n is None else n
    tm = min(tm, m)
    assert col0 % tn == 0 and n % tn == 0
    return pl.pallas_call(
        _mm_kernel,
        grid=(m // tm, n // tn),
        in_specs=[pl.BlockSpec((tm, k), lambda i, j: (i, 0)),
                  pl.BlockSpec((k, tn), lambda i, j: (0, j + col0 // tn))],
        out_specs=pl.BlockSpec((tm, tn), lambda i, j: (i, j)),
        out_shape=jax.ShapeDtypeStruct((m, n), out_dtype),
        compiler_params=_cparams(("parallel", "parallel")),
    )(a, b)


def _s5_kernel(u_ref, mt_ref, bc_ref, cc_ref, ar_ref, ai_ref, d_ref, o_ref, *, nc):
    c = S5_CHUNK
    rows = u_ref.shape[0] // c
    groups = u_ref.shape[1] // S5_CH
    xs = [u_ref[pl.ds(tl, rows, stride=c), :] for tl in range(c)]
    cidx = lax.broadcasted_iota(jnp.int32, (rows, 2 * S5_STATE), 0) & (nc - 1)
    ys = []
    for g in range(groups):
        lanes = slice(g * S5_CH, (g + 1) * S5_CH)
        u = jnp.concatenate([x[:, lanes] for x in xs], axis=1).astype(BF16)
        y = jnp.dot(u, mt_ref[g], preferred_element_type=F32)
        s = jnp.dot(u, bc_ref[g], preferred_element_type=F32)
        d, k = 1, 0
        while d < nc:
            sh = jnp.where(cidx >= d, pltpu.roll(s, d, axis=0), 0.0)
            shs = pltpu.roll(sh, S5_STATE, axis=1)
            s = s + sh * ar_ref[g, k:k + 1, :] + shs * ai_ref[g, k:k + 1, :]
            d, k = d * 2, k + 1
        sp = jnp.where(cidx >= 1, pltpu.roll(s, 1, axis=0), 0.0)
        ys.append(y + jnp.dot(sp.astype(BF16), cc_ref[g], preferred_element_type=F32))
    for tl in range(c):
        y = jnp.concatenate([yg[:, tl * S5_CH:(tl + 1) * S5_CH] for yg in ys], axis=1)
        o_ref[pl.ds(tl, rows, stride=c), :] = _gelu(y + d_ref[...] * xs[tl])


def _s5_params(lam_re, lam_im, log_step, b_re, b_im, c_re, c_im, d_skip, nc):
    hi = lax.Precision.HIGHEST
    c = S5_CHUNK
    step = jnp.exp(log_step)[:, None]
    a_re = jnp.exp(lam_re * step) * jnp.cos(lam_im * step)
    a_im = jnp.exp(lam_re * step) * jnp.sin(lam_im * step)
    den = lam_re * lam_re + lam_im * lam_im
    f_re = ((a_re - 1.0) * lam_re + a_im * lam_im) / den
    f_im = (a_im * lam_re - (a_re - 1.0) * lam_im) / den
    bb_re = f_re[..., None] * b_re - f_im[..., None] * b_im
    bb_im = f_re[..., None] * b_im + f_im[..., None] * b_re

    def power(j):
        jj = j[None, :, None]
        mag = jnp.exp(lam_re[:, None, :] * step[:, None, :] * jj)
        ang = lam_im[:, None, :] * step[:, None, :] * jj
        return mag * jnp.cos(ang), mag * jnp.sin(ang)

    pw_re, pw_im = power(jnp.arange(c + 1, dtype=F32))
    ce_re = c_re[:, None] * pw_re[:, :, None, :] - c_im[:, None] * pw_im[:, :, None, :]
    ce_im = c_re[:, None] * pw_im[:, :, None, :] + c_im[:, None] * pw_re[:, :, None, :]
    kern = jnp.einsum('gjop,gpi->gjoi', jnp.concatenate([ce_re[:, :c], -ce_im[:, :c]], axis=-1),
                      jnp.concatenate([bb_re, bb_im], axis=1), precision=hi)
    t_idx = jnp.arange(c)
    lag = t_idx[None, :] - t_idx[:, None]
    place = (lag[:, :, None] == t_idx[None, None, :]).astype(F32)
    g = lam_re.shape[0]
    mt = jnp.einsum('stj,gjoi->gsito', place, kern, precision=hi)
    mt = mt.reshape(g, c * S5_CH, c * S5_CH)
    rv_re, rv_im = pw_re[:, c - 1::-1][:, :c], pw_im[:, c - 1::-1][:, :c]
    bc_r = rv_re[:, :, None, :] * jnp.transpose(bb_re, (0, 2, 1))[:, None] \
        - rv_im[:, :, None, :] * jnp.transpose(bb_im, (0, 2, 1))[:, None]
    bc_i = rv_re[:, :, None, :] * jnp.transpose(bb_im, (0, 2, 1))[:, None] \
        + rv_im[:, :, None, :] * jnp.transpose(bb_re, (0, 2, 1))[:, None]
    bc = jnp.concatenate([bc_r, bc_i], axis=-1).reshape(g, c * S5_CH, 2 * S5_STATE)
    cc_r = jnp.transpose(ce_re[:, 1:], (0, 3, 1, 2)).reshape(g, S5_STATE, c * S5_CH)
    cc_i = -jnp.transpose(ce_im[:, 1:], (0, 3, 1, 2)).reshape(g, S5_STATE, c * S5_CH)
    cc = jnp.concatenate([cc_r, cc_i], axis=1)
    sc_re, sc_im = power(c * (2.0 ** jnp.arange(8, dtype=F32)))
    ar = jnp.concatenate([sc_re, sc_re], axis=-1)
    ai = jnp.concatenate([-sc_im, sc_im], axis=-1)
    return mt.astype(BF16), bc.astype(BF16), cc.astype(BF16), ar, ai, d_skip[None, :]


def _s5_scan(u_tok, params, bsz, seq, gpb=8):
    mt, bc, cc, ar, ai, d_row = params
    nc = seq // S5_CHUNK
    t = bsz * seq
    w = S5_CHUNK * S5_CH
    lanes = gpb * S5_CH
    tok = pl.BlockSpec((t, lanes), lambda i: (0, i))
    spec3 = lambda a, b: pl.BlockSpec((gpb, a, b), lambda i: (i, 0, 0))
    return pl.pallas_call(
        functools.partial(_s5_kernel, nc=nc),
        grid=(S5_GROUPS // gpb,),
        in_specs=[tok, spec3(w, w), spec3(w, 2 * S5_STATE), spec3(2 * S5_STATE, w),
                  spec3(8, 2 * S5_STATE), spec3(8, 2 * S5_STATE),
                  pl.BlockSpec((1, lanes), lambda i: (0, i))],
        out_specs=tok,
        out_shape=jax.ShapeDtypeStruct((t, D_SSM), F32),
        compiler_params=_cparams(("parallel",)),
    )(u_tok, mt, bc, cc, ar, ai, d_row)


def _glu_kernel(y_ref, w_ref, b_ref, o_ref):
    y = y_ref[...]
    z = jnp.dot(y.astype(BF16), w_ref[...], preferred_element_type=F32) + b_ref[...]
    o_ref[...] = (y * _sigmoid(z)).astype(o_ref.dtype)


def _glu(y, w, b, tm=512):
    m, n = y.shape
    tm = min(tm, m)
    return pl.pallas_call(
        _glu_kernel,
        grid=(m // tm,),
        in_specs=[pl.BlockSpec((tm, n), lambda i: (i, 0)),
                  pl.BlockSpec((n, n), lambda i: (0, 0)),
                  pl.BlockSpec((1, n), lambda i: (0, 0))],
        out_specs=pl.BlockSpec((tm, n), lambda i: (i, 0)),
        out_shape=jax.ShapeDtypeStruct((m, n), BF16),
        compiler_params=_cparams(("parallel",)),
    )(y, w, b)


def _rw_prep_kernel(p_ref, pp_ref, l_ref, lp_ref, mu_ref, mul_ref, w0_ref, a0_ref,
                    kk_ref, ka_ref, wup_ref, aup_ref, gup_ref,
                    r_ref, k_ref, v_ref, kkr_ref, a_ref, lw_ref, g_ref, *, tiles_per_seq):
    first = (pl.program_id(0) % tiles_per_seq) == 0

    def shifted(cur_ref, prev_ref, mu):
        cur = cur_ref[...].astype(F32)
        prev_row = jnp.where(first, 0.0, prev_ref[15:16, :].astype(F32))
        row = lax.broadcasted_iota(jnp.int32, cur.shape, 0)
        prev = jnp.where(row == 0, prev_row, pltpu.roll(cur, 1, axis=0))
        return cur + mu * (prev - cur)

    p = shifted(p_ref, pp_ref, mu_ref[...])
    lo = shifted(l_ref, lp_ref, mul_ref[...])
    r = p[:, :D_RWKV]
    k = p[:, D_RWKV:2 * D_RWKV]
    v = p[:, 2 * D_RWKV:]
    w_pre = w0_ref[...] + _dot(jnp.tanh(lo), wup_ref[...])
    a = _sigmoid(a0_ref[...] + _dot(lo, aup_ref[...]))
    g = _dot(_sigmoid(lo), gup_ref[...])
    z = -w_pre
    softplus = jnp.maximum(z, 0.0) + jnp.log(1.0 + jnp.exp(-jnp.abs(z)))
    w = -softplus - 0.5
    r_ref[...] = r.astype(r_ref.dtype)
    k_ref[...] = (k * (1.0 + (a - 1.0) * ka_ref[...])).astype(k_ref.dtype)
    v_ref[...] = v.astype(v_ref.dtype)
    kkr_ref[...] = (k * kk_ref[...]).astype(kkr_ref.dtype)
    a_ref[...] = a.astype(a_ref.dtype)
    lw_ref[...] = -jnp.exp(w)
    g_ref[...] = g.astype(g_ref.dtype)


def _rw_prep(p_rkv, p_lora, mu_rkv, mu_lora, w0, a0, k_k, k_a, wup, aup, gup, seq, tm=256):
    t = p_rkv.shape[0]
    tm = min(tm, seq)
    n3 = 3 * D_RWKV
    row = lambda n: pl.BlockSpec((1, n), lambda i: (0, 0))
    full = lambda a, b: pl.BlockSpec((a, b), lambda i: (0, 0))
    prev = lambda n: pl.BlockSpec((16, n), lambda i: (jnp.maximum(i * (tm // 16) - 1, 0), 0))
    out = lambda dt: jax.ShapeDtypeStruct((t, D_RWKV), dt)
    ospec = pl.BlockSpec((tm, D_RWKV), lambda i: (i, 0))
    return pl.pallas_call(
        functools.partial(_rw_prep_kernel, tiles_per_seq=seq // tm),
        grid=(t // tm,),
        in_specs=[pl.BlockSpec((tm, n3), lambda i: (i, 0)), prev(n3),
                  pl.BlockSpec((tm, RW_LORA_PAD), lambda i: (i, 0)), prev(RW_LORA_PAD),
                  row(n3), row(RW_LORA_PAD), row(D_RWKV), row(D_RWKV), row(D_RWKV), row(D_RWKV),
                  full(RW_LORA_PAD, D_RWKV), full(RW_LORA_PAD, D_RWKV), full(RW_LORA_PAD, D_RWKV)],
        out_specs=[ospec] * 7,
        out_shape=[out(BF16)] * 5 + [out(F32), out(BF16)],
        compiler_params=_cparams(("parallel",)),
    )(p_rkv, p_rkv, p_lora, p_lora, mu_rkv, mu_lora, w0, a0, k_k, k_a, wup, aup, gup)


def _rw_chunk_kernel(r_ref, k_ref, v_ref, kkr_ref, a_ref, lw_ref, rk_ref, ones_ref,
                     q_ref, oi_ref, bonus_ref, m_ref, n_ref):
    c = RW_CHUNK
    hd = RW_HEAD
    heads = range(r_ref.shape[1] // hd)
    ri = lax.broadcasted_iota(jnp.int32, (c, c), 0)
    ci = lax.broadcasted_iota(jnp.int32, (c, c), 1)
    tri_incl = (ci <= ri)
    tri_strict = (ci < ri)
    ltri = tri_incl.astype(BF16)
    eye = (ci == ri).astype(F32)
    lw = lw_ref[...]
    lw_hi = lw.astype(BF16)
    lw_lo = (lw - lw_hi.astype(F32)).astype(BF16)
    cl = (jnp.dot(ltri, lw_hi, preferred_element_type=F32)
          + jnp.dot(ltri, lw_lo, preferred_element_type=F32))
    cl_end = cl[c - 1:c, :]
    r = r_ref[...].astype(F32)
    k = k_ref[...].astype(F32)
    rt_d = r * jnp.exp(cl)
    e_neg_d = jnp.exp(-cl)
    kt_d = k * e_neg_d
    e_prev_d = jnp.exp(cl - lw)
    e_end_d = jnp.exp(cl_end - cl)
    kte_d = k * e_end_d
    gam_d = jnp.exp(cl_end)
    kkr_d = kkr_ref[...].astype(F32)

    nrm = jnp.sqrt(_head_sum(kkr_d * kkr_d, ones_ref))
    kk_d = kkr_d / jnp.maximum(nrm, 1e-12)
    b_d = kk_d * a_ref[...].astype(F32)
    bte_d = b_d * e_end_d
    kr_d = jnp.concatenate([kk_d * e_prev_d, rt_d], axis=0)
    kb_d = jnp.concatenate([kt_d, b_d * e_neg_d], axis=0)
    v_d = v_ref[...]
    bonus_ref[...] = _head_sum(r * k * rk_ref[0:1, :], ones_ref) * v_d.astype(F32)
    sl = [slice(h * hd, (h + 1) * hd) for h in heads]
    v = [v_d[:, s] for s in sl]
    rt = [rt_d[:, s] for s in sl]
    kkt = [kr_d[:c, s] for s in sl]
    bte = [bte_d[:, s] for s in sl]
    a4 = [_dot_nt(kr_d[:, s], kb_d[:, s]) for s in sl]
    a_kb = [jnp.where(tri_strict, a4[h][:c, c:], 0.0) for h in heads]
    a_rb = [jnp.where(tri_incl, a4[h][c:, c:], 0.0) for h in heads]
    a_kr = [jnp.concatenate([jnp.where(tri_strict, a4[h][:c, :c], 0.0),
                             jnp.where(tri_incl, a4[h][c:, :c], 0.0)], axis=0) for h in heads]
    x = [eye - a_kb[h] for h in heads]
    pw = [_dot(a_kb[h], a_kb[h]) for h in heads]
    n_sq = int(math.log2(c)) - 1
    for it in range(n_sq):
        if it + 1 < n_sq:
            xp = [_dot(jnp.concatenate([x[h], pw[h]], axis=0), pw[h]) for h in heads]
            x = [x[h] + xp[h][:c] for h in heads]
            pw = [xp[h][c:] for h in heads]
        else:
            x = [x[h] + _dot(x[h], pw[h]) for h in heads]
    av = [_dot(a_kr[h], v[h]) for h in heads]
    wu = [_dot(x[h], jnp.concatenate([kkt[h], av[h][:c]], axis=1)) for h in heads]
    rb = [_dot(a_rb[h], wu[h]) for h in heads]
    wub = [_dot_tn(wu[h], bte[h]) for h in heads]
    vk = [_dot_tn(v[h], kte_d[:, sl[h]]) for h in heads]
    ji = lax.broadcasted_iota(jnp.int32, (hd, hd), 0)
    jo = lax.broadcasted_iota(jnp.int32, (hd, hd), 1)
    for h in heads:
        m_ref[0, h] = jnp.where(ji == jo, gam_d[:, sl[h]], 0.0) - wub[h][:hd]
        n_ref[0, h] = vk[h] - wub[h][hd:]
    q_ref[...] = jnp.concatenate([rt[h] - rb[h][:, :hd] for h in heads], axis=1)
    oi_ref[...] = jnp.concatenate([av[h][c:] - rb[h][:, hd:] for h in heads], axis=1)


def _rw_chunks(r, k, v, kkr, a, lw, r_k, hp=RW_HEADS):
    t = r.shape[0]
    c = RW_CHUNK
    nch = t // c
    wdt = hp * RW_HEAD
    blk = pl.BlockSpec((c, wdt), lambda i, j: (i, j))
    mat = pl.BlockSpec((1, hp, RW_HEAD, RW_HEAD), lambda i, j: (i, j, 0, 0))
    tok = jax.ShapeDtypeStruct((t, D_RWKV), F32)
    mshape = jax.ShapeDtypeStruct((nch, RW_HEADS, RW_HEAD, RW_HEAD), F32)
    return pl.pallas_call(
        _rw_chunk_kernel,
        grid=(nch, RW_HEADS // hp),
        in_specs=[blk] * 6 + [pl.BlockSpec((1, wdt), lambda i, j: (0, j)),
                  pl.BlockSpec((4 * RW_HEAD, 4 * RW_HEAD), lambda i, j: (0, 0))],
        out_specs=[blk, blk, blk, mat, mat],
        out_shape=[tok, tok, tok, mshape, mshape],
        compiler_params=_cparams(("parallel", "parallel")),
    )(r, k, v, kkr, a, lw, r_k, _head_ones())


def _rw_seq_kernel(q_ref, oi_ref, bonus_ref, g_ref, m_ref, n_ref, gg_ref, gb_ref, ones_ref,
                   o_ref, st_ref):
    @pl.when(pl.program_id(0) == 0)
    def _():
        st_ref[...] = jnp.zeros_like(st_ref)

    hd = RW_HEAD
    sl = [slice(h * hd, (h + 1) * hd) for h in range(RW_HEADS)]
    streams = [(b, h) for b in range(q_ref.shape[0]) for h in range(RW_HEADS)]
    s = {bh: st_ref[bh[0], bh[1]] for bh in streams}
    q = [q_ref[b] for b in range(q_ref.shape[0])]
    o = {(b, h): _dot_nt(q[b][:, sl[h]], s[b, h]) for b, h in streams}
    for b, h in streams:
        st_ref[b, h] = _dot(s[b, h], m_ref[b, 0, h]) + n_ref[b, 0, h]
    inv_n = 1.0 / hd
    for b in range(q_ref.shape[0]):
        ob = jnp.concatenate([o[b, h] for h in range(RW_HEADS)], axis=1) + oi_ref[b]
        ctr = ob - _head_sum(ob, ones_ref) * inv_n
        var = _head_sum(ctr * ctr, ones_ref) * inv_n
        y = ctr * lax.rsqrt(var + RW_GN_EPS) * gg_ref[...] + gb_ref[...] + bonus_ref[b]
        o_ref[b] = (y * g_ref[b].astype(F32)).astype(o_ref.dtype)


def _rw_seq(q, oi, bonus, g, m, n, gn_g, gn_b, bsz, seq):
    c = RW_CHUNK
    nch = seq // c
    r3 = lambda x: x.reshape(bsz, seq, D_RWKV)
    r5 = lambda x: x.reshape(bsz, nch, RW_HEADS, RW_HEAD, RW_HEAD)
    tok = pl.BlockSpec((bsz, c, D_RWKV), lambda i: (0, i, 0))
    mat = pl.BlockSpec((bsz, 1, RW_HEADS, RW_HEAD, RW_HEAD), lambda i: (0, i, 0, 0, 0))
    row = pl.BlockSpec((1, D_RWKV), lambda i: (0, 0))
    y = pl.pallas_call(
        _rw_seq_kernel,
        grid=(nch,),
        in_specs=[tok, tok, tok, tok, mat, mat, row, row,
                  pl.BlockSpec((4 * RW_HEAD, 4 * RW_HEAD), lambda i: (0, 0))],
        out_specs=tok,
        out_shape=jax.ShapeDtypeStruct((bsz, seq, D_RWKV), BF16),
        scratch_shapes=[pltpu.VMEM((bsz, RW_HEADS, RW_HEAD, RW_HEAD), F32)],
        compiler_params=_cparams(("arbitrary",)),
    )(r3(q), r3(oi), r3(bonus), r3(g), r5(m), r5(n), gn_g, gn_b, _head_ones())
    return y.reshape(bsz * seq, D_RWKV)


def _layer_norm(x, g, b):
    mu = jnp.mean(x, axis=-1, keepdims=True)
    var = jnp.mean(jnp.square(x - mu), axis=-1, keepdims=True)
    return (x - mu) * lax.rsqrt(var + LN_EPS) * g + b


def _outproj_kernel(ys_ref, yr_ref, w1_ref, w2_ref, x_ref, g_ref, b_ref, h_ref, hb_ref, ht_ref):
    mix = (jnp.dot(ys_ref[...], w1_ref[...], preferred_element_type=F32)
           + jnp.dot(yr_ref[...], w2_ref[...], preferred_element_type=F32))
    h = _layer_norm(DEEPNORM_ALPHA * x_ref[...] + mix, g_ref[...], b_ref[...])
    h_ref[...] = h
    hb = h.astype(BF16)
    hb_ref[...] = hb
    ht_ref[...] = hb.T


def _outproj(ys, yr, w_out, x, g, b, tm=512):
    t = x.shape[0]
    tm = min(tm, t)
    half = pl.BlockSpec((tm, D_SSM), lambda i: (i, 0))
    full = pl.BlockSpec((tm, D_MODEL), lambda i: (i, 0))
    row = pl.BlockSpec((1, D_MODEL), lambda i: (0, 0))
    return pl.pallas_call(
        _outproj_kernel,
        grid=(t // tm,),
        in_specs=[half, half,
                  pl.BlockSpec((D_SSM, D_MODEL), lambda i: (0, 0)),
                  pl.BlockSpec((D_RWKV, D_MODEL), lambda i: (1, 0)),
                  full, row, row],
        out_specs=[full, full, pl.BlockSpec((D_MODEL, tm), lambda i: (0, i))],
        out_shape=[jax.ShapeDtypeStruct((t, D_MODEL), F32),
                   jax.ShapeDtypeStruct((t, D_MODEL), BF16),
                   jax.ShapeDtypeStruct((D_MODEL, t), BF16)],
        compiler_params=_cparams(("parallel",)),
    )(ys, yr, w_out, w_out, x, g, b)


def _peer_scores_kernel(h_ref, wq_ref, keys_ref, s_ref):
    q = jnp.dot(h_ref[...], wq_ref[...], preferred_element_type=F32)
    for blk in range(2 * PEER_HEADS):
        qb = q[:, blk * PEER_HALF:(blk + 1) * PEER_HALF]
        s_ref[blk] = _dot_nt(keys_ref[blk], qb)


def _peer_scores(hb, wq, keys, tm=512):
    t = hb.shape[0]
    tm = min(tm, t)
    nb = 2 * PEER_HEADS
    return pl.pallas_call(
        _peer_scores_kernel,
        grid=(t // tm,),
        in_specs=[pl.BlockSpec((tm, D_MODEL), lambda i: (i, 0)),
                  pl.BlockSpec((D_MODEL, D_MODEL), lambda i: (0, 0)),
                  pl.BlockSpec((nb, PEER_NKEYS, PEER_HALF), lambda i: (0, 0, 0))],
        out_specs=pl.BlockSpec((nb, PEER_NKEYS, tm), lambda i: (0, 0, i)),
        out_shape=jax.ShapeDtypeStruct((nb, PEER_NKEYS, t), F32),
        compiler_params=_cparams(("parallel",)),
    )(hb, wq, keys)


NO_RANK = 127.0


def _extract_top(x, n, want_rank=False):
    vals = []
    rank = jnp.full(x.shape, NO_RANK, F32) if want_rank else None
    for i in range(n):
        m = jnp.max(x, axis=0, keepdims=True)
        vals.append(m)
        hit = x == m
        if want_rank:
            rank = jnp.where(hit, float(i), rank)
        x = jnp.where(hit, NEG_INF, x)
    return vals, rank


def _peer_topk_kernel(s_ref, nsel_ref, e1_ref, rank2_ref, e2_ref):
    k = PEER_TOPK

    def per_head(h, carry):
        s1 = s_ref[2 * h]
        s2 = s_ref[2 * h + 1]
        ta, _ = _extract_top(s1, k)
        tb, rank2 = _extract_top(s2, k, want_rank=True)
        tbs = jnp.concatenate(tb, axis=0)
        cands = [ta[i] + tbs[0:k // (i + 1), :] for i in range(k)]
        n_c = sum(k // (i + 1) for i in range(k))
        pad = (-n_c) % 8
        padded = cands + ([jnp.full((pad, s1.shape[1]), NEG_INF, F32)] if pad else [])
        best, _ = _extract_top(jnp.concatenate(padded, axis=0), k)
        m0 = best[0]
        tau = best[k - 1]
        z = jnp.zeros_like(m0)
        for bv in best:
            z = z + jnp.exp(bv - m0)
        nsel = jnp.zeros_like(s1)
        for i in range(k):
            cnt = jnp.sum(jnp.where(cands[i] >= tau, 1.0, 0.0), axis=0, keepdims=True)
            nsel = jnp.where(s1 == ta[i], cnt, nsel)
        nsel_ref[h] = nsel
        e1_ref[h] = jnp.exp(s1 - ta[0]) / z
        rank2_ref[h] = rank2.astype(BF16)
        e2_ref[h] = jnp.exp(s2 - tb[0]).astype(BF16)
        return carry

    lax.fori_loop(0, PEER_HEADS, per_head, 0)


def _peer_topk(scores, tt=256):
    nb, nk, t = scores.shape
    tt = min(tt, t)
    big = pl.BlockSpec((PEER_HEADS, nk, tt), lambda i: (0, 0, i))
    f32 = jax.ShapeDtypeStruct((PEER_HEADS, nk, t), F32)
    b16 = jax.ShapeDtypeStruct((PEER_HEADS, nk, t), BF16)
    return pl.pallas_call(
        _peer_topk_kernel,
        grid=(t // tt,),
        in_specs=[pl.BlockSpec((nb, nk, tt), lambda i: (0, 0, i))],
        out_specs=[big, big, big, big],
        out_shape=[f32, f32, b16, b16],
        compiler_params=_cparams(("parallel",)),
    )(scores)


def _peer_dense_kernel(ht_ref, u0_ref, ub_ref, un_ref, v_ref, ns0_ref, e10_ref, nsc_ref, e1c_ref,
                       nsn_ref, e1n_ref, rank2_ref, e2_ref, h1_ref, lg_ref, lb_ref,
                       o_ref, hsa_ref, hsb_ref, *, q, nj):
    j = pl.program_id(1)
    nk = PEER_NKEYS
    tm = ht_ref.shape[1]
    half = q * nk
    ht = ht_ref[...]

    def gate_part(gate, heads, ns_ref, e1_ref, row, live):
        for h in heads:
            nsel = ns_ref[h, 0, row:row + 1, :]
            nsel = nsel if live is None else nsel * live
            nsel = jnp.concatenate([jnp.broadcast_to(nsel, (16, tm)).astype(BF16)] * (nk // 16), axis=0)
            e1 = jnp.concatenate(
                [jnp.broadcast_to(e1_ref[h, 0, row:row + 1, :], (16, tm)).astype(BF16)] * (nk // 16), axis=0)
            gate = gate + e1 * jnp.where(rank2_ref[h] < nsel, e2_ref[h], 0.0)
        return gate

    def hidden_block(u_ref, ns_ref, e1_ref, row, qi, live, dst_ref, values=None):
        gate = gate_part(jnp.zeros((nk, tm), BF16), range(0, PEER_HEADS // 2), ns_ref, e1_ref, row, live)
        if values is not None:
            values()
        gate = gate_part(gate, range(PEER_HEADS // 2, PEER_HEADS), ns_ref, e1_ref, row, live)
        act = jnp.dot(u_ref[qi * nk:(qi + 1) * nk, :], ht, preferred_element_type=F32)
        c0 = math.sqrt(2.0 / math.pi)
        inner = act * (c0 + (c0 * 0.044715) * (act * act))
        half_act = 0.5 * act.astype(BF16)
        hid = (half_act + half_act * jnp.tanh(inner.astype(BF16))) * gate
        dst_ref[:, qi * nk:(qi + 1) * nk] = hid.T

    @pl.when(j == 0)
    def _():
        o_ref[...] = jnp.zeros_like(o_ref)
        for qi in range(q):
            hidden_block(u0_ref, ns0_ref, e10_ref, qi, qi, None, hsa_ref)

    def phase(src_ref, v_rows, u_ref, ns_ref, e1_ref, row0, live, dst_ref):
        wv = D_MODEL // q
        for qi in range(q):
            cols = slice(qi * wv, (qi + 1) * wv)

            def values():
                o_ref[:, cols] += jnp.dot(src_ref[...], v_ref[v_rows, cols], preferred_element_type=F32)

            hidden_block(u_ref, ns_ref, e1_ref, row0 + qi, qi, live, dst_ref, values)

    phase(hsa_ref, slice(0, half), ub_ref, nsc_ref, e1c_ref, q, None, hsb_ref)
    more = jnp.where(j + 1 < nj, 1.0, 0.0)
    phase(hsb_ref, slice(half, 2 * half), un_ref, nsn_ref, e1n_ref, 0, more, hsa_ref)

    @pl.when(j == nj - 1)
    def _():
        o_ref[...] = _layer_norm(DEEPNORM_ALPHA * h1_ref[...] + o_ref[...], lg_ref[...], lb_ref[...])


def _peer_dense(ht, u_tab, v_tab, nsel, e1, rank2, e2, h1, ln_g, ln_b, tm=512, te=1024):
    t = ht.shape[1]
    tm = min(tm, t)
    half = te // 2
    q = half // PEER_NKEYS
    ng = PEER_NKEYS // (2 * q)
    nj = PEER_EXPERTS // te
    rows = lambda a: a.reshape(PEER_HEADS, ng, 2 * q, t)
    nxt = lambda j: jnp.minimum(j + 1, nj - 1)
    u_spec = lambda f: pl.BlockSpec((half, D_MODEL), lambda i, j: (f(j), 0))
    row_spec = lambda f: pl.BlockSpec((PEER_HEADS, 1, 2 * q, tm), lambda i, j: (0, f(j), 0, i))
    key_spec = pl.BlockSpec((PEER_HEADS, PEER_NKEYS, tm), lambda i, j: (0, 0, i))
    vec_spec = pl.BlockSpec((1, D_MODEL), lambda i, j: (0, 0))
    first, cur = (lambda j: 0), (lambda j: j)
    return pl.pallas_call(
        functools.partial(_peer_dense_kernel, q=q, nj=nj),
        grid=(t // tm, nj),
        in_specs=[pl.BlockSpec((D_MODEL, tm), lambda i, j: (0, i)),
                  u_spec(first), u_spec(lambda j: 2 * j + 1), u_spec(lambda j: 2 * nxt(j)),
                  pl.BlockSpec((te, D_MODEL), lambda i, j: (j, 0)),
                  row_spec(first), row_spec(first), row_spec(cur), row_spec(cur),
                  row_spec(nxt), row_spec(nxt), key_spec, key_spec,
                  pl.BlockSpec((tm, D_MODEL), lambda i, j: (i, 0)), vec_spec, vec_spec],
        out_specs=pl.BlockSpec((tm, D_MODEL), lambda i, j: (i, 0)),
        out_shape=jax.ShapeDtypeStruct((t, D_MODEL), F32),
        scratch_shapes=[pltpu.VMEM((tm, half), BF16), pltpu.VMEM((tm, half), BF16)],
        compiler_params=_cparams(("parallel", "arbitrary")),
    )(ht, u_tab, u_tab, u_tab, v_tab, rows(nsel), rows(e1), rows(nsel), rows(e1),
      rows(nsel), rows(e1), rank2, e2, h1, ln_g, ln_b)


def _layer(h, bsz, seq, w_in, s5, rw, w_out, ln1, peer, ln2):
    (lam_re, lam_im, log_step, b_re, b_im, c_re, c_im, d_skip, w_glu, b_glu) = s5
    (mu, w0, w_up, a0, a_up, g_up, k_k, k_a, r_k, gn_g, gn_b) = rw
    (w_q, keys1, keys2, u_tab, v_tab) = peer
    hb = h.astype(BF16)
    n_rkv = 3 * D_RWKV
    w_in_b = w_in.astype(BF16)
    pad = RW_LORA_PAD - RW_LORA

    u = _matmul(hb, w_in_b, F32, 1024, 1024, 0, D_SSM)
    p_rkv = _matmul(hb, w_in_b, BF16, 1024, 1024, D_SSM, n_rkv)
    p_lora = _matmul(hb, jnp.pad(w_in_b[:, D_SSM + n_rkv:], ((0, 0), (0, pad))), F32, 512, RW_LORA_PAD)

    yg = _s5_scan(u, _s5_params(lam_re, lam_im, log_step, b_re, b_im, c_re, c_im, d_skip,
                                seq // S5_CHUNK), bsz, seq)
    y_ssm = _glu(yg, w_glu.astype(BF16), b_glu[None, :])

    mu_rkv = mu[None, :n_rkv]
    mu_lora = jnp.pad(mu[n_rkv:], (0, pad))[None, :]
    zrow = lambda n: jnp.zeros((n, D_RWKV), F32)
    wup = jnp.concatenate([w_up, zrow(RW_LORA_PAD - RW_W_LORA)], axis=0).astype(BF16)
    aup = jnp.concatenate([zrow(RW_W_LORA), a_up, zrow(RW_LORA_PAD - RW_W_LORA - RW_A_LORA)],
                          axis=0).astype(BF16)
    gup = jnp.concatenate([zrow(RW_W_LORA + RW_A_LORA), g_up, zrow(pad)], axis=0).astype(BF16)
    r, k, v, kkr, a, lw, g = _rw_prep(p_rkv, p_lora, mu_rkv, mu_lora, w0[None], a0[None],
                                      k_k[None], k_a[None], wup, aup, gup, seq)
    q, oi, bonus, m, n = _rw_chunks(r, k, v, kkr, a, lw, r_k.reshape(1, D_RWKV))
    y_rw = _rw_seq(q, oi, bonus, g, m, n, gn_g[None], gn_b[None], bsz, seq)

    h1, h1b, h1t = _outproj(y_ssm, y_rw, w_out.astype(BF16), h, ln1[0][None], ln1[1][None])

    keys = jnp.stack([keys1, keys2], axis=1).reshape(2 * PEER_HEADS, PEER_NKEYS, PEER_HALF)
    scores = _peer_scores(h1b, w_q.astype(BF16), keys.astype(BF16))
    nsel, e1, rank2, e2 = _peer_topk(scores)
    return _peer_dense(h1t, u_tab.astype(BF16), v_tab.astype(BF16), nsel, e1, rank2, e2,
                       h1, ln2[0][None], ln2[1][None])


def kernel(x, w_in, s5_lam_re, s5_lam_im, s5_log_step, s5_b_re, s5_b_im, s5_c_re, s5_c_im, s5_d, s5_w_glu, s5_b_glu, rw_mu, rw_w0, rw_w_up, rw_a0, rw_a_up, rw_g_up, rw_k_k, rw_k_a, rw_r_k, rw_gn_g, rw_gn_b, w_out, ln1_g, ln1_b, peer_w_q, peer_keys1, peer_keys2, peer_u, peer_v, ln2_g, ln2_b):
    bsz, seq, dim = x.shape
    h = x.reshape(bsz * seq, dim)
    for l in range(w_in.shape[0]):
        h = _layer(
            h, bsz, seq, w_in[l],
            (s5_lam_re[l], s5_lam_im[l], s5_log_step[l], s5_b_re[l], s5_b_im[l], s5_c_re[l],
             s5_c_im[l], s5_d[l], s5_w_glu[l], s5_b_glu[l]),
            (rw_mu[l], rw_w0[l], rw_w_up[l], rw_a0[l], rw_a_up[l], rw_g_up[l], rw_k_k[l],
             rw_k_a[l], rw_r_k[l], rw_gn_g[l], rw_gn_b[l]),
            w_out[l], (ln1_g[l], ln1_b[l]),
            (peer_w_q[l], peer_keys1[l], peer_keys2[l], peer_u[l], peer_v[l]),
            (ln2_g[l], ln2_b[l]))
    return h.reshape(bsz, seq, dim)
```

```python
import functools
import math

import jax
import jax.numpy as jnp
from jax import lax
from jax.experimental import pallas as pl
from jax.experimental.pallas import tpu as pltpu

F32 = jnp.float32
BF16 = jnp.bfloat16
F8 = jnp.float8_e4m3fn
F8_TARGET = 224.0

D_MODEL = 2048
D_SSM = 1024
D_RWKV = 1024
S5_CH = 16
S5_GROUPS = D_SSM // S5_CH
S5_STATE = 64
S5_CHUNK = 16
RW_HEAD = 64
RW_HEADS = D_RWKV // RW_HEAD
RW_W_LORA = 64
RW_A_LORA = 64
RW_G_LORA = 160
RW_LORA = RW_W_LORA + RW_A_LORA + RW_G_LORA
RW_LORA_PAD = 384
RW_CHUNK = 64
PEER_HEADS = 8
PEER_NKEYS = 128
PEER_EXPERTS = PEER_NKEYS * PEER_NKEYS
PEER_HALF = 128
PEER_TOPK = 16
DEPTH = 1
DEEPNORM_ALPHA = (2.0 * DEPTH) ** 0.25
LN_EPS = 1e-5
RW_GN_EPS = 64e-5
NEG_INF = float("-inf")

VMEM_LIMIT = 56 * 1024 * 1024


def _cparams(sem):
    return pltpu.CompilerParams(dimension_semantics=sem, vmem_limit_bytes=VMEM_LIMIT)


def _dot(a, b):
    return jnp.dot(a.astype(BF16), b.astype(BF16), preferred_element_type=F32)


def _dot_nt(a, b):
    return lax.dot_general(a.astype(BF16), b.astype(BF16), (((1,), (1,)), ((), ())),
                           preferred_element_type=F32)


def _dot_tn(a, b):
    return lax.dot_general(a.astype(BF16), b.astype(BF16), (((0,), (0,)), ((), ())),
                           preferred_element_type=F32)


def _gelu(x):
    c = math.sqrt(2.0 / math.pi)
    return 0.5 * x * (1.0 + jnp.tanh(c * (x + 0.044715 * (x * x * x))))


def _sigmoid(x):
    return 1.0 / (1.0 + jnp.exp(-x))


def _head_sum(x, ones_ref):
    x_hi = x.astype(BF16)
    x_lo = (x - x_hi.astype(F32)).astype(BF16)
    w = ones_ref.shape[0]
    return jnp.concatenate(
        [jnp.dot(x_hi[:, i:i + w], ones_ref[...], preferred_element_type=F32)
         + jnp.dot(x_lo[:, i:i + w], ones_ref[...], preferred_element_type=F32)
         for i in range(0, x.shape[1], w)], axis=1)


def _pow2_scale(amax):
    return jnp.exp2(jnp.floor(jnp.log2(F8_TARGET / jnp.maximum(amax, 1e-30))))


def _head_ones():
    hid = jnp.arange(4 * RW_HEAD) // RW_HEAD
    return (hid[:, None] == hid[None, :]).astype(BF16)


def _mm_kernel(a_ref, b_ref, o_ref):
    o_ref[...] = jnp.dot(a_ref[...], b_ref[...],
                         preferred_element_type=F32).astype(o_ref.dtype)


def _matmul(a, b, out_dtype, tm, tn, col0=0, n=None):
    m, k = a.shape
    n = b.shape[1] if n is None else n
    tm = min(tm, m)
    assert col0 % tn == 0 and n % tn == 0
    return pl.pallas_call(
        _mm_kernel,
        grid=(m // tm, n // tn),
        in_specs=[pl.BlockSpec((tm, k), lambda i, j: (i, 0)),
                  pl.BlockSpec((k, tn), lambda i, j: (0, j + col0 // tn))],
        out_specs=pl.BlockSpec((tm, tn), lambda i, j: (i, j)),
        out_shape=jax.ShapeDtypeStruct((m, n), out_dtype),
        compiler_params=_cparams(("parallel", "parallel")),
    )(a, b)


def _s5_kernel(u_ref, mt_ref, bc_ref, cc_ref, ar_ref, ai_ref, d_ref, o_ref, *, nc):
    c = S5_CHUNK
    rows = u_ref.shape[0] // c
    groups = u_ref.shape[1] // S5_CH
    xs = [u_ref[pl.ds(tl, rows, stride=c), :] for tl in range(c)]
    cidx = lax.broadcasted_iota(jnp.int32, (rows, 2 * S5_STATE), 0) & (nc - 1)
    ys = []
    for g in range(groups):
        lanes = slice(g * S5_CH, (g + 1) * S5_CH)
        u = jnp.concatenate([x[:, lanes] for x in xs], axis=1).astype(BF16)
        y = jnp.dot(u, mt_ref[g], preferred_element_type=F32)
        s = jnp.dot(u, bc_ref[g], preferred_element_type=F32)
        d, k = 1, 0
        while d < nc:
            sh = jnp.where(cidx >= d, pltpu.roll(s, d, axis=0), 0.0)
            shs = pltpu.roll(sh, S5_STATE, axis=1)
            s = s + sh * ar_ref[g, k:k + 1, :] + shs * ai_ref[g, k:k + 1, :]
            d, k = d * 2, k + 1
        sp = jnp.where(cidx >= 1, pltpu.roll(s, 1, axis=0), 0.0)
        ys.append(y + jnp.dot(sp.astype(BF16), cc_ref[g], preferred_element_type=F32))
    for tl in range(c):
        y = jnp.concatenate([yg[:, tl * S5_CH:(tl + 1) * S5_CH] for yg in ys], axis=1)
        o_ref[pl.ds(tl, rows, stride=c), :] = _gelu(y + d_ref[...] * xs[tl])


def _s5_params(lam_re, lam_im, log_step, b_re, b_im, c_re, c_im, d_skip, nc):
    hi = lax.Precision.HIGHEST
    c = S5_CHUNK
    step = jnp.exp(log_step)[:, None]
    a_re = jnp.exp(lam_re * step) * jnp.cos(lam_im * step)
    a_im = jnp.exp(lam_re * step) * jnp.sin(lam_im * step)
    den = lam_re * lam_re + lam_im * lam_im
    f_re = ((a_re - 1.0) * lam_re + a_im * lam_im) / den
    f_im = (a_im * lam_re - (a_re - 1.0) * lam_im) / den
    bb_re = f_re[..., None] * b_re - f_im[..., None] * b_im
    bb_im = f_re[..., None] * b_im + f_im[..., None] * b_re

    def power(j):
        jj = j[None, :, None]
        mag = jnp.exp(lam_re[:, None, :] * step[:, None, :] * jj)
        ang = lam_im[:, None, :] * step[:, None, :] * jj
        return mag * jnp.cos(ang), mag * jnp.sin(ang)

    pw_re, pw_im = power(jnp.arange(c + 1, dtype=F32))
    ce_re = c_re[:, None] * pw_re[:, :, None, :] - c_im[:, None] * pw_im[:, :, None, :]
    ce_im = c_re[:, None] * pw_im[:, :, None, :] + c_im[:, None] * pw_re[:, :, None, :]
    kern = jnp.einsum('gjop,gpi->gjoi', jnp.concatenate([ce_re[:, :c], -ce_im[:, :c]], axis=-1),
                      jnp.concatenate([bb_re, bb_im], axis=1), precision=hi)
    t_idx = jnp.arange(c)
    lag = t_idx[None, :] - t_idx[:, None]
    place = (lag[:, :, None] == t_idx[None, None, :]).astype(F32)
    g = lam_re.shape[0]
    mt = jnp.einsum('stj,gjoi->gsito', place, kern, precision=hi)
    mt = mt.reshape(g, c * S5_CH, c * S5_CH)
    rv_re, rv_im = pw_re[:, c - 1::-1][:, :c], pw_im[:, c - 1::-1][:, :c]
    bc_r = rv_re[:, :, None, :] * jnp.transpose(bb_re, (0, 2, 1))[:, None] \
        - rv_im[:, :, None, :] * jnp.transpose(bb_im, (0, 2, 1))[:, None]
    bc_i = rv_re[:, :, None, :] * jnp.transpose(bb_im, (0, 2, 1))[:, None] \
        + rv_im[:, :, None, :] * jnp.transpose(bb_re, (0, 2, 1))[:, None]
    bc = jnp.concatenate([bc_r, bc_i], axis=-1).reshape(g, c * S5_CH, 2 * S5_STATE)
    cc_r = jnp.transpose(ce_re[:, 1:], (0, 3, 1, 2)).reshape(g, S5_STATE, c * S5_CH)
    cc_i = -jnp.transpose(ce_im[:, 1:], (0, 3, 1, 2)).reshape(g, S5_STATE, c * S5_CH)
    cc = jnp.concatenate([cc_r, cc_i], axis=1)
    sc_re, sc_im = power(c * (2.0 ** jnp.arange(8, dtype=F32)))
    ar = jnp.concatenate([sc_re, sc_re], axis=-1)
    ai = jnp.concatenate([-sc_im, sc_im], axis=-1)
    return mt.astype(BF16), bc.astype(BF16), cc.astype(BF16), ar, ai, d_skip[None, :]


def _s5_scan(u_tok, params, bsz, seq, gpb=8):
    mt, bc, cc, ar, ai, d_row = params
    nc = seq // S5_CHUNK
    t = bsz * seq
    w = S5_CHUNK * S5_CH
    lanes = gpb * S5_CH
    tok = pl.BlockSpec((t, lanes), lambda i: (0, i))
    spec3 = lambda a, b: pl.BlockSpec((gpb, a, b), lambda i: (i, 0, 0))
    return pl.pallas_call(
        functools.partial(_s5_kernel, nc=nc),
        grid=(S5_GROUPS // gpb,),
        in_specs=[tok, spec3(w, w), spec3(w, 2 * S5_STATE), spec3(2 * S5_STATE, w),
                  spec3(8, 2 * S5_STATE), spec3(8, 2 * S5_STATE),
                  pl.BlockSpec((1, lanes), lambda i: (0, i))],
        out_specs=tok,
        out_shape=jax.ShapeDtypeStruct((t, D_SSM), F32),
        compiler_params=_cparams(("parallel",)),
    )(u_tok, mt, bc, cc, ar, ai, d_row)


def _glu_kernel(y_ref, w_ref, b_ref, o_ref):
    y = y_ref[...]
    z = jnp.dot(y.astype(BF16), w_ref[...], preferred_element_type=F32) + b_ref[...]
    o_ref[...] = (y * _sigmoid(z)).astype(o_ref.dtype)


def _glu(y, w, b, tm=512):
    m, n = y.shape
    tm = min(tm, m)
    return pl.pallas_call(
        _glu_kernel,
        grid=(m // tm,),
        in_specs=[pl.BlockSpec((tm, n), lambda i: (i, 0)),
                  pl.BlockSpec((n, n), lambda i: (0, 0)),
                  pl.BlockSpec((1, n), lambda i: (0, 0))],
        out_specs=pl.BlockSpec((tm, n), lambda i: (i, 0)),
        out_shape=jax.ShapeDtypeStruct((m, n), BF16),
        compiler_params=_cparams(("parallel",)),
    )(y, w, b)


def _rw_prep_kernel(p_ref, pp_ref, l_ref, lp_ref, mu_ref, mul_ref, w0_ref, a0_ref,
                    kk_ref, ka_ref, wup_ref, aup_ref, gup_ref,
                    r_ref, k_ref, v_ref, kkr_ref, a_ref, lw_ref, g_ref, *, tiles_per_seq):
    first = (pl.program_id(0) % tiles_per_seq) == 0

    def shifted(cur_ref, prev_ref, mu):
        cur = cur_ref[...].astype(F32)
        prev_row = jnp.where(first, 0.0, prev_ref[15:16, :].astype(F32))
        row = lax.broadcasted_iota(jnp.int32, cur.shape, 0)
        prev = jnp.where(row == 0, prev_row, pltpu.roll(cur, 1, axis=0))
        return cur + mu * (prev - cur)

    p = shifted(p_ref, pp_ref, mu_ref[...])
    lo = shifted(l_ref, lp_ref, mul_ref[...])
    r = p[:, :D_RWKV]
    k = p[:, D_RWKV:2 * D_RWKV]
    v = p[:, 2 * D_RWKV:]
    w_pre = w0_ref[...] + _dot(jnp.tanh(lo), wup_ref[...])
    a = _sigmoid(a0_ref[...] + _dot(lo, aup_ref[...]))
    g = _dot(_sigmoid(lo), gup_ref[...])
    z = -w_pre
    softplus = jnp.maximum(z, 0.0) + jnp.log(1.0 + jnp.exp(-jnp.abs(z)))
    w = -softplus - 0.5
    r_ref[...] = r.astype(r_ref.dtype)
    k_ref[...] = (k * (1.0 + (a - 1.0) * ka_ref[...])).astype(k_ref.dtype)
    v_ref[...] = v.astype(v_ref.dtype)
    kkr_ref[...] = (k * kk_ref[...]).astype(kkr_ref.dtype)
    a_ref[...] = a.astype(a_ref.dtype)
    lw_ref[...] = -jnp.exp(w)
    g_ref[...] = g.astype(g_ref.dtype)


def _rw_prep(p_rkv, p_lora, mu_rkv, mu_lora, w0, a0, k_k, k_a, wup, aup, gup, seq, tm=256):
    t = p_rkv.shape[0]
    tm = min(tm, seq)
    n3 = 3 * D_RWKV
    row = lambda n: pl.BlockSpec((1, n), lambda i: (0, 0))
    full = lambda a, b: pl.BlockSpec((a, b), lambda i: (0, 0))
    prev = lambda n: pl.BlockSpec((16, n), lambda i: (jnp.maximum(i * (tm // 16) - 1, 0), 0))
    out = lambda dt: jax.ShapeDtypeStruct((t, D_RWKV), dt)
    ospec = pl.BlockSpec((tm, D_RWKV), lambda i: (i, 0))
    return pl.pallas_call(
        functools.partial(_rw_prep_kernel, tiles_per_seq=seq // tm),
        grid=(t // tm,),
        in_specs=[pl.BlockSpec((tm, n3), lambda i: (i, 0)), prev(n3),
                  pl.BlockSpec((tm, RW_LORA_PAD), lambda i: (i, 0)), prev(RW_LORA_PAD),
                  row(n3), row(RW_LORA_PAD), row(D_RWKV), row(D_RWKV), row(D_RWKV), row(D_RWKV),
                  full(RW_LORA_PAD, D_RWKV), full(RW_LORA_PAD, D_RWKV), full(RW_LORA_PAD, D_RWKV)],
        out_specs=[ospec] * 7,
        out_shape=[out(BF16)] * 5 + [out(F32), out(BF16)],
        compiler_params=_cparams(("parallel",)),
    )(p_rkv, p_rkv, p_lora, p_lora, mu_rkv, mu_lora, w0, a0, k_k, k_a, wup, aup, gup)


def _rw_chunk_kernel(r_ref, k_ref, v_ref, kkr_ref, a_ref, lw_ref, rk_ref, ones_ref,
                     q_ref, oi_ref, bonus_ref, m_ref, n_ref):
    c = RW_CHUNK
    hd = RW_HEAD
    heads = range(r_ref.shape[1] // hd)
    ri = lax.broadcasted_iota(jnp.int32, (c, c), 0)
    ci = lax.broadcasted_iota(jnp.int32, (c, c), 1)
    tri_incl = (ci <= ri)
    tri_strict = (ci < ri)
    ltri = tri_incl.astype(BF16)
    eye = (ci == ri).astype(F32)
    lw = lw_ref[...]
    lw_hi = lw.astype(BF16)
    lw_lo = (lw - lw_hi.astype(F32)).astype(BF16)
    cl = (jnp.dot(ltri, lw_hi, preferred_element_type=F32)
          + jnp.dot(ltri, lw_lo, preferred_element_type=F32))
    cl_end = cl[c - 1:c, :]
    r = r_ref[...].astype(F32)
    k = k_ref[...].astype(F32)
    rt_d = r * jnp.exp(cl)
    e_neg_d = jnp.exp(-cl)
    kt_d = k * e_neg_d
    e_prev_d = jnp.exp(cl - lw)
    e_end_d = jnp.exp(cl_end - cl)
    kte_d = k * e_end_d
    gam_d = jnp.exp(cl_end)
    kkr_d = kkr_ref[...].astype(F32)

    nrm = jnp.sqrt(_head_sum(kkr_d * kkr_d, ones_ref))
    kk_d = kkr_d / jnp.maximum(nrm, 1e-12)
    b_d = kk_d * a_ref[...].astype(F32)
    bte_d = b_d * e_end_d
    kr_d = jnp.concatenate([kk_d * e_prev_d, rt_d], axis=0)
    kb_d = jnp.concatenate([kt_d, b_d * e_neg_d], axis=0)
    v_d = v_ref[...]
    bonus_ref[...] = _head_sum(r * k * rk_ref[0:1, :], ones_ref) * v_d.astype(F32)
    sl = [slice(h * hd, (h + 1) * hd) for h in heads]
    v = [v_d[:, s] for s in sl]
    rt = [rt_d[:, s] for s in sl]
    kkt = [kr_d[:c, s] for s in sl]
    bte = [bte_d[:, s] for s in sl]
    a4 = [_dot_nt(kr_d[:, s], kb_d[:, s]) for s in sl]
    a_kb = [jnp.where(tri_strict, a4[h][:c, c:], 0.0) for h in heads]
    a_rb = [jnp.where(tri_incl, a4[h][c:, c:], 0.0) for h in heads]
    a_kr = [jnp.concatenate([jnp.where(tri_strict, a4[h][:c, :c], 0.0),
                             jnp.where(tri_incl, a4[h][c:, :c], 0.0)], axis=0) for h in heads]
    x = [eye - a_kb[h] for h in heads]
    pw = [_dot(a_kb[h], a_kb[h]) for h in heads]
    n_sq = int(math.log2(c)) - 1
    for it in range(n_sq):
        if it + 1 < n_sq:
            xp = [_dot(jnp.concatenate([x[h], pw[h]], axis=0), pw[h]) for h in heads]
            x = [x[h] + xp[h][:c] for h in heads]
            pw = [xp[h][c:] for h in heads]
        else:
            x = [x[h] + _dot(x[h], pw[h]) for h in heads]
    av = [_dot(a_kr[h], v[h]) for h in heads]
    wu = [_dot(x[h], jnp.concatenate([kkt[h], av[h][:c]], axis=1)) for h in heads]
    rb = [_dot(a_rb[h], wu[h]) for h in heads]
    wub = [_dot_tn(wu[h], bte[h]) for h in heads]
    vk = [_dot_tn(v[h], kte_d[:, sl[h]]) for h in heads]
    ji = lax.broadcasted_iota(jnp.int32, (hd, hd), 0)
    jo = lax.broadcasted_iota(jnp.int32, (hd, hd), 1)
    for h in heads:
        m_ref[0, h] = jnp.where(ji == jo, gam_d[:, sl[h]], 0.0) - wub[h][:hd]
        n_ref[0, h] = vk[h] - wub[h][hd:]
    q_ref[...] = jnp.concatenate([rt[h] - rb[h][:, :hd] for h in heads], axis=1)
    oi_ref[...] = jnp.concatenate([av[h][c:] - rb[h][:, hd:] for h in heads], axis=1)


def _rw_chunks(r, k, v, kkr, a, lw, r_k, hp=RW_HEADS):
    t = r.shape[0]
    c = RW_CHUNK
    nch = t // c
    wdt = hp * RW_HEAD
    blk = pl.BlockSpec((c, wdt), lambda i, j: (i, j))
    mat = pl.BlockSpec((1, hp, RW_HEAD, RW_HEAD), lambda i, j: (i, j, 0, 0))
    tok = jax.ShapeDtypeStruct((t, D_RWKV), F32)
    mshape = jax.ShapeDtypeStruct((nch, RW_HEADS, RW_HEAD, RW_HEAD), F32)
    return pl.pallas_call(
        _rw_chunk_kernel,
        grid=(nch, RW_HEADS // hp),
        in_specs=[blk] * 6 + [pl.BlockSpec((1, wdt), lambda i, j: (0, j)),
                  pl.BlockSpec((4 * RW_HEAD, 4 * RW_HEAD), lambda i, j: (0, 0))],
        out_specs=[blk, blk, blk, mat, mat],
        out_shape=[tok, tok, tok, mshape, mshape],
        compiler_params=_cparams(("parallel", "parallel")),
    )(r, k, v, kkr, a, lw, r_k, _head_ones())


def _rw_seq_kernel(q_ref, oi_ref, bonus_ref, g_ref, m_ref, n_ref, gg_ref, gb_ref, ones_ref,
                   o_ref, st_ref):
    @pl.when(pl.program_id(0) == 0)
    def _():
        st_ref[...] = jnp.zeros_like(st_ref)

    hd = RW_HEAD
    sl = [slice(h * hd, (h + 1) * hd) for h in range(RW_HEADS)]
    streams = [(b, h) for b in range(q_ref.shape[0]) for h in range(RW_HEADS)]
    s = {bh: st_ref[bh[0], bh[1]] for bh in streams}
    q = [q_ref[b] for b in range(q_ref.shape[0])]
    o = {(b, h): _dot_nt(q[b][:, sl[h]], s[b, h]) for b, h in streams}
    for b, h in streams:
        st_ref[b, h] = _dot(s[b, h], m_ref[b, 0, h]) + n_ref[b, 0, h]
    inv_n = 1.0 / hd
    for b in range(q_ref.shape[0]):
        ob = jnp.concatenate([o[b, h] for h in range(RW_HEADS)], axis=1) + oi_ref[b]
        ctr = ob - _head_sum(ob, ones_ref) * inv_n
        var = _head_sum(ctr * ctr, ones_ref) * inv_n
        y = ctr * lax.rsqrt(var + RW_GN_EPS) * gg_ref[...] + gb_ref[...] + bonus_ref[b]
        o_ref[b] = (y * g_ref[b].astype(F32)).astype(o_ref.dtype)


def _rw_seq(q, oi, bonus, g, m, n, gn_g, gn_b, bsz, seq):
    c = RW_CHUNK
    nch = seq // c
    r3 = lambda x: x.reshape(bsz, seq, D_RWKV)
    r5 = lambda x: x.reshape(bsz, nch, RW_HEADS, RW_HEAD, RW_HEAD)
    tok = pl.BlockSpec((bsz, c, D_RWKV), lambda i: (0, i, 0))
    mat = pl.BlockSpec((bsz, 1, RW_HEADS, RW_HEAD, RW_HEAD), lambda i: (0, i, 0, 0, 0))
    row = pl.BlockSpec((1, D_RWKV), lambda i: (0, 0))
    y = pl.pallas_call(
        _rw_seq_kernel,
        grid=(nch,),
        in_specs=[tok, tok, tok, tok, mat, mat, row, row,
                  pl.BlockSpec((4 * RW_HEAD, 4 * RW_HEAD), lambda i: (0, 0))],
        out_specs=tok,
        out_shape=jax.ShapeDtypeStruct((bsz, seq, D_RWKV), BF16),
        scratch_shapes=[pltpu.VMEM((bsz, RW_HEADS, RW_HEAD, RW_HEAD), F32)],
        compiler_params=_cparams(("arbitrary",)),
    )(r3(q), r3(oi), r3(bonus), r3(g), r5(m), r5(n), gn_g, gn_b, _head_ones())
    return y.reshape(bsz * seq, D_RWKV)


def _layer_norm(x, g, b):
    mu = jnp.mean(x, axis=-1, keepdims=True)
    var = jnp.mean(jnp.square(x - mu), axis=-1, keepdims=True)
    return (x - mu) * lax.rsqrt(var + LN_EPS) * g + b


def _outproj_kernel(ys_ref, yr_ref, w1_ref, w2_ref, x_ref, g_ref, b_ref, h_ref, hb_ref, ht_ref, isc_ref):
    mix = (jnp.dot(ys_ref[...], w1_ref[...], preferred_element_type=F32)
           + jnp.dot(yr_ref[...], w2_ref[...], preferred_element_type=F32))
    h = _layer_norm(DEEPNORM_ALPHA * x_ref[...] + mix, g_ref[...], b_ref[...])
    h_ref[...] = h
    hb_ref[...] = h.astype(BF16)
    ht = h.T
    sx = _pow2_scale(jnp.max(jnp.abs(ht), axis=0, keepdims=True))
    ht_ref[...] = (ht * sx).astype(F8)
    isc_ref[...] = 1.0 / sx


def _outproj(ys, yr, w_out, x, g, b, tm=512):
    t = x.shape[0]
    tm = min(tm, t)
    half = pl.BlockSpec((tm, D_SSM), lambda i: (i, 0))
    full = pl.BlockSpec((tm, D_MODEL), lambda i: (i, 0))
    row = pl.BlockSpec((1, D_MODEL), lambda i: (0, 0))
    return pl.pallas_call(
        _outproj_kernel,
        grid=(t // tm,),
        in_specs=[half, half,
                  pl.BlockSpec((D_SSM, D_MODEL), lambda i: (0, 0)),
                  pl.BlockSpec((D_RWKV, D_MODEL), lambda i: (1, 0)),
                  full, row, row],
        out_specs=[full, full, pl.BlockSpec((D_MODEL, tm), lambda i: (0, i)),
                   pl.BlockSpec((1, tm), lambda i: (0, i))],
        out_shape=[jax.ShapeDtypeStruct((t, D_MODEL), F32),
                   jax.ShapeDtypeStruct((t, D_MODEL), BF16),
                   jax.ShapeDtypeStruct((D_MODEL, t), F8),
                   jax.ShapeDtypeStruct((1, t), F32)],
        compiler_params=_cparams(("parallel",)),
    )(ys, yr, w_out, w_out, x, g, b)


def _peer_scores_kernel(h_ref, wq_ref, keys_ref, s_ref):
    q = jnp.dot(h_ref[...], wq_ref[...], preferred_element_type=F32)
    for blk in range(2 * PEER_HEADS):
        qb = q[:, blk * PEER_HALF:(blk + 1) * PEER_HALF]
        s_ref[blk] = _dot_nt(keys_ref[blk], qb)


def _peer_scores(hb, wq, keys, tm=512):
    t = hb.shape[0]
    tm = min(tm, t)
    nb = 2 * PEER_HEADS
    return pl.pallas_call(
        _peer_scores_kernel,
        grid=(t // tm,),
        in_specs=[pl.BlockSpec((tm, D_MODEL), lambda i: (i, 0)),
                  pl.BlockSpec((D_MODEL, D_MODEL), lambda i: (0, 0)),
                  pl.BlockSpec((nb, PEER_NKEYS, PEER_HALF), lambda i: (0, 0, 0))],
        out_specs=pl.BlockSpec((nb, PEER_NKEYS, tm), lambda i: (0, 0, i)),
        out_shape=jax.ShapeDtypeStruct((nb, PEER_NKEYS, t), F32),
        compiler_params=_cparams(("parallel",)),
    )(hb, wq, keys)


NO_RANK = 127.0


def _extract_top(x, n, want_rank=False):
    vals = []
    rank = jnp.full(x.shape, NO_RANK, F32) if want_rank else None
    for i in range(n):
        m = jnp.max(x, axis=0, keepdims=True)
        vals.append(m)
        hit = x == m
        if want_rank:
            rank = jnp.where(hit, float(i), rank)
        x = jnp.where(hit, NEG_INF, x)
    return vals, rank


def _peer_topk_kernel(s_ref, nsel_ref, e1_ref, rank2_ref, e2_ref):
    k = PEER_TOPK

    def per_head(h, carry):
        s1 = s_ref[2 * h]
        s2 = s_ref[2 * h + 1]
        ta, _ = _extract_top(s1, k)
        tb, rank2 = _extract_top(s2, k, want_rank=True)
        tbs = jnp.concatenate(tb, axis=0)
        cands = [ta[i] + tbs[0:k // (i + 1), :] for i in range(k)]
        n_c = sum(k // (i + 1) for i in range(k))
        pad = (-n_c) % 8
        padded = cands + ([jnp.full((pad, s1.shape[1]), NEG_INF, F32)] if pad else [])
        best, _ = _extract_top(jnp.concatenate(padded, axis=0), k)
        m0 = best[0]
        tau = best[k - 1]
        z = jnp.zeros_like(m0)
        for bv in best:
            z = z + jnp.exp(bv - m0)
        nsel = jnp.zeros_like(s1)
        for i in range(k):
            cnt = jnp.sum(jnp.where(cands[i] >= tau, 1.0, 0.0), axis=0, keepdims=True)
            nsel = jnp.where(s1 == ta[i], cnt, nsel)
        nsel_ref[h] = nsel
        e1_ref[h] = jnp.exp(s1 - ta[0]) / z
        rank2_ref[h] = rank2.astype(BF16)
        e2_ref[h] = jnp.exp(s2 - tb[0]).astype(BF16)
        return carry

    lax.fori_loop(0, PEER_HEADS, per_head, 0)


def _peer_topk(scores, tt=256):
    nb, nk, t = scores.shape
    tt = min(tt, t)
    big = pl.BlockSpec((PEER_HEADS, nk, tt), lambda i: (0, 0, i))
    f32 = jax.ShapeDtypeStruct((PEER_HEADS, nk, t), F32)
    b16 = jax.ShapeDtypeStruct((PEER_HEADS, nk, t), BF16)
    return pl.pallas_call(
        _peer_topk_kernel,
        grid=(t // tt,),
        in_specs=[pl.BlockSpec((nb, nk, tt), lambda i: (0, 0, i))],
        out_specs=[big, big, big, big],
        out_shape=[f32, f32, b16, b16],
        compiler_params=_cparams(("parallel",)),
    )(scores)


def _peer_dense_kernel(ht_ref, asc_ref, u0_ref, ub_ref, un_ref, v_ref, ns0_ref, e10_ref, nsc_ref, e1c_ref,
                       nsn_ref, e1n_ref, rank2_ref, e2_ref, h1_ref, lg_ref, lb_ref,
                       o_ref, hsa_ref, hsb_ref, *, q, nj):
    j = pl.program_id(1)
    nk = PEER_NKEYS
    tm = ht_ref.shape[1]
    half = q * nk
    ht = ht_ref[...]

    def gate_part(gate, heads, ns_ref, e1_ref, row, live):
        for h in heads:
            nsel = ns_ref[h, 0, row:row + 1, :]
            nsel = nsel if live is None else nsel * live
            nsel = jnp.concatenate([jnp.broadcast_to(nsel, (16, tm)).astype(BF16)] * (nk // 16), axis=0)
            e1 = jnp.concatenate(
                [jnp.broadcast_to(e1_ref[h, 0, row:row + 1, :], (16, tm)).astype(BF16)] * (nk // 16), axis=0)
            gate = gate + e1 * jnp.where(rank2_ref[h] < nsel, e2_ref[h], 0.0)
        return gate

    def hidden_block(u_ref, ns_ref, e1_ref, row, qi, live, dst_ref, values=None):
        gate = gate_part(jnp.zeros((nk, tm), BF16), range(0, PEER_HEADS // 2), ns_ref, e1_ref, row, live)
        if values is not None:
            values()
        gate = gate_part(gate, range(PEER_HEADS // 2, PEER_HEADS), ns_ref, e1_ref, row, live)
        act = jnp.dot(u_ref[qi * nk:(qi + 1) * nk, :], ht, preferred_element_type=F32) * asc_ref[...]
        c0 = math.sqrt(2.0 / math.pi)
        inner = act * (c0 + (c0 * 0.044715) * (act * act))
        half_act = 0.5 * act.astype(BF16)
        hid = (half_act + half_act * jnp.tanh(inner.astype(BF16))) * gate
        dst_ref[:, qi * nk:(qi + 1) * nk] = hid.T

    @pl.when(j == 0)
    def _():
        o_ref[...] = jnp.zeros_like(o_ref)
        for qi in range(q):
            hidden_block(u0_ref, ns0_ref, e10_ref, qi, qi, None, hsa_ref)

    def phase(src_ref, v_rows, u_ref, ns_ref, e1_ref, row0, live, dst_ref):
        wv = D_MODEL // q
        for qi in range(q):
            cols = slice(qi * wv, (qi + 1) * wv)

            def values():
                o_ref[:, cols] += jnp.dot(src_ref[...], v_ref[v_rows, cols], preferred_element_type=F32)

            hidden_block(u_ref, ns_ref, e1_ref, row0 + qi, qi, live, dst_ref, values)

    phase(hsa_ref, slice(0, half), ub_ref, nsc_ref, e1c_ref, q, None, hsb_ref)
    more = jnp.where(j + 1 < nj, 1.0, 0.0)
    phase(hsb_ref, slice(half, 2 * half), un_ref, nsn_ref, e1n_ref, 0, more, hsa_ref)

    @pl.when(j == nj - 1)
    def _():
        o_ref[...] = _layer_norm(DEEPNORM_ALPHA * h1_ref[...] + o_ref[...], lg_ref[...], lb_ref[...])


def _peer_dense(ht, act_scale, u_tab, v_tab, nsel, e1, rank2, e2, h1, ln_g, ln_b, tm=512, te=1024):
    t = ht.shape[1]
    tm = min(tm, t)
    half = te // 2
    q = half // PEER_NKEYS
    ng = PEER_NKEYS // (2 * q)
    nj = PEER_EXPERTS // te
    rows = lambda a: a.reshape(PEER_HEADS, ng, 2 * q, t)
    nxt = lambda j: jnp.minimum(j + 1, nj - 1)
    u_spec = lambda f: pl.BlockSpec((half, D_MODEL), lambda i, j: (f(j), 0))
    row_spec = lambda f: pl.BlockSpec((PEER_HEADS, 1, 2 * q, tm), lambda i, j: (0, f(j), 0, i))
    key_spec = pl.BlockSpec((PEER_HEADS, PEER_NKEYS, tm), lambda i, j: (0, 0, i))
    vec_spec = pl.BlockSpec((1, D_MODEL), lambda i, j: (0, 0))
    first, cur = (lambda j: 0), (lambda j: j)
    return pl.pallas_call(
        functools.partial(_peer_dense_kernel, q=q, nj=nj),
        grid=(t // tm, nj),
        in_specs=[pl.BlockSpec((D_MODEL, tm), lambda i, j: (0, i)),
                  pl.BlockSpec((1, tm), lambda i, j: (0, i)),
                  u_spec(first), u_spec(lambda j: 2 * j + 1), u_spec(lambda j: 2 * nxt(j)),
                  pl.BlockSpec((te, D_MODEL), lambda i, j: (j, 0)),
                  row_spec(first), row_spec(first), row_spec(cur), row_spec(cur),
                  row_spec(nxt), row_spec(nxt), key_spec, key_spec,
                  pl.BlockSpec((tm, D_MODEL), lambda i, j: (i, 0)), vec_spec, vec_spec],
        out_specs=pl.BlockSpec((tm, D_MODEL), lambda i, j: (i, 0)),
        out_shape=jax.ShapeDtypeStruct((t, D_MODEL), F32),
        scratch_shapes=[pltpu.VMEM((tm, half), BF16), pltpu.VMEM((tm, half), BF16)],
        compiler_params=_cparams(("parallel", "arbitrary")),
    )(ht, act_scale, u_tab, u_tab, u_tab, v_tab, rows(nsel), rows(e1), rows(nsel), rows(e1),
      rows(nsel), rows(e1), rank2, e2, h1, ln_g, ln_b)


def _layer(h, bsz, seq, w_in, s5, rw, w_out, ln1, peer, ln2):
    (lam_re, lam_im, log_step, b_re, b_im, c_re, c_im, d_skip, w_glu, b_glu) = s5
    (mu, w0, w_up, a0, a_up, g_up, k_k, k_a, r_k, gn_g, gn_b) = rw
    (w_q, keys1, keys2, u_tab, v_tab) = peer
    hb = h.astype(BF16)
    n_rkv = 3 * D_RWKV
    w_in_b = w_in.astype(BF16)
    pad = RW_LORA_PAD - RW_LORA

    u = _matmul(hb, w_in_b, F32, 1024, 1024, 0, D_SSM)
    p_rkv = _matmul(hb, w_in_b, BF16, 1024, 1024, D_SSM, n_rkv)
    p_lora = _matmul(hb, jnp.pad(w_in_b[:, D_SSM + n_rkv:], ((0, 0), (0, pad))), F32, 512, RW_LORA_PAD)

    yg = _s5_scan(u, _s5_params(lam_re, lam_im, log_step, b_re, b_im, c_re, c_im, d_skip,
                                seq // S5_CHUNK), bsz, seq)
    y_ssm = _glu(yg, w_glu.astype(BF16), b_glu[None, :])

    mu_rkv = mu[None, :n_rkv]
    mu_lora = jnp.pad(mu[n_rkv:], (0, pad))[None, :]
    zrow = lambda n: jnp.zeros((n, D_RWKV), F32)
    wup = jnp.concatenate([w_up, zrow(RW_LORA_PAD - RW_W_LORA)], axis=0).astype(BF16)
    aup = jnp.concatenate([zrow(RW_W_LORA), a_up, zrow(RW_LORA_PAD - RW_W_LORA - RW_A_LORA)],
                          axis=0).astype(BF16)
    gup = jnp.concatenate([zrow(RW_W_LORA + RW_A_LORA), g_up, zrow(pad)], axis=0).astype(BF16)
    r, k, v, kkr, a, lw, g = _rw_prep(p_rkv, p_lora, mu_rkv, mu_lora, w0[None], a0[None],
                                      k_k[None], k_a[None], wup, aup, gup, seq)
    q, oi, bonus, m, n = _rw_chunks(r, k, v, kkr, a, lw, r_k.reshape(1, D_RWKV))
    y_rw = _rw_seq(q, oi, bonus, g, m, n, gn_g[None], gn_b[None], bsz, seq)

    h1, h1b, h1t, inv_sx = _outproj(y_ssm, y_rw, w_out.astype(BF16), h, ln1[0][None], ln1[1][None])

    keys = jnp.stack([keys1, keys2], axis=1).reshape(2 * PEER_HEADS, PEER_NKEYS, PEER_HALF)
    scores = _peer_scores(h1b, w_q.astype(BF16), keys.astype(BF16))
    nsel, e1, rank2, e2 = _peer_topk(scores)
    su = _pow2_scale(jnp.max(jnp.abs(u_tab)))
    return _peer_dense(h1t, inv_sx / su, (u_tab * su).astype(F8), v_tab.astype(BF16), nsel, e1, rank2, e2,
                       h1, ln2[0][None], ln2[1][None])


def kernel(x, w_in, s5_lam_re, s5_lam_im, s5_log_step, s5_b_re, s5_b_im, s5_c_re, s5_c_im, s5_d, s5_w_glu, s5_b_glu, rw_mu, rw_w0, rw_w_up, rw_a0, rw_a_up, rw_g_up, rw_k_k, rw_k_a, rw_r_k, rw_gn_g, rw_gn_b, w_out, ln1_g, ln1_b, peer_w_q, peer_keys1, peer_keys2, peer_u, peer_v, ln2_g, ln2_b):
    bsz, seq, dim = x.shape
    h = x.reshape(bsz * seq, dim)
    for l in range(w_in.shape[0]):
        h = _layer(
            h, bsz, seq, w_in[l],
            (s5_lam_re[l], s5_lam_im[l], s5_log_step[l], s5_b_re[l], s5_b_im[l], s5_c_re[l],
             s5_c_im[l], s5_d[l], s5_w_glu[l], s5_b_glu[l]),
            (rw_mu[l], rw_w0[l], rw_w_up[l], rw_a0[l], rw_a_up[l], rw_g_up[l], rw_k_k[l],
             rw_k_a[l], rw_r_k[l], rw_gn_g[l], rw_gn_b[l]),
            w_out[l], (ln1_g[l], ln1_b[l]),
            (peer_w_q[l], peer_keys1[l], peer_keys2[l], peer_u[l], peer_v[l]),
            (ln2_g[l], ln2_b[l]))
    return h.reshape(bsz, seq, dim)
```

```python
import functools
import math

import jax
import jax.numpy as jnp
from jax import lax
from jax.experimental import pallas as pl
from jax.experimental.pallas import tpu as pltpu

F32 = jnp.float32
BF16 = jnp.bfloat16
F8 = jnp.float8_e4m3fn
F8_TARGET = 224.0

D_MODEL = 2048
D_SSM = 1024
D_RWKV = 1024
S5_CH = 16
S5_GROUPS = D_SSM // S5_CH
S5_STATE = 64
S5_CHUNK = 16
RW_HEAD = 64
RW_HEADS = D_RWKV // RW_HEAD
RW_W_LORA = 64
RW_A_LORA = 64
RW_G_LORA = 160
RW_LORA = RW_W_LORA + RW_A_LORA + RW_G_LORA
RW_LORA_PAD = 384
RW_CHUNK = 64
PEER_HEADS = 8
PEER_NKEYS = 128
PEER_EXPERTS = PEER_NKEYS * PEER_NKEYS
PEER_HALF = 128
PEER_TOPK = 16
DEPTH = 1
DEEPNORM_ALPHA = (2.0 * DEPTH) ** 0.25
LN_EPS = 1e-5
RW_GN_EPS = 64e-5
NEG_INF = float("-inf")

VMEM_LIMIT = 56 * 1024 * 1024


def _cparams(sem):
    return pltpu.CompilerParams(dimension_semantics=sem, vmem_limit_bytes=VMEM_LIMIT)


def _dot(a, b):
    return jnp.dot(a.astype(BF16), b.astype(BF16), preferred_element_type=F32)


def _dot_nt(a, b):
    return lax.dot_general(a.astype(BF16), b.astype(BF16), (((1,), (1,)), ((), ())),
                           preferred_element_type=F32)


def _dot_tn(a, b):
    return lax.dot_general(a.astype(BF16), b.astype(BF16), (((0,), (0,)), ((), ())),
                           preferred_element_type=F32)


def _gelu(x):
    c = math.sqrt(2.0 / math.pi)
    return 0.5 * x * (1.0 + jnp.tanh(c * (x + 0.044715 * (x * x * x))))


def _sigmoid(x):
    return 1.0 / (1.0 + jnp.exp(-x))


def _head_sum(x, ones_ref):
    x_hi = x.astype(BF16)
    x_lo = (x - x_hi.astype(F32)).astype(BF16)
    w = ones_ref.shape[0]
    return jnp.concatenate(
        [jnp.dot(x_hi[:, i:i + w], ones_ref[...], preferred_element_type=F32)
         + jnp.dot(x_lo[:, i:i + w], ones_ref[...], preferred_element_type=F32)
         for i in range(0, x.shape[1], w)], axis=1)


def _pow2_scale(amax):
    return jnp.exp2(jnp.floor(jnp.log2(F8_TARGET / jnp.maximum(amax, 1e-30))))


def _head_ones():
    hid = jnp.arange(4 * RW_HEAD) // RW_HEAD
    return (hid[:, None] == hid[None, :]).astype(BF16)


def _mm_kernel(a_ref, b_ref, o_ref):
    o_ref[...] = jnp.dot(a_ref[...], b_ref[...],
                         preferred_element_type=F32).astype(o_ref.dtype)


def _matmul(a, b, out_dtype, tm, tn, col0=0, n=None):
    m, k = a.shape
    n = b.shape[1] if n is None else n
    tm = min(tm, m)
    assert col0 % tn == 0 and n % tn == 0
    return pl.pallas_call(
        _mm_kernel,
        grid=(m // tm, n // tn),
        in_specs=[pl.BlockSpec((tm, k), lambda i, j: (i, 0)),
                  pl.BlockSpec((k, tn), lambda i, j: (0, j + col0 // tn))],
        out_specs=pl.BlockSpec((tm, tn), lambda i, j: (i, j)),
        out_shape=jax.ShapeDtypeStruct((m, n), out_dtype),
        compiler_params=_cparams(("parallel", "parallel")),
    )(a, b)


def _s5_kernel(u_ref, mt_ref, bc_ref, cc_ref, ar_ref, ai_ref, d_ref, o_ref, *, nc):
    c = S5_CHUNK
    rows = u_ref.shape[0] // c
    groups = u_ref.shape[1] // S5_CH
    xs = [u_ref[pl.ds(tl, rows, stride=c), :] for tl in range(c)]
    cidx = lax.broadcasted_iota(jnp.int32, (rows, 2 * S5_STATE), 0) & (nc - 1)
    ys = []
    for g in range(groups):
        lanes = slice(g * S5_CH, (g + 1) * S5_CH)
        u = jnp.concatenate([x[:, lanes] for x in xs], axis=1).astype(BF16)
        y = jnp.dot(u, mt_ref[g], preferred_element_type=F32)
        s = jnp.dot(u, bc_ref[g], preferred_element_type=F32)
        d, k = 1, 0
        while d < nc:
            sh = jnp.where(cidx >= d, pltpu.roll(s, d, axis=0), 0.0)
            shs = pltpu.roll(sh, S5_STATE, axis=1)
            s = s + sh * ar_ref[g, k:k + 1, :] + shs * ai_ref[g, k:k + 1, :]
            d, k = d * 2, k + 1
        sp = jnp.where(cidx >= 1, pltpu.roll(s, 1, axis=0), 0.0)
        ys.append(y + jnp.dot(sp.astype(BF16), cc_ref[g], preferred_element_type=F32))
    for tl in range(c):
        y = jnp.concatenate([yg[:, tl * S5_CH:(tl + 1) * S5_CH] for yg in ys], axis=1)
        o_ref[pl.ds(tl, rows, stride=c), :] = _gelu(y + d_ref[...] * xs[tl])


def _s5_params(lam_re, lam_im, log_step, b_re, b_im, c_re, c_im, d_skip, nc):
    hi = lax.Precision.HIGHEST
    c = S5_CHUNK
    step = jnp.exp(log_step)[:, None]
    a_re = jnp.exp(lam_re * step) * jnp.cos(lam_im * step)
    a_im = jnp.exp(lam_re * step) * jnp.sin(lam_im * step)
    den = lam_re * lam_re + lam_im * lam_im
    f_re = ((a_re - 1.0) * lam_re + a_im * lam_im) / den
    f_im = (a_im * lam_re - (a_re - 1.0) * lam_im) / den
    bb_re = f_re[..., None] * b_re - f_im[..., None] * b_im
    bb_im = f_re[..., None] * b_im + f_im[..., None] * b_re

    def power(j):
        jj = j[None, :, None]
        mag = jnp.exp(lam_re[:, None, :] * step[:, None, :] * jj)
        ang = lam_im[:, None, :] * step[:, None, :] * jj
        return mag * jnp.cos(ang), mag * jnp.sin(ang)

    pw_re, pw_im = power(jnp.arange(c + 1, dtype=F32))
    ce_re = c_re[:, None] * pw_re[:, :, None, :] - c_im[:, None] * pw_im[:, :, None, :]
    ce_im = c_re[:, None] * pw_im[:, :, None, :] + c_im[:, None] * pw_re[:, :, None, :]
    kern = jnp.einsum('gjop,gpi->gjoi', jnp.concatenate([ce_re[:, :c], -ce_im[:, :c]], axis=-1),
                      jnp.concatenate([bb_re, bb_im], axis=1), precision=hi)
    t_idx = jnp.arange(c)
    lag = t_idx[None, :] - t_idx[:, None]
    place = (lag[:, :, None] == t_idx[None, None, :]).astype(F32)
    g = lam_re.shape[0]
    mt = jnp.einsum('stj,gjoi->gsito', place, kern, precision=hi)
    mt = mt.reshape(g, c * S5_CH, c * S5_CH)
    rv_re, rv_im = pw_re[:, c - 1::-1][:, :c], pw_im[:, c - 1::-1][:, :c]
    bc_r = rv_re[:, :, None, :] * jnp.transpose(bb_re, (0, 2, 1))[:, None] \
        - rv_im[:, :, None, :] * jnp.transpose(bb_im, (0, 2, 1))[:, None]
    bc_i = rv_re[:, :, None, :] * jnp.transpose(bb_im, (0, 2, 1))[:, None] \
        + rv_im[:, :, None, :] * jnp.transpose(bb_re, (0, 2, 1))[:, None]
    bc = jnp.concatenate([bc_r, bc_i], axis=-1).reshape(g, c * S5_CH, 2 * S5_STATE)
    cc_r = jnp.transpose(ce_re[:, 1:], (0, 3, 1, 2)).reshape(g, S5_STATE, c * S5_CH)
    cc_i = -jnp.transpose(ce_im[:, 1:], (0, 3, 1, 2)).reshape(g, S5_STATE, c * S5_CH)
    cc = jnp.concatenate([cc_r, cc_i], axis=1)
    sc_re, sc_im = power(c * (2.0 ** jnp.arange(8, dtype=F32)))
    ar = jnp.concatenate([sc_re, sc_re], axis=-1)
    ai = jnp.concatenate([-sc_im, sc_im], axis=-1)
    return mt.astype(BF16), bc.astype(BF16), cc.astype(BF16), ar, ai, d_skip[None, :]


def _s5_scan(u_tok, params, bsz, seq, gpb=8):
    mt, bc, cc, ar, ai, d_row = params
    nc = seq // S5_CHUNK
    t = bsz * seq
    w = S5_CHUNK * S5_CH
    lanes = gpb * S5_CH
    tok = pl.BlockSpec((t, lanes), lambda i: (0, i))
    spec3 = lambda a, b: pl.BlockSpec((gpb, a, b), lambda i: (i, 0, 0))
    return pl.pallas_call(
        functools.partial(_s5_kernel, nc=nc),
        grid=(S5_GROUPS // gpb,),
        in_specs=[tok, spec3(w, w), spec3(w, 2 * S5_STATE), spec3(2 * S5_STATE, w),
                  spec3(8, 2 * S5_STATE), spec3(8, 2 * S5_STATE),
                  pl.BlockSpec((1, lanes), lambda i: (0, i))],
        out_specs=tok,
        out_shape=jax.ShapeDtypeStruct((t, D_SSM), F32),
        compiler_params=_cparams(("parallel",)),
    )(u_tok, mt, bc, cc, ar, ai, d_row)


def _glu_kernel(y_ref, w_ref, b_ref, o_ref):
    y = y_ref[...]
    z = jnp.dot(y.astype(BF16), w_ref[...], preferred_element_type=F32) + b_ref[...]
    o_ref[...] = (y * _sigmoid(z)).astype(o_ref.dtype)


def _glu(y, w, b, tm=512):
    m, n = y.shape
    tm = min(tm, m)
    return pl.pallas_call(
        _glu_kernel,
        grid=(m // tm,),
        in_specs=[pl.BlockSpec((tm, n), lambda i: (i, 0)),
                  pl.BlockSpec((n, n), lambda i: (0, 0)),
                  pl.BlockSpec((1, n), lambda i: (0, 0))],
        out_specs=pl.BlockSpec((tm, n), lambda i: (i, 0)),
        out_shape=jax.ShapeDtypeStruct((m, n), BF16),
        compiler_params=_cparams(("parallel",)),
    )(y, w, b)


def _rw_prep_kernel(p_ref, pp_ref, l_ref, lp_ref, mu_ref, mul_ref, w0_ref, a0_ref,
                    kk_ref, ka_ref, wup_ref, aup_ref, gup_ref,
                    r_ref, k_ref, v_ref, kkr_ref, a_ref, lw_ref, g_ref, *, tiles_per_seq):
    first = (pl.program_id(0) % tiles_per_seq) == 0

    def shifted(cur_ref, prev_ref, mu):
        cur = cur_ref[...].astype(F32)
        prev_row = jnp.where(first, 0.0, prev_ref[15:16, :].astype(F32))
        row = lax.broadcasted_iota(jnp.int32, cur.shape, 0)
        prev = jnp.where(row == 0, prev_row, pltpu.roll(cur, 1, axis=0))
        return cur + mu * (prev - cur)

    p = shifted(p_ref, pp_ref, mu_ref[...])
    lo = shifted(l_ref, lp_ref, mul_ref[...])
    r = p[:, :D_RWKV]
    k = p[:, D_RWKV:2 * D_RWKV]
    v = p[:, 2 * D_RWKV:]
    w_pre = w0_ref[...] + _dot(jnp.tanh(lo), wup_ref[...])
    a = _sigmoid(a0_ref[...] + _dot(lo, aup_ref[...]))
    g = _dot(_sigmoid(lo), gup_ref[...])
    z = -w_pre
    softplus = jnp.maximum(z, 0.0) + jnp.log(1.0 + jnp.exp(-jnp.abs(z)))
    w = -softplus - 0.5
    r_ref[...] = r.astype(r_ref.dtype)
    k_ref[...] = (k * (1.0 + (a - 1.0) * ka_ref[...])).astype(k_ref.dtype)
    v_ref[...] = v.astype(v_ref.dtype)
    kkr_ref[...] = (k * kk_ref[...]).astype(kkr_ref.dtype)
    a_ref[...] = a.astype(a_ref.dtype)
    lw_ref[...] = -jnp.exp(w)
    g_ref[...] = g.astype(g_ref.dtype)


def _rw_prep(p_rkv, p_lora, mu_rkv, mu_lora, w0, a0, k_k, k_a, wup, aup, gup, seq, tm=256):
    t = p_rkv.shape[0]
    tm = min(tm, seq)
    n3 = 3 * D_RWKV
    row = lambda n: pl.BlockSpec((1, n), lambda i: (0, 0))
    full = lambda a, b: pl.BlockSpec((a, b), lambda i: (0, 0))
    prev = lambda n: pl.BlockSpec((16, n), lambda i: (jnp.maximum(i * (tm // 16) - 1, 0), 0))
    out = lambda dt: jax.ShapeDtypeStruct((t, D_RWKV), dt)
    ospec = pl.BlockSpec((tm, D_RWKV), lambda i: (i, 0))
    return pl.pallas_call(
        functools.partial(_rw_prep_kernel, tiles_per_seq=seq // tm),
        grid=(t // tm,),
        in_specs=[pl.BlockSpec((tm, n3), lambda i: (i, 0)), prev(n3),
                  pl.BlockSpec((tm, RW_LORA_PAD), lambda i: (i, 0)), prev(RW_LORA_PAD),
                  row(n3), row(RW_LORA_PAD), row(D_RWKV), row(D_RWKV), row(D_RWKV), row(D_RWKV),
                  full(RW_LORA_PAD, D_RWKV), full(RW_LORA_PAD, D_RWKV), full(RW_LORA_PAD, D_RWKV)],
        out_specs=[ospec] * 7,
        out_shape=[out(BF16)] * 5 + [out(F32), out(BF16)],
        compiler_params=_cparams(("parallel",)),
    )(p_rkv, p_rkv, p_lora, p_lora, mu_rkv, mu_lora, w0, a0, k_k, k_a, wup, aup, gup)


def _rw_chunk_kernel(r_ref, k_ref, v_ref, kkr_ref, a_ref, lw_ref, rk_ref, ones_ref,
                     q_ref, oi_ref, bonus_ref, m_ref, n_ref):
    c = RW_CHUNK
    hd = RW_HEAD
    heads = range(r_ref.shape[1] // hd)
    ri = lax.broadcasted_iota(jnp.int32, (c, c), 0)
    ci = lax.broadcasted_iota(jnp.int32, (c, c), 1)
    tri_incl = (ci <= ri)
    tri_strict = (ci < ri)
    ltri = tri_incl.astype(BF16)
    eye = (ci == ri).astype(F32)
    lw = lw_ref[...]
    lw_hi = lw.astype(BF16)
    lw_lo = (lw - lw_hi.astype(F32)).astype(BF16)
    cl = (jnp.dot(ltri, lw_hi, preferred_element_type=F32)
          + jnp.dot(ltri, lw_lo, preferred_element_type=F32))
    cl_end = cl[c - 1:c, :]
    r = r_ref[...].astype(F32)
    k = k_ref[...].astype(F32)
    rt_d = r * jnp.exp(cl)
    e_neg_d = jnp.exp(-cl)
    kt_d = k * e_neg_d
    e_prev_d = jnp.exp(cl - lw)
    e_end_d = jnp.exp(cl_end - cl)
    kte_d = (k * e_end_d).astype(BF16)
    gam_d = jnp.exp(cl_end)
    kkr_d = kkr_ref[...].astype(F32)

    nrm = jnp.sqrt(_head_sum(kkr_d * kkr_d, ones_ref))
    kk_d = kkr_d / jnp.maximum(nrm, 1e-12)
    b_d = kk_d * a_ref[...].astype(F32)
    bte_d = (b_d * e_end_d).astype(BF16)
    kr_d = jnp.concatenate([kk_d * e_prev_d, rt_d], axis=0).astype(BF16)
    kb_d = jnp.concatenate([kt_d, b_d * e_neg_d], axis=0).astype(BF16)
    v_d = v_ref[...]
    bonus_ref[...] = _head_sum(r * k * rk_ref[0:1, :], ones_ref) * v_d.astype(F32)
    sl = [slice(h * hd, (h + 1) * hd) for h in heads]
    v = [v_d[:, s] for s in sl]
    rt = [rt_d[:, s] for s in sl]
    kkt = [kr_d[:c, s] for s in sl]
    bte = [bte_d[:, s] for s in sl]
    a4 = [_dot_nt(kr_d[:, s], kb_d[:, s]) for s in sl]
    a_kb = [jnp.where(tri_strict, a4[h][:c, c:], 0.0) for h in heads]
    a_rb = [jnp.where(tri_incl, a4[h][c:, c:], 0.0) for h in heads]
    a_kr = [jnp.concatenate([jnp.where(tri_strict, a4[h][:c, :c], 0.0),
                             jnp.where(tri_incl, a4[h][c:, :c], 0.0)], axis=0) for h in heads]
    x = [eye - a_kb[h] for h in heads]
    pw = [_dot(a_kb[h], a_kb[h]) for h in heads]
    n_sq = int(math.log2(c)) - 1
    for it in range(n_sq):
        if it + 1 < n_sq:
            xp = [_dot(jnp.concatenate([x[h], pw[h]], axis=0), pw[h]) for h in heads]
            x = [x[h] + xp[h][:c] for h in heads]
            pw = [xp[h][c:] for h in heads]
        else:
            x = [x[h] + _dot(x[h], pw[h]) for h in heads]
    av = [_dot(a_kr[h], v[h]) for h in heads]
    wu = [_dot(x[h], jnp.concatenate([kkt[h], av[h][:c]], axis=1)) for h in heads]
    rb = [_dot(a_rb[h], wu[h]) for h in heads]
    wub = [_dot_tn(wu[h], bte[h]) for h in heads]
    vk = [_dot_tn(v[h], kte_d[:, sl[h]]) for h in heads]
    ji = lax.broadcasted_iota(jnp.int32, (hd, hd), 0)
    jo = lax.broadcasted_iota(jnp.int32, (hd, hd), 1)
    for h in heads:
        m_ref[0, h] = jnp.where(ji == jo, gam_d[:, sl[h]], 0.0) - wub[h][:hd]
        n_ref[0, h] = vk[h] - wub[h][hd:]
    q_ref[...] = jnp.concatenate([rt[h] - rb[h][:, :hd] for h in heads], axis=1)
    oi_ref[...] = jnp.concatenate([av[h][c:] - rb[h][:, hd:] for h in heads], axis=1)


def _rw_chunks(r, k, v, kkr, a, lw, r_k, hp=RW_HEADS):
    t = r.shape[0]
    c = RW_CHUNK
    nch = t // c
    wdt = hp * RW_HEAD
    blk = pl.BlockSpec((c, wdt), lambda i, j: (i, j))
    mat = pl.BlockSpec((1, hp, RW_HEAD, RW_HEAD), lambda i, j: (i, j, 0, 0))
    tok = jax.ShapeDtypeStruct((t, D_RWKV), F32)
    mshape = jax.ShapeDtypeStruct((nch, RW_HEADS, RW_HEAD, RW_HEAD), F32)
    return pl.pallas_call(
        _rw_chunk_kernel,
        grid=(nch, RW_HEADS // hp),
        in_specs=[blk] * 6 + [pl.BlockSpec((1, wdt), lambda i, j: (0, j)),
                  pl.BlockSpec((4 * RW_HEAD, 4 * RW_HEAD), lambda i, j: (0, 0))],
        out_specs=[blk, blk, blk, mat, mat],
        out_shape=[tok, tok, tok, mshape, mshape],
        compiler_params=_cparams(("parallel", "parallel")),
    )(r, k, v, kkr, a, lw, r_k, _head_ones())


def _rw_seq_kernel(q_ref, oi_ref, bonus_ref, g_ref, m_ref, n_ref, gg_ref, gb_ref, ones_ref,
                   o_ref, st_ref):
    @pl.when(pl.program_id(0) == 0)
    def _():
        st_ref[...] = jnp.zeros_like(st_ref)

    hd = RW_HEAD
    sl = [slice(h * hd, (h + 1) * hd) for h in range(RW_HEADS)]
    streams = [(b, h) for b in range(q_ref.shape[0]) for h in range(RW_HEADS)]
    s = {bh: st_ref[bh[0], bh[1]] for bh in streams}
    q = [q_ref[b] for b in range(q_ref.shape[0])]
    o = {(b, h): _dot_nt(q[b][:, sl[h]], s[b, h]) for b, h in streams}
    for b, h in streams:
        st_ref[b, h] = _dot(s[b, h], m_ref[b, 0, h]) + n_ref[b, 0, h]
    inv_n = 1.0 / hd
    for b in range(q_ref.shape[0]):
        ob = jnp.concatenate([o[b, h] for h in range(RW_HEADS)], axis=1) + oi_ref[b]
        ctr = ob - _head_sum(ob, ones_ref) * inv_n
        var = _head_sum(ctr * ctr, ones_ref) * inv_n
        y = ctr * lax.rsqrt(var + RW_GN_EPS) * gg_ref[...] + gb_ref[...] + bonus_ref[b]
        o_ref[b] = (y * g_ref[b].astype(F32)).astype(o_ref.dtype)


def _rw_seq(q, oi, bonus, g, m, n, gn_g, gn_b, bsz, seq):
    c = RW_CHUNK
    nch = seq // c
    r3 = lambda x: x.reshape(bsz, seq, D_RWKV)
    r5 = lambda x: x.reshape(bsz, nch, RW_HEADS, RW_HEAD, RW_HEAD)
    tok = pl.BlockSpec((bsz, c, D_RWKV), lambda i: (0, i, 0))
    mat = pl.BlockSpec((bsz, 1, RW_HEADS, RW_HEAD, RW_HEAD), lambda i: (0, i, 0, 0, 0))
    row = pl.BlockSpec((1, D_RWKV), lambda i: (0, 0))
    y = pl.pallas_call(
        _rw_seq_kernel,
        grid=(nch,),
        in_specs=[tok, tok, tok, tok, mat, mat, row, row,
                  pl.BlockSpec((4 * RW_HEAD, 4 * RW_HEAD), lambda i: (0, 0))],
        out_specs=tok,
        out_shape=jax.ShapeDtypeStruct((bsz, seq, D_RWKV), BF16),
        scratch_shapes=[pltpu.VMEM((bsz, RW_HEADS, RW_HEAD, RW_HEAD), F32)],
        compiler_params=_cparams(("arbitrary",)),
    )(r3(q), r3(oi), r3(bonus), r3(g), r5(m), r5(n), gn_g, gn_b, _head_ones())
    return y.reshape(bsz * seq, D_RWKV)


def _layer_norm(x, g, b):
    mu = jnp.mean(x, axis=-1, keepdims=True)
    var = jnp.mean(jnp.square(x - mu), axis=-1, keepdims=True)
    return (x - mu) * lax.rsqrt(var + LN_EPS) * g + b


def _outproj_kernel(ys_ref, yr_ref, w1_ref, w2_ref, x_ref, g_ref, b_ref, h_ref, hb_ref, ht_ref, isc_ref):
    mix = (jnp.dot(ys_ref[...], w1_ref[...], preferred_element_type=F32)
           + jnp.dot(yr_ref[...], w2_ref[...], preferred_element_type=F32))
    h = _layer_norm(DEEPNORM_ALPHA * x_ref[...] + mix, g_ref[...], b_ref[...])
    h_ref[...] = h
    hb_ref[...] = h.astype(BF16)
    ht = h.T
    sx = _pow2_scale(jnp.max(jnp.abs(ht), axis=0, keepdims=True))
    ht_ref[...] = (ht * sx).astype(F8)
    isc_ref[...] = 1.0 / sx


def _outproj(ys, yr, w_out, x, g, b, tm=512):
    t = x.shape[0]
    tm = min(tm, t)
    half = pl.BlockSpec((tm, D_SSM), lambda i: (i, 0))
    full = pl.BlockSpec((tm, D_MODEL), lambda i: (i, 0))
    row = pl.BlockSpec((1, D_MODEL), lambda i: (0, 0))
    return pl.pallas_call(
        _outproj_kernel,
        grid=(t // tm,),
        in_specs=[half, half,
                  pl.BlockSpec((D_SSM, D_MODEL), lambda i: (0, 0)),
                  pl.BlockSpec((D_RWKV, D_MODEL), lambda i: (1, 0)),
                  full, row, row],
        out_specs=[full, full, pl.BlockSpec((D_MODEL, tm), lambda i: (0, i)),
                   pl.BlockSpec((1, tm), lambda i: (0, i))],
        out_shape=[jax.ShapeDtypeStruct((t, D_MODEL), F32),
                   jax.ShapeDtypeStruct((t, D_MODEL), BF16),
                   jax.ShapeDtypeStruct((D_MODEL, t), F8),
                   jax.ShapeDtypeStruct((1, t), F32)],
        compiler_params=_cparams(("parallel",)),
    )(ys, yr, w_out, w_out, x, g, b)


NO_RANK = 127.0


def _extract_top(x, n, want_rank=False):
    vals = []
    rank = jnp.full(x.shape, NO_RANK, F32) if want_rank else None
    for i in range(n):
        m = jnp.max(x, axis=0, keepdims=True)
        vals.append(m)
        hit = x == m
        if want_rank:
            rank = jnp.where(hit, float(i), rank)
        x = jnp.where(hit, NEG_INF, x)
    return vals, rank


def _peer_select_kernel(h_ref, wq_ref, keys_ref, nsel_ref, e1_ref, rank2_ref, e2_ref):
    k = PEER_TOPK
    hb = h_ref[...]
    for h in range(PEER_HEADS):
        q = jnp.dot(hb, wq_ref[h], preferred_element_type=F32)
        s1 = _dot_nt(keys_ref[2 * h], q[:, :PEER_HALF])
        s2 = _dot_nt(keys_ref[2 * h + 1], q[:, PEER_HALF:])
        ta, _ = _extract_top(s1, k)
        tb, rank2 = _extract_top(s2, k, want_rank=True)
        tbs = jnp.concatenate(tb, axis=0)
        cands = [ta[i] + tbs[0:k // (i + 1), :] for i in range(k)]
        n_c = sum(k // (i + 1) for i in range(k))
        pad = (-n_c) % 8
        padded = cands + ([jnp.full((pad, s1.shape[1]), NEG_INF, F32)] if pad else [])
        best, _ = _extract_top(jnp.concatenate(padded, axis=0), k)
        m0 = best[0]
        tau = best[k - 1]
        z = jnp.zeros_like(m0)
        for bv in best:
            z = z + jnp.exp(bv - m0)
        nsel = jnp.zeros_like(s1)
        for i in range(k):
            cnt = jnp.sum(jnp.where(cands[i] >= tau, 1.0, 0.0), axis=0, keepdims=True)
            nsel = jnp.where(s1 == ta[i], cnt, nsel)
        nsel_ref[h] = nsel
        e1_ref[h] = jnp.exp(s1 - ta[0]) / z
        rank2_ref[h] = rank2.astype(BF16)
        e2_ref[h] = jnp.exp(s2 - tb[0]).astype(BF16)


def _peer_select(hb, wq, keys, tt=256):
    t = hb.shape[0]
    tt = min(tt, t)
    nk = PEER_NKEYS
    wq_h = jnp.transpose(wq.reshape(D_MODEL, PEER_HEADS, 2 * PEER_HALF), (1, 0, 2))
    big = pl.BlockSpec((PEER_HEADS, nk, tt), lambda i: (0, 0, i))
    f32 = jax.ShapeDtypeStruct((PEER_HEADS, nk, t), F32)
    b16 = jax.ShapeDtypeStruct((PEER_HEADS, nk, t), BF16)
    return pl.pallas_call(
        _peer_select_kernel,
        grid=(t // tt,),
        in_specs=[pl.BlockSpec((tt, D_MODEL), lambda i: (i, 0)),
                  pl.BlockSpec((PEER_HEADS, D_MODEL, 2 * PEER_HALF), lambda i: (0, 0, 0)),
                  pl.BlockSpec((2 * PEER_HEADS, nk, PEER_HALF), lambda i: (0, 0, 0))],
        out_specs=[big, big, big, big],
        out_shape=[f32, f32, b16, b16],
        compiler_params=_cparams(("parallel",)),
    )(hb, wq_h, keys)


def _peer_dense_kernel(ht_ref, asc_ref, u0_ref, ub_ref, un_ref, v_ref, ns0_ref, e10_ref, nsc_ref, e1c_ref,
                       nsn_ref, e1n_ref, rank2_ref, e2_ref, h1_ref, lg_ref, lb_ref,
                       o_ref, hsa_ref, hsb_ref, *, q, nj):
    j = pl.program_id(1)
    nk = PEER_NKEYS
    tm = ht_ref.shape[1]
    half = q * nk
    ht = ht_ref[...]

    def gate_part(gate, heads, ns_ref, e1_ref, row, live):
        for h in heads:
            nsel = ns_ref[h, 0, row:row + 1, :]
            nsel = nsel if live is None else nsel * live
            nsel = jnp.concatenate([jnp.broadcast_to(nsel, (16, tm)).astype(BF16)] * (nk // 16), axis=0)
            e1 = jnp.concatenate(
                [jnp.broadcast_to(e1_ref[h, 0, row:row + 1, :], (16, tm)).astype(BF16)] * (nk // 16), axis=0)
            gate = gate + e1 * jnp.where(rank2_ref[h] < nsel, e2_ref[h], 0.0)
        return gate

    def hidden_block(u_ref, ns_ref, e1_ref, row, qi, live, dst_ref, values=None):
        gate = gate_part(jnp.zeros((nk, tm), BF16), range(0, PEER_HEADS // 2), ns_ref, e1_ref, row, live)
        if values is not None:
            values()
        gate = gate_part(gate, range(PEER_HEADS // 2, PEER_HEADS), ns_ref, e1_ref, row, live)
        act = jnp.dot(u_ref[qi * nk:(qi + 1) * nk, :], ht, preferred_element_type=F32) * asc_ref[...]
        c0 = math.sqrt(2.0 / math.pi)
        inner = act * (c0 + (c0 * 0.044715) * (act * act))
        half_act = 0.5 * act.astype(BF16)
        hid = (half_act + half_act * jnp.tanh(inner.astype(BF16))) * gate
        dst_ref[:, qi * nk:(qi + 1) * nk] = hid.T

    @pl.when(j == 0)
    def _():
        o_ref[...] = jnp.zeros_like(o_ref)
        for qi in range(q):
            hidden_block(u0_ref, ns0_ref, e10_ref, qi, qi, None, hsa_ref)

    def phase(src_ref, v_rows, u_ref, ns_ref, e1_ref, row0, live, dst_ref):
        wv = D_MODEL // q
        for qi in range(q):
            cols = slice(qi * wv, (qi + 1) * wv)

            def values():
                o_ref[:, cols] += jnp.dot(src_ref[...], v_ref[v_rows, cols], preferred_element_type=F32)

            hidden_block(u_ref, ns_ref, e1_ref, row0 + qi, qi, live, dst_ref, values)

    phase(hsa_ref, slice(0, half), ub_ref, nsc_ref, e1c_ref, q, None, hsb_ref)
    more = jnp.where(j + 1 < nj, 1.0, 0.0)
    phase(hsb_ref, slice(half, 2 * half), un_ref, nsn_ref, e1n_ref, 0, more, hsa_ref)

    @pl.when(j == nj - 1)
    def _():
        o_ref[...] = _layer_norm(DEEPNORM_ALPHA * h1_ref[...] + o_ref[...], lg_ref[...], lb_ref[...])


def _peer_dense(ht, act_scale, u_tab, v_tab, nsel, e1, rank2, e2, h1, ln_g, ln_b, tm=512, te=1024):
    t = ht.shape[1]
    tm = min(tm, t)
    half = te // 2
    q = half // PEER_NKEYS
    ng = PEER_NKEYS // (2 * q)
    nj = PEER_EXPERTS // te
    rows = lambda a: a.reshape(PEER_HEADS, ng, 2 * q, t)
    nxt = lambda j: jnp.minimum(j + 1, nj - 1)
    u_spec = lambda f: pl.BlockSpec((half, D_MODEL), lambda i, j: (f(j), 0))
    row_spec = lambda f: pl.BlockSpec((PEER_HEADS, 1, 2 * q, tm), lambda i, j: (0, f(j), 0, i))
    key_spec = pl.BlockSpec((PEER_HEADS, PEER_NKEYS, tm), lambda i, j: (0, 0, i))
    vec_spec = pl.BlockSpec((1, D_MODEL), lambda i, j: (0, 0))
    first, cur = (lambda j: 0), (lambda j: j)
    return pl.pallas_call(
        functools.partial(_peer_dense_kernel, q=q, nj=nj),
        grid=(t // tm, nj),
        in_specs=[pl.BlockSpec((D_MODEL, tm), lambda i, j: (0, i)),
                  pl.BlockSpec((1, tm), lambda i, j: (0, i)),
                  u_spec(first), u_spec(lambda j: 2 * j + 1), u_spec(lambda j: 2 * nxt(j)),
                  pl.BlockSpec((te, D_MODEL), lambda i, j: (j, 0)),
                  row_spec(first), row_spec(first), row_spec(cur), row_spec(cur),
                  row_spec(nxt), row_spec(nxt), key_spec, key_spec,
                  pl.BlockSpec((tm, D_MODEL), lambda i, j: (i, 0)), vec_spec, vec_spec],
        out_specs=pl.BlockSpec((tm, D_MODEL), lambda i, j: (i, 0)),
        out_shape=jax.ShapeDtypeStruct((t, D_MODEL), F32),
        scratch_shapes=[pltpu.VMEM((tm, half), BF16), pltpu.VMEM((tm, half), BF16)],
        compiler_params=_cparams(("parallel", "arbitrary")),
    )(ht, act_scale, u_tab, u_tab, u_tab, v_tab, rows(nsel), rows(e1), rows(nsel), rows(e1),
      rows(nsel), rows(e1), rank2, e2, h1, ln_g, ln_b)


def _layer(h, bsz, seq, w_in, s5, rw, w_out, ln1, peer, ln2):
    (lam_re, lam_im, log_step, b_re, b_im, c_re, c_im, d_skip, w_glu, b_glu) = s5
    (mu, w0, w_up, a0, a_up, g_up, k_k, k_a, r_k, gn_g, gn_b) = rw
    (w_q, keys1, keys2, u_tab, v_tab) = peer
    hb = h.astype(BF16)
    n_rkv = 3 * D_RWKV
    w_in_b = w_in.astype(BF16)
    pad = RW_LORA_PAD - RW_LORA

    u = _matmul(hb, w_in_b, F32, 1024, 1024, 0, D_SSM)
    p_rkv = _matmul(hb, w_in_b, BF16, 1024, 1024, D_SSM, n_rkv)
    p_lora = _matmul(hb, jnp.pad(w_in_b[:, D_SSM + n_rkv:], ((0, 0), (0, pad))), F32, 512, RW_LORA_PAD)

    yg = _s5_scan(u, _s5_params(lam_re, lam_im, log_step, b_re, b_im, c_re, c_im, d_skip,
                                seq // S5_CHUNK), bsz, seq)
    y_ssm = _glu(yg, w_glu.astype(BF16), b_glu[None, :])

    mu_rkv = mu[None, :n_rkv]
    mu_lora = jnp.pad(mu[n_rkv:], (0, pad))[None, :]
    zrow = lambda n: jnp.zeros((n, D_RWKV), F32)
    wup = jnp.concatenate([w_up, zrow(RW_LORA_PAD - RW_W_LORA)], axis=0).astype(BF16)
    aup = jnp.concatenate([zrow(RW_W_LORA), a_up, zrow(RW_LORA_PAD - RW_W_LORA - RW_A_LORA)],
                          axis=0).astype(BF16)
    gup = jnp.concatenate([zrow(RW_W_LORA + RW_A_LORA), g_up, zrow(pad)], axis=0).astype(BF16)
    r, k, v, kkr, a, lw, g = _rw_prep(p_rkv, p_lora, mu_rkv, mu_lora, w0[None], a0[None],
                                      k_k[None], k_a[None], wup, aup, gup, seq)
    q, oi, bonus, m, n = _rw_chunks(r, k, v, kkr, a, lw, r_k.reshape(1, D_RWKV))
    y_rw = _rw_seq(q, oi, bonus, g, m, n, gn_g[None], gn_b[None], bsz, seq)

    h1, h1b, h1t, inv_sx = _outproj(y_ssm, y_rw, w_out.astype(BF16), h, ln1[0][None], ln1[1][None])

    keys = jnp.stack([keys1, keys2], axis=1).reshape(2 * PEER_HEADS, PEER_NKEYS, PEER_HALF)
    nsel, e1, rank2, e2 = _peer_select(h1b, w_q.astype(BF16), keys.astype(BF16))
    su = _pow2_scale(jnp.max(jnp.abs(u_tab)))
    return _peer_dense(h1t, inv_sx / su, (u_tab * su).astype(F8), v_tab.astype(BF16), nsel, e1, rank2, e2,
                       h1, ln2[0][None], ln2[1][None])


def kernel(x, w_in, s5_lam_re, s5_lam_im, s5_log_step, s5_b_re, s5_b_im, s5_c_re, s5_c_im, s5_d, s5_w_glu, s5_b_glu, rw_mu, rw_w0, rw_w_up, rw_a0, rw_a_up, rw_g_up, rw_k_k, rw_k_a, rw_r_k, rw_gn_g, rw_gn_b, w_out, ln1_g, ln1_b, peer_w_q, peer_keys1, peer_keys2, peer_u, peer_v, ln2_g, ln2_b):
    bsz, seq, dim = x.shape
    h = x.reshape(bsz * seq, dim)
    for l in range(w_in.shape[0]):
        h = _layer(
            h, bsz, seq, w_in[l],
            (s5_lam_re[l], s5_lam_im[l], s5_log_step[l], s5_b_re[l], s5_b_im[l], s5_c_re[l],
             s5_c_im[l], s5_d[l], s5_w_glu[l], s5_b_glu[l]),
            (rw_mu[l], rw_w0[l], rw_w_up[l], rw_a0[l], rw_a_up[l], rw_g_up[l], rw_k_k[l],
             rw_k_a[l], rw_r_k[l], rw_gn_g[l], rw_gn_b[l]),
            w_out[l], (ln1_g[l], ln1_b[l]),
            (peer_w_q[l], peer_keys1[l], peer_keys2[l], peer_u[l], peer_v[l]),
            (ln2_g[l], ln2_b[l]))
    return h.reshape(bsz, seq, dim)
```

```python
import functools
import math

import jax
import jax.numpy as jnp
from jax import lax
from jax.experimental import pallas as pl
from jax.experimental.pallas import tpu as pltpu

F32 = jnp.float32
BF16 = jnp.bfloat16
F8 = jnp.float8_e4m3fn
F8_TARGET = 224.0

D_MODEL = 2048
D_SSM = 1024
D_RWKV = 1024
S5_CH = 16
S5_GROUPS = D_SSM // S5_CH
S5_STATE = 64
S5_CHUNK = 16
RW_HEAD = 64
RW_HEADS = D_RWKV // RW_HEAD
RW_W_LORA = 64
RW_A_LORA = 64
RW_G_LORA = 160
RW_LORA = RW_W_LORA + RW_A_LORA + RW_G_LORA
RW_LORA_PAD = 384
RW_CHUNK = 64
PEER_HEADS = 8
PEER_NKEYS = 128
PEER_EXPERTS = PEER_NKEYS * PEER_NKEYS
PEER_HALF = 128
PEER_TOPK = 16
DEPTH = 1
DEEPNORM_ALPHA = (2.0 * DEPTH) ** 0.25
LN_EPS = 1e-5
RW_GN_EPS = 64e-5
NEG_INF = float("-inf")

VMEM_LIMIT = 60 * 1024 * 1024


def _cparams(sem):
    return pltpu.CompilerParams(dimension_semantics=sem, vmem_limit_bytes=VMEM_LIMIT)


def _dot(a, b):
    return jnp.dot(a.astype(BF16), b.astype(BF16), preferred_element_type=F32)


def _dot_nt(a, b):
    return lax.dot_general(a.astype(BF16), b.astype(BF16), (((1,), (1,)), ((), ())),
                           preferred_element_type=F32)


def _dot_tn(a, b):
    return lax.dot_general(a.astype(BF16), b.astype(BF16), (((0,), (0,)), ((), ())),
                           preferred_element_type=F32)


def _gelu(x):
    c = math.sqrt(2.0 / math.pi)
    return 0.5 * x * (1.0 + jnp.tanh(c * (x + 0.044715 * (x * x * x))))


def _sigmoid(x):
    return 1.0 / (1.0 + jnp.exp(-x))


def _head_sum(x, ones_ref):
    x_hi = x.astype(BF16)
    x_lo = (x - x_hi.astype(F32)).astype(BF16)
    w = ones_ref.shape[0]
    return jnp.concatenate(
        [jnp.dot(x_hi[:, i:i + w], ones_ref[...], preferred_element_type=F32)
         + jnp.dot(x_lo[:, i:i + w], ones_ref[...], preferred_element_type=F32)
         for i in range(0, x.shape[1], w)], axis=1)


def _pow2_scale(amax):
    return jnp.exp2(jnp.floor(jnp.log2(F8_TARGET / jnp.maximum(amax, 1e-30))))


def _head_ones():
    hid = jnp.arange(4 * RW_HEAD) // RW_HEAD
    return (hid[:, None] == hid[None, :]).astype(BF16)


def _mm_kernel(a_ref, b_ref, o_ref):
    o_ref[...] = jnp.dot(a_ref[...], b_ref[...],
                         preferred_element_type=F32).astype(o_ref.dtype)


def _matmul(a, b, out_dtype, tm, tn, col0=0, n=None):
    m, k = a.shape
    n = b.shape[1] if n is None else n
    tm = min(tm, m)
    assert col0 % tn == 0 and n % tn == 0
    return pl.pallas_call(
        _mm_kernel,
        grid=(m // tm, n // tn),
        in_specs=[pl.BlockSpec((tm, k), lambda i, j: (i, 0)),
                  pl.BlockSpec((k, tn), lambda i, j: (0, j + col0 // tn))],
        out_specs=pl.BlockSpec((tm, tn), lambda i, j: (i, j)),
        out_shape=jax.ShapeDtypeStruct((m, n), out_dtype),
        compiler_params=_cparams(("parallel", "parallel")),
    )(a, b)


def _s5_kernel(u_ref, mt_ref, bc_ref, cc_ref, ar_ref, ai_ref, d_ref, o_ref, *, nc):
    c = S5_CHUNK
    rows = u_ref.shape[0] // c
    groups = u_ref.shape[1] // S5_CH
    xs = [u_ref[pl.ds(tl, rows, stride=c), :] for tl in range(c)]
    cidx = lax.broadcasted_iota(jnp.int32, (rows, 2 * S5_STATE), 0) & (nc - 1)
    ys = []
    for g in range(groups):
        lanes = slice(g * S5_CH, (g + 1) * S5_CH)
        u = jnp.concatenate([x[:, lanes] for x in xs], axis=1).astype(BF16)
        y = jnp.dot(u, mt_ref[g], preferred_element_type=F32)
        s = jnp.dot(u, bc_ref[g], preferred_element_type=F32)
        d, k = 1, 0
        while d < nc:
            sh = jnp.where(cidx >= d, pltpu.roll(s, d, axis=0), 0.0)
            shs = pltpu.roll(sh, S5_STATE, axis=1)
            s = s + sh * ar_ref[g, k:k + 1, :] + shs * ai_ref[g, k:k + 1, :]
            d, k = d * 2, k + 1
        sp = jnp.where(cidx >= 1, pltpu.roll(s, 1, axis=0), 0.0)
        ys.append(y + jnp.dot(sp.astype(BF16), cc_ref[g], preferred_element_type=F32))
    for tl in range(c):
        y = jnp.concatenate([yg[:, tl * S5_CH:(tl + 1) * S5_CH] for yg in ys], axis=1)
        o_ref[pl.ds(tl, rows, stride=c), :] = _gelu(y + d_ref[...] * xs[tl])


def _s5_params(lam_re, lam_im, log_step, b_re, b_im, c_re, c_im, d_skip, nc):
    hi = lax.Precision.HIGHEST
    c = S5_CHUNK
    step = jnp.exp(log_step)[:, None]
    a_re = jnp.exp(lam_re * step) * jnp.cos(lam_im * step)
    a_im = jnp.exp(lam_re * step) * jnp.sin(lam_im * step)
    den = lam_re * lam_re + lam_im * lam_im
    f_re = ((a_re - 1.0) * lam_re + a_im * lam_im) / den
    f_im = (a_im * lam_re - (a_re - 1.0) * lam_im) / den
    bb_re = f_re[..., None] * b_re - f_im[..., None] * b_im
    bb_im = f_re[..., None] * b_im + f_im[..., None] * b_re

    def power(j):
        jj = j[None, :, None]
        mag = jnp.exp(lam_re[:, None, :] * step[:, None, :] * jj)
        ang = lam_im[:, None, :] * step[:, None, :] * jj
        return mag * jnp.cos(ang), mag * jnp.sin(ang)

    pw_re, pw_im = power(jnp.arange(c + 1, dtype=F32))
    ce_re = c_re[:, None] * pw_re[:, :, None, :] - c_im[:, None] * pw_im[:, :, None, :]
    ce_im = c_re[:, None] * pw_im[:, :, None, :] + c_im[:, None] * pw_re[:, :, None, :]
    kern = jnp.einsum('gjop,gpi->gjoi', jnp.concatenate([ce_re[:, :c], -ce_im[:, :c]], axis=-1),
                      jnp.concatenate([bb_re, bb_im], axis=1), precision=hi)
    t_idx = jnp.arange(c)
    lag = t_idx[None, :] - t_idx[:, None]
    place = (lag[:, :, None] == t_idx[None, None, :]).astype(F32)
    g = lam_re.shape[0]
    mt = jnp.einsum('stj,gjoi->gsito', place, kern, precision=hi)
    mt = mt.reshape(g, c * S5_CH, c * S5_CH)
    rv_re, rv_im = pw_re[:, c - 1::-1][:, :c], pw_im[:, c - 1::-1][:, :c]
    bc_r = rv_re[:, :, None, :] * jnp.transpose(bb_re, (0, 2, 1))[:, None] \
        - rv_im[:, :, None, :] * jnp.transpose(bb_im, (0, 2, 1))[:, None]
    bc_i = rv_re[:, :, None, :] * jnp.transpose(bb_im, (0, 2, 1))[:, None] \
        + rv_im[:, :, None, :] * jnp.transpose(bb_re, (0, 2, 1))[:, None]
    bc = jnp.concatenate([bc_r, bc_i], axis=-1).reshape(g, c * S5_CH, 2 * S5_STATE)
    cc_r = jnp.transpose(ce_re[:, 1:], (0, 3, 1, 2)).reshape(g, S5_STATE, c * S5_CH)
    cc_i = -jnp.transpose(ce_im[:, 1:], (0, 3, 1, 2)).reshape(g, S5_STATE, c * S5_CH)
    cc = jnp.concatenate([cc_r, cc_i], axis=1)
    sc_re, sc_im = power(c * (2.0 ** jnp.arange(8, dtype=F32)))
    ar = jnp.concatenate([sc_re, sc_re], axis=-1)
    ai = jnp.concatenate([-sc_im, sc_im], axis=-1)
    return mt.astype(BF16), bc.astype(BF16), cc.astype(BF16), ar, ai, d_skip[None, :]


def _s5_scan(u_tok, params, bsz, seq, gpb=8):
    mt, bc, cc, ar, ai, d_row = params
    nc = seq // S5_CHUNK
    t = bsz * seq
    w = S5_CHUNK * S5_CH
    lanes = gpb * S5_CH
    tok = pl.BlockSpec((t, lanes), lambda i: (0, i))
    spec3 = lambda a, b: pl.BlockSpec((gpb, a, b), lambda i: (i, 0, 0))
    return pl.pallas_call(
        functools.partial(_s5_kernel, nc=nc),
        grid=(S5_GROUPS // gpb,),
        in_specs=[tok, spec3(w, w), spec3(w, 2 * S5_STATE), spec3(2 * S5_STATE, w),
                  spec3(8, 2 * S5_STATE), spec3(8, 2 * S5_STATE),
                  pl.BlockSpec((1, lanes), lambda i: (0, i))],
        out_specs=tok,
        out_shape=jax.ShapeDtypeStruct((t, D_SSM), F32),
        compiler_params=_cparams(("parallel",)),
    )(u_tok, mt, bc, cc, ar, ai, d_row)


def _glu_kernel(y_ref, w_ref, b_ref, o_ref):
    y = y_ref[...]
    z = jnp.dot(y.astype(BF16), w_ref[...], preferred_element_type=F32) + b_ref[...]
    o_ref[...] = (y * _sigmoid(z)).astype(o_ref.dtype)


def _glu(y, w, b, tm=512):
    m, n = y.shape
    tm = min(tm, m)
    return pl.pallas_call(
        _glu_kernel,
        grid=(m // tm,),
        in_specs=[pl.BlockSpec((tm, n), lambda i: (i, 0)),
                  pl.BlockSpec((n, n), lambda i: (0, 0)),
                  pl.BlockSpec((1, n), lambda i: (0, 0))],
        out_specs=pl.BlockSpec((tm, n), lambda i: (i, 0)),
        out_shape=jax.ShapeDtypeStruct((m, n), BF16),
        compiler_params=_cparams(("parallel",)),
    )(y, w, b)


def _rw_prep_kernel(p_ref, pp_ref, l_ref, lp_ref, mu_ref, mul_ref, w0_ref, a0_ref,
                    kk_ref, ka_ref, wup_ref, aup_ref, gup_ref,
                    r_ref, k_ref, v_ref, kkr_ref, a_ref, lw_ref, g_ref, *, tiles_per_seq):
    first = (pl.program_id(0) % tiles_per_seq) == 0

    def shifted(cur_ref, prev_ref, mu):
        cur = cur_ref[...].astype(F32)
        prev_row = jnp.where(first, 0.0, prev_ref[15:16, :].astype(F32))
        row = lax.broadcasted_iota(jnp.int32, cur.shape, 0)
        prev = jnp.where(row == 0, prev_row, pltpu.roll(cur, 1, axis=0))
        return cur + mu * (prev - cur)

    p = shifted(p_ref, pp_ref, mu_ref[...])
    lo = shifted(l_ref, lp_ref, mul_ref[...])
    r = p[:, :D_RWKV]
    k = p[:, D_RWKV:2 * D_RWKV]
    v = p[:, 2 * D_RWKV:]
    w_pre = w0_ref[...] + _dot(jnp.tanh(lo), wup_ref[...])
    a = _sigmoid(a0_ref[...] + _dot(lo, aup_ref[...]))
    g = _dot(_sigmoid(lo), gup_ref[...])
    z = -w_pre
    softplus = jnp.maximum(z, 0.0) + jnp.log(1.0 + jnp.exp(-jnp.abs(z)))
    w = -softplus - 0.5
    r_ref[...] = r.astype(r_ref.dtype)
    k_ref[...] = (k * (1.0 + (a - 1.0) * ka_ref[...])).astype(k_ref.dtype)
    v_ref[...] = v.astype(v_ref.dtype)
    kkr_ref[...] = (k * kk_ref[...]).astype(kkr_ref.dtype)
    a_ref[...] = a.astype(a_ref.dtype)
    lw_ref[...] = -jnp.exp(w)
    g_ref[...] = g.astype(g_ref.dtype)


def _rw_prep(p_rkv, p_lora, mu_rkv, mu_lora, w0, a0, k_k, k_a, wup, aup, gup, seq, tm=256):
    t = p_rkv.shape[0]
    tm = min(tm, seq)
    n3 = 3 * D_RWKV
    row = lambda n: pl.BlockSpec((1, n), lambda i: (0, 0))
    full = lambda a, b: pl.BlockSpec((a, b), lambda i: (0, 0))
    prev = lambda n: pl.BlockSpec((16, n), lambda i: (jnp.maximum(i * (tm // 16) - 1, 0), 0))
    out = lambda dt: jax.ShapeDtypeStruct((t, D_RWKV), dt)
    ospec = pl.BlockSpec((tm, D_RWKV), lambda i: (i, 0))
    return pl.pallas_call(
        functools.partial(_rw_prep_kernel, tiles_per_seq=seq // tm),
        grid=(t // tm,),
        in_specs=[pl.BlockSpec((tm, n3), lambda i: (i, 0)), prev(n3),
                  pl.BlockSpec((tm, RW_LORA_PAD), lambda i: (i, 0)), prev(RW_LORA_PAD),
                  row(n3), row(RW_LORA_PAD), row(D_RWKV), row(D_RWKV), row(D_RWKV), row(D_RWKV),
                  full(RW_LORA_PAD, D_RWKV), full(RW_LORA_PAD, D_RWKV), full(RW_LORA_PAD, D_RWKV)],
        out_specs=[ospec] * 7,
        out_shape=[out(BF16)] * 5 + [out(F32), out(BF16)],
        compiler_params=_cparams(("parallel",)),
    )(p_rkv, p_rkv, p_lora, p_lora, mu_rkv, mu_lora, w0, a0, k_k, k_a, wup, aup, gup)


def _rw_chunk_kernel(r_ref, k_ref, v_ref, kkr_ref, a_ref, lw_ref, rk_ref, ones_ref,
                     q_ref, oi_ref, bonus_ref, m_ref, n_ref):
    c = RW_CHUNK
    hd = RW_HEAD
    heads = range(r_ref.shape[1] // hd)
    ri = lax.broadcasted_iota(jnp.int32, (c, c), 0)
    ci = lax.broadcasted_iota(jnp.int32, (c, c), 1)
    tri_incl = (ci <= ri)
    tri_strict = (ci < ri)
    ltri = tri_incl.astype(BF16)
    eye = (ci == ri).astype(F32)
    lw = lw_ref[...]
    lw_hi = lw.astype(BF16)
    lw_lo = (lw - lw_hi.astype(F32)).astype(BF16)
    cl = (jnp.dot(ltri, lw_hi, preferred_element_type=F32)
          + jnp.dot(ltri, lw_lo, preferred_element_type=F32))
    cl_end = cl[c - 1:c, :]
    r = r_ref[...].astype(F32)
    k = k_ref[...].astype(F32)
    rt_d = r * jnp.exp(cl)
    e_neg_d = jnp.exp(-cl)
    kt_d = k * e_neg_d
    e_prev_d = jnp.exp(cl - lw)
    e_end_d = jnp.exp(cl_end - cl)
    kte_d = k * e_end_d
    gam_d = jnp.exp(cl_end)
    kkr_d = kkr_ref[...].astype(F32)

    nrm = jnp.sqrt(_head_sum(kkr_d * kkr_d, ones_ref))
    kk_d = kkr_d / jnp.maximum(nrm, 1e-12)
    b_d = kk_d * a_ref[...].astype(F32)
    bte_d = b_d * e_end_d
    kr_d = jnp.concatenate([kk_d * e_prev_d, rt_d], axis=0)
    kb_d = jnp.concatenate([kt_d, b_d * e_neg_d], axis=0)
    v_d = v_ref[...]
    bonus_ref[...] = _head_sum(r * k * rk_ref[0:1, :], ones_ref) * v_d.astype(F32)
    sl = [slice(h * hd, (h + 1) * hd) for h in heads]
    v = [v_d[:, s] for s in sl]
    rt = [rt_d[:, s] for s in sl]
    kkt = [kr_d[:c, s] for s in sl]
    bte = [bte_d[:, s] for s in sl]
    a4 = [_dot_nt(kr_d[:, s], kb_d[:, s]) for s in sl]
    a_kb = [jnp.where(tri_strict, a4[h][:c, c:], 0.0) for h in heads]
    a_rb = [jnp.where(tri_incl, a4[h][c:, c:], 0.0) for h in heads]
    a_kr = [jnp.concatenate([jnp.where(tri_strict, a4[h][:c, :c], 0.0),
                             jnp.where(tri_incl, a4[h][c:, :c], 0.0)], axis=0) for h in heads]
    x = [eye - a_kb[h] for h in heads]
    pw = [_dot(a_kb[h], a_kb[h]) for h in heads]
    n_sq = int(math.log2(c)) - 1
    for it in range(n_sq):
        if it + 1 < n_sq:
            xp = [_dot(jnp.concatenate([x[h], pw[h]], axis=0), pw[h]) for h in heads]
            x = [x[h] + xp[h][:c] for h in heads]
            pw = [xp[h][c:] for h in heads]
        else:
            x = [x[h] + _dot(x[h], pw[h]) for h in heads]
    av = [_dot(a_kr[h], v[h]) for h in heads]
    wu = [_dot(x[h], jnp.concatenate([kkt[h], av[h][:c]], axis=1)) for h in heads]
    rb = [_dot(a_rb[h], wu[h]) for h in heads]
    wub = [_dot_tn(wu[h], bte[h]) for h in heads]
    vk = [_dot_tn(v[h], kte_d[:, sl[h]]) for h in heads]
    ji = lax.broadcasted_iota(jnp.int32, (hd, hd), 0)
    jo = lax.broadcasted_iota(jnp.int32, (hd, hd), 1)
    for h in heads:
        m_ref[0, h] = jnp.where(ji == jo, gam_d[:, sl[h]], 0.0) - wub[h][:hd]
        n_ref[0, h] = vk[h] - wub[h][hd:]
    q_ref[...] = jnp.concatenate([rt[h] - rb[h][:, :hd] for h in heads], axis=1)
    oi_ref[...] = jnp.concatenate([av[h][c:] - rb[h][:, hd:] for h in heads], axis=1)


def _rw_chunks(r, k, v, kkr, a, lw, r_k, hp=RW_HEADS):
    t = r.shape[0]
    c = RW_CHUNK
    nch = t // c
    wdt = hp * RW_HEAD
    blk = pl.BlockSpec((c, wdt), lambda i, j: (i, j))
    mat = pl.BlockSpec((1, hp, RW_HEAD, RW_HEAD), lambda i, j: (i, j, 0, 0))
    tok = jax.ShapeDtypeStruct((t, D_RWKV), F32)
    mshape = jax.ShapeDtypeStruct((nch, RW_HEADS, RW_HEAD, RW_HEAD), F32)
    return pl.pallas_call(
        _rw_chunk_kernel,
        grid=(nch, RW_HEADS // hp),
        in_specs=[blk] * 6 + [pl.BlockSpec((1, wdt), lambda i, j: (0, j)),
                  pl.BlockSpec((4 * RW_HEAD, 4 * RW_HEAD), lambda i, j: (0, 0))],
        out_specs=[blk, blk, blk, mat, mat],
        out_shape=[tok, tok, tok, mshape, mshape],
        compiler_params=_cparams(("parallel", "parallel")),
    )(r, k, v, kkr, a, lw, r_k, _head_ones())


def _rw_seq_kernel(q_ref, oi_ref, bonus_ref, g_ref, m_ref, n_ref, gg_ref, gb_ref, ones_ref,
                   o_ref, st_ref):
    @pl.when(pl.program_id(0) == 0)
    def _():
        st_ref[...] = jnp.zeros_like(st_ref)

    hd = RW_HEAD
    sl = [slice(h * hd, (h + 1) * hd) for h in range(RW_HEADS)]
    streams = [(b, h) for b in range(q_ref.shape[0]) for h in range(RW_HEADS)]
    s = {bh: st_ref[bh[0], bh[1]] for bh in streams}
    q = [q_ref[b] for b in range(q_ref.shape[0])]
    o = {(b, h): _dot_nt(q[b][:, sl[h]], s[b, h]) for b, h in streams}
    for b, h in streams:
        st_ref[b, h] = _dot(s[b, h], m_ref[b, 0, h]) + n_ref[b, 0, h]
    inv_n = 1.0 / hd
    for b in range(q_ref.shape[0]):
        ob = jnp.concatenate([o[b, h] for h in range(RW_HEADS)], axis=1) + oi_ref[b]
        ctr = ob - _head_sum(ob, ones_ref) * inv_n
        var = _head_sum(ctr * ctr, ones_ref) * inv_n
        y = ctr * lax.rsqrt(var + RW_GN_EPS) * gg_ref[...] + gb_ref[...] + bonus_ref[b]
        o_ref[b] = (y * g_ref[b].astype(F32)).astype(o_ref.dtype)


def _rw_seq(q, oi, bonus, g, m, n, gn_g, gn_b, bsz, seq):
    c = RW_CHUNK
    nch = seq // c
    r3 = lambda x: x.reshape(bsz, seq, D_RWKV)
    r5 = lambda x: x.reshape(bsz, nch, RW_HEADS, RW_HEAD, RW_HEAD)
    tok = pl.BlockSpec((bsz, c, D_RWKV), lambda i: (0, i, 0))
    mat = pl.BlockSpec((bsz, 1, RW_HEADS, RW_HEAD, RW_HEAD), lambda i: (0, i, 0, 0, 0))
    row = pl.BlockSpec((1, D_RWKV), lambda i: (0, 0))
    y = pl.pallas_call(
        _rw_seq_kernel,
        grid=(nch,),
        in_specs=[tok, tok, tok, tok, mat, mat, row, row,
                  pl.BlockSpec((4 * RW_HEAD, 4 * RW_HEAD), lambda i: (0, 0))],
        out_specs=tok,
        out_shape=jax.ShapeDtypeStruct((bsz, seq, D_RWKV), BF16),
        scratch_shapes=[pltpu.VMEM((bsz, RW_HEADS, RW_HEAD, RW_HEAD), F32)],
        compiler_params=_cparams(("arbitrary",)),
    )(r3(q), r3(oi), r3(bonus), r3(g), r5(m), r5(n), gn_g, gn_b, _head_ones())
    return y.reshape(bsz * seq, D_RWKV)


def _layer_norm(x, g, b):
    mu = jnp.mean(x, axis=-1, keepdims=True)
    var = jnp.mean(jnp.square(x - mu), axis=-1, keepdims=True)
    return (x - mu) * lax.rsqrt(var + LN_EPS) * g + b


def _outproj_kernel(ys_ref, yr_ref, w1_ref, w2_ref, x_ref, g_ref, b_ref, h_ref, hb_ref, ht_ref, isc_ref):
    mix = (jnp.dot(ys_ref[...], w1_ref[...], preferred_element_type=F32)
           + jnp.dot(yr_ref[...], w2_ref[...], preferred_element_type=F32))
    h = _layer_norm(DEEPNORM_ALPHA * x_ref[...] + mix, g_ref[...], b_ref[...])
    h_ref[...] = h
    hb_ref[...] = h.astype(BF16)
    ht = h.T
    sx = _pow2_scale(jnp.max(jnp.abs(ht), axis=0, keepdims=True))
    ht_ref[...] = (ht * sx).astype(F8)
    isc_ref[...] = 1.0 / sx


def _outproj(ys, yr, w_out, x, g, b, tm=512):
    t = x.shape[0]
    tm = min(tm, t)
    half = pl.BlockSpec((tm, D_SSM), lambda i: (i, 0))
    full = pl.BlockSpec((tm, D_MODEL), lambda i: (i, 0))
    row = pl.BlockSpec((1, D_MODEL), lambda i: (0, 0))
    return pl.pallas_call(
        _outproj_kernel,
        grid=(t // tm,),
        in_specs=[half, half,
                  pl.BlockSpec((D_SSM, D_MODEL), lambda i: (0, 0)),
                  pl.BlockSpec((D_RWKV, D_MODEL), lambda i: (1, 0)),
                  full, row, row],
        out_specs=[full, full, pl.BlockSpec((D_MODEL, tm), lambda i: (0, i)),
                   pl.BlockSpec((1, tm), lambda i: (0, i))],
        out_shape=[jax.ShapeDtypeStruct((t, D_MODEL), F32),
                   jax.ShapeDtypeStruct((t, D_MODEL), BF16),
                   jax.ShapeDtypeStruct((D_MODEL, t), F8),
                   jax.ShapeDtypeStruct((1, t), F32)],
        compiler_params=_cparams(("parallel",)),
    )(ys, yr, w_out, w_out, x, g, b)


def _peer_scores_kernel(h_ref, wq_ref, keys_ref, s_ref):
    q = jnp.dot(h_ref[...], wq_ref[...], preferred_element_type=F32)
    for blk in range(2 * PEER_HEADS):
        qb = q[:, blk * PEER_HALF:(blk + 1) * PEER_HALF]
        s_ref[blk] = _dot_nt(keys_ref[blk], qb)


def _peer_scores(hb, wq, keys, tm=512):
    t = hb.shape[0]
    tm = min(tm, t)
    nb = 2 * PEER_HEADS
    return pl.pallas_call(
        _peer_scores_kernel,
        grid=(t // tm,),
        in_specs=[pl.BlockSpec((tm, D_MODEL), lambda i: (i, 0)),
                  pl.BlockSpec((D_MODEL, D_MODEL), lambda i: (0, 0)),
                  pl.BlockSpec((nb, PEER_NKEYS, PEER_HALF), lambda i: (0, 0, 0))],
        out_specs=pl.BlockSpec((nb, PEER_NKEYS, tm), lambda i: (0, 0, i)),
        out_shape=jax.ShapeDtypeStruct((nb, PEER_NKEYS, t), F32),
        compiler_params=_cparams(("parallel",)),
    )(hb, wq, keys)


NO_RANK = 127.0


def _extract_top(x, n, want_rank=False):
    vals = []
    rank = jnp.full(x.shape, NO_RANK, F32) if want_rank else None
    for i in range(n):
        m = jnp.max(x, axis=0, keepdims=True)
        vals.append(m)
        hit = x == m
        if want_rank:
            rank = jnp.where(hit, float(i), rank)
        x = jnp.where(hit, NEG_INF, x)
    return vals, rank


def _peer_topk_kernel(s_ref, nsel_ref, e1_ref, rank2_ref, e2_ref):
    k = PEER_TOPK

    def per_head(h, carry):
        s1 = s_ref[2 * h]
        s2 = s_ref[2 * h + 1]
        ta, _ = _extract_top(s1, k)
        tb, rank2 = _extract_top(s2, k, want_rank=True)
        tbs = jnp.concatenate(tb, axis=0)
        cands = [ta[i] + tbs[0:k // (i + 1), :] for i in range(k)]
        n_c = sum(k // (i + 1) for i in range(k))
        pad = (-n_c) % 8
        padded = cands + ([jnp.full((pad, s1.shape[1]), NEG_INF, F32)] if pad else [])
        best, _ = _extract_top(jnp.concatenate(padded, axis=0), k)
        m0 = best[0]
        tau = best[k - 1]
        z = jnp.zeros_like(m0)
        for bv in best:
            z = z + jnp.exp(bv - m0)
        nsel = jnp.zeros_like(s1)
        for i in range(k):
            cnt = jnp.sum(jnp.where(cands[i] >= tau, 1.0, 0.0), axis=0, keepdims=True)
            nsel = jnp.where(s1 == ta[i], cnt, nsel)
        nsel_ref[h] = nsel
        e1_ref[h] = jnp.exp(s1 - ta[0]) / z
        rank2_ref[h] = rank2.astype(BF16)
        e2_ref[h] = jnp.exp(s2 - tb[0]).astype(BF16)
        return carry

    lax.fori_loop(0, PEER_HEADS, per_head, 0)


def _peer_topk(scores, tt=256):
    nb, nk, t = scores.shape
    tt = min(tt, t)
    big = pl.BlockSpec((PEER_HEADS, nk, tt), lambda i: (0, 0, i))
    f32 = jax.ShapeDtypeStruct((PEER_HEADS, nk, t), F32)
    b16 = jax.ShapeDtypeStruct((PEER_HEADS, nk, t), BF16)
    return pl.pallas_call(
        _peer_topk_kernel,
        grid=(t // tt,),
        in_specs=[pl.BlockSpec((nb, nk, tt), lambda i: (0, 0, i))],
        out_specs=[big, big, big, big],
        out_shape=[f32, f32, b16, b16],
        compiler_params=_cparams(("parallel",)),
    )(scores)


def _peer_dense_kernel(ht_ref, asc_ref, u0_ref, ub_ref, un_ref, v_ref, ns0_ref, e10_ref, nsc_ref, e1c_ref,
                       nsn_ref, e1n_ref, rank2_ref, e2_ref, h1_ref, lg_ref, lb_ref,
                       o_ref, hsa_ref, hsb_ref, *, q, nj):
    j = pl.program_id(1)
    nk = PEER_NKEYS
    tm = ht_ref.shape[1]
    half = q * nk
    ht = ht_ref[...]

    def gate_part(gate, heads, ns_ref, e1_ref, row, live):
        for h in heads:
            nsel = ns_ref[h, 0, row:row + 1, :]
            nsel = nsel if live is None else nsel * live
            nsel = jnp.concatenate([jnp.broadcast_to(nsel, (16, tm)).astype(BF16)] * (nk // 16), axis=0)
            e1 = jnp.concatenate(
                [jnp.broadcast_to(e1_ref[h, 0, row:row + 1, :], (16, tm)).astype(BF16)] * (nk // 16), axis=0)
            gate = gate + e1 * jnp.where(rank2_ref[h] < nsel, e2_ref[h], 0.0)
        return gate

    def hidden_block(u_ref, ns_ref, e1_ref, row, qi, live, dst_ref, values=None):
        gate = gate_part(jnp.zeros((nk, tm), BF16), range(0, PEER_HEADS // 2), ns_ref, e1_ref, row, live)
        if values is not None:
            values()
        gate = gate_part(gate, range(PEER_HEADS // 2, PEER_HEADS), ns_ref, e1_ref, row, live)
        act = jnp.dot(u_ref[qi * nk:(qi + 1) * nk, :], ht, preferred_element_type=F32) * asc_ref[...]
        c0 = math.sqrt(2.0 / math.pi)
        inner = act * (c0 + (c0 * 0.044715) * (act * act))
        half_act = 0.5 * act.astype(BF16)
        hid = (half_act + half_act * jnp.tanh(inner.astype(BF16))) * gate
        dst_ref[:, qi * nk:(qi + 1) * nk] = hid.T

    @pl.when(j == 0)
    def _():
        o_ref[...] = jnp.zeros_like(o_ref)
        for qi in range(q):
            hidden_block(u0_ref, ns0_ref, e10_ref, qi, qi, None, hsa_ref)

    def phase(src_ref, v_rows, u_ref, ns_ref, e1_ref, row0, live, dst_ref):
        wv = D_MODEL // q
        for qi in range(q):
            cols = slice(qi * wv, (qi + 1) * wv)

            def values():
                o_ref[:, cols] += jnp.dot(src_ref[...], v_ref[v_rows, cols], preferred_element_type=F32)

            hidden_block(u_ref, ns_ref, e1_ref, row0 + qi, qi, live, dst_ref, values)

    phase(hsa_ref, slice(0, half), ub_ref, nsc_ref, e1c_ref, q, None, hsb_ref)
    more = jnp.where(j + 1 < nj, 1.0, 0.0)
    phase(hsb_ref, slice(half, 2 * half), un_ref, nsn_ref, e1n_ref, 0, more, hsa_ref)

    @pl.when(j == nj - 1)
    def _():
        o_ref[...] = _layer_norm(DEEPNORM_ALPHA * h1_ref[...] + o_ref[...], lg_ref[...], lb_ref[...])


def _peer_dense(ht, act_scale, u_tab, v_tab, nsel, e1, rank2, e2, h1, ln_g, ln_b, tm=512, te=2048):
    t = ht.shape[1]
    tm = min(tm, t)
    half = te // 2
    q = half // PEER_NKEYS
    ng = PEER_NKEYS // (2 * q)
    nj = PEER_EXPERTS // te
    rows = lambda a: a.reshape(PEER_HEADS, ng, 2 * q, t)
    nxt = lambda j: jnp.minimum(j + 1, nj - 1)
    u_spec = lambda f: pl.BlockSpec((half, D_MODEL), lambda i, j: (f(j), 0))
    row_spec = lambda f: pl.BlockSpec((PEER_HEADS, 1, 2 * q, tm), lambda i, j: (0, f(j), 0, i))
    key_spec = pl.BlockSpec((PEER_HEADS, PEER_NKEYS, tm), lambda i, j: (0, 0, i))
    vec_spec = pl.BlockSpec((1, D_MODEL), lambda i, j: (0, 0))
    first, cur = (lambda j: 0), (lambda j: j)
    return pl.pallas_call(
        functools.partial(_peer_dense_kernel, q=q, nj=nj),
        grid=(t // tm, nj),
        in_specs=[pl.BlockSpec((D_MODEL, tm), lambda i, j: (0, i)),
                  pl.BlockSpec((1, tm), lambda i, j: (0, i)),
                  u_spec(first), u_spec(lambda j: 2 * j + 1), u_spec(lambda j: 2 * nxt(j)),
                  pl.BlockSpec((te, D_MODEL), lambda i, j: (j, 0)),
                  row_spec(first), row_spec(first), row_spec(cur), row_spec(cur),
                  row_spec(nxt), row_spec(nxt), key_spec, key_spec,
                  pl.BlockSpec((tm, D_MODEL), lambda i, j: (i, 0)), vec_spec, vec_spec],
        out_specs=pl.BlockSpec((tm, D_MODEL), lambda i, j: (i, 0)),
        out_shape=jax.ShapeDtypeStruct((t, D_MODEL), F32),
        scratch_shapes=[pltpu.VMEM((tm, half), BF16), pltpu.VMEM((tm, half), BF16)],
        compiler_params=_cparams(("parallel", "arbitrary")),
    )(ht, act_scale, u_tab, u_tab, u_tab, v_tab, rows(nsel), rows(e1), rows(nsel), rows(e1),
      rows(nsel), rows(e1), rank2, e2, h1, ln_g, ln_b)


def _layer(h, bsz, seq, w_in, s5, rw, w_out, ln1, peer, ln2):
    (lam_re, lam_im, log_step, b_re, b_im, c_re, c_im, d_skip, w_glu, b_glu) = s5
    (mu, w0, w_up, a0, a_up, g_up, k_k, k_a, r_k, gn_g, gn_b) = rw
    (w_q, keys1, keys2, u_tab, v_tab) = peer
    hb = h.astype(BF16)
    n_rkv = 3 * D_RWKV
    w_in_b = w_in.astype(BF16)
    pad = RW_LORA_PAD - RW_LORA

    u = _matmul(hb, w_in_b, F32, 1024, 1024, 0, D_SSM)
    p_rkv = _matmul(hb, w_in_b, BF16, 1024, 1024, D_SSM, n_rkv)
    p_lora = _matmul(hb, jnp.pad(w_in_b[:, D_SSM + n_rkv:], ((0, 0), (0, pad))), F32, 512, RW_LORA_PAD)

    yg = _s5_scan(u, _s5_params(lam_re, lam_im, log_step, b_re, b_im, c_re, c_im, d_skip,
                                seq // S5_CHUNK), bsz, seq)
    y_ssm = _glu(yg, w_glu.astype(BF16), b_glu[None, :])

    mu_rkv = mu[None, :n_rkv]
    mu_lora = jnp.pad(mu[n_rkv:], (0, pad))[None, :]
    zrow = lambda n: jnp.zeros((n, D_RWKV), F32)
    wup = jnp.concatenate([w_up, zrow(RW_LORA_PAD - RW_W_LORA)], axis=0).astype(BF16)
    aup = jnp.concatenate([zrow(RW_W_LORA), a_up, zrow(RW_LORA_PAD - RW_W_LORA - RW_A_LORA)],
                          axis=0).astype(BF16)
    gup = jnp.concatenate([zrow(RW_W_LORA + RW_A_LORA), g_up, zrow(pad)], axis=0).astype(BF16)
    r, k, v, kkr, a, lw, g = _rw_prep(p_rkv, p_lora, mu_rkv, mu_lora, w0[None], a0[None],
                                      k_k[None], k_a[None], wup, aup, gup, seq)
    q, oi, bonus, m, n = _rw_chunks(r, k, v, kkr, a, lw, r_k.reshape(1, D_RWKV))
    y_rw = _rw_seq(q, oi, bonus, g, m, n, gn_g[None], gn_b[None], bsz, seq)

    h1, h1b, h1t, inv_sx = _outproj(y_ssm, y_rw, w_out.astype(BF16), h, ln1[0][None], ln1[1][None])

    keys = jnp.stack([keys1, keys2], axis=1).reshape(2 * PEER_HEADS, PEER_NKEYS, PEER_HALF)
    scores = _peer_scores(h1b, w_q.astype(BF16), keys.astype(BF16))
    nsel, e1, rank2, e2 = _peer_topk(scores)
    su = _pow2_scale(jnp.max(jnp.abs(u_tab)))
    return _peer_dense(h1t, inv_sx / su, (u_tab * su).astype(F8), v_tab.astype(BF16), nsel, e1, rank2, e2,
                       h1, ln2[0][None], ln2[1][None])


def kernel(x, w_in, s5_lam_re, s5_lam_im, s5_log_step, s5_b_re, s5_b_im, s5_c_re, s5_c_im, s5_d, s5_w_glu, s5_b_glu, rw_mu, rw_w0, rw_w_up, rw_a0, rw_a_up, rw_g_up, rw_k_k, rw_k_a, rw_r_k, rw_gn_g, rw_gn_b, w_out, ln1_g, ln1_b, peer_w_q, peer_keys1, peer_keys2, peer_u, peer_v, ln2_g, ln2_b):
    bsz, seq, dim = x.shape
    h = x.reshape(bsz * seq, dim)
    for l in range(w_in.shape[0]):
        h = _layer(
            h, bsz, seq, w_in[l],
            (s5_lam_re[l], s5_lam_im[l], s5_log_step[l], s5_b_re[l], s5_b_im[l], s5_c_re[l],
             s5_c_im[l], s5_d[l], s5_w_glu[l], s5_b_glu[l]),
            (rw_mu[l], rw_w0[l], rw_w_up[l], rw_a0[l], rw_a_up[l], rw_g_up[l], rw_k_k[l],
             rw_k_a[l], rw_r_k[l], rw_gn_g[l], rw_gn_b[l]),
            w_out[l], (ln1_g[l], ln1_b[l]),
            (peer_w_q[l], peer_keys1[l], peer_keys2[l], peer_u[l], peer_v[l]),
            (ln2_g[l], ln2_b[l]))
    return h.reshape(bsz, seq, dim)
```

```python
import functools
import math

import jax
import jax.numpy as jnp
from jax import lax
from jax.experimental import pallas as pl
from jax.experimental.pallas import tpu as pltpu

F32 = jnp.float32
BF16 = jnp.bfloat16
F8 = jnp.float8_e4m3fn
F8_TARGET = 224.0

D_MODEL = 2048
D_SSM = 1024
D_RWKV = 1024
S5_CH = 16
S5_GROUPS = D_SSM // S5_CH
S5_STATE = 64
S5_CHUNK = 16
RW_HEAD = 64
RW_HEADS = D_RWKV // RW_HEAD
RW_W_LORA = 64
RW_A_LORA = 64
RW_G_LORA = 160
RW_LORA = RW_W_LORA + RW_A_LORA + RW_G_LORA
RW_LORA_PAD = 384
RW_CHUNK = 64
PEER_HEADS = 8
PEER_NKEYS = 128
PEER_EXPERTS = PEER_NKEYS * PEER_NKEYS
PEER_HALF = 128
PEER_TOPK = 16
DEPTH = 1
DEEPNORM_ALPHA = (2.0 * DEPTH) ** 0.25
LN_EPS = 1e-5
RW_GN_EPS = 64e-5
NEG_INF = float("-inf")

VMEM_LIMIT = 60 * 1024 * 1024


def _cparams(sem):
    return pltpu.CompilerParams(dimension_semantics=sem, vmem_limit_bytes=VMEM_LIMIT)


def _dot(a, b):
    return jnp.dot(a.astype(BF16), b.astype(BF16), preferred_element_type=F32)


def _dot_nt(a, b):
    return lax.dot_general(a.astype(BF16), b.astype(BF16), (((1,), (1,)), ((), ())),
                           preferred_element_type=F32)


def _dot_tn(a, b):
    return lax.dot_general(a.astype(BF16), b.astype(BF16), (((0,), (0,)), ((), ())),
                           preferred_element_type=F32)


def _gelu(x):
    c = math.sqrt(2.0 / math.pi)
    return 0.5 * x * (1.0 + jnp.tanh(c * (x + 0.044715 * (x * x * x))))


def _sigmoid(x):
    return 1.0 / (1.0 + jnp.exp(-x))


def _head_sum(x, ones_ref):
    x_hi = x.astype(BF16)
    x_lo = (x - x_hi.astype(F32)).astype(BF16)
    w = ones_ref.shape[0]
    return jnp.concatenate(
        [jnp.dot(x_hi[:, i:i + w], ones_ref[...], preferred_element_type=F32)
         + jnp.dot(x_lo[:, i:i + w], ones_ref[...], preferred_element_type=F32)
         for i in range(0, x.shape[1], w)], axis=1)


def _pow2_scale(amax):
    return jnp.exp2(jnp.floor(jnp.log2(F8_TARGET / jnp.maximum(amax, 1e-30))))


def _head_ones():
    hid = jnp.arange(4 * RW_HEAD) // RW_HEAD
    return (hid[:, None] == hid[None, :]).astype(BF16)


def _mm_kernel(a_ref, b_ref, o_ref):
    o_ref[...] = jnp.dot(a_ref[...], b_ref[...],
                         preferred_element_type=F32).astype(o_ref.dtype)


def _matmul(a, b, out_dtype, tm, tn, col0=0, n=None):
    m, k = a.shape
    n = b.shape[1] if n is None else n
    tm = min(tm, m)
    assert col0 % tn == 0 and n % tn == 0
    return pl.pallas_call(
        _mm_kernel,
        grid=(m // tm, n // tn),
        in_specs=[pl.BlockSpec((tm, k), lambda i, j: (i, 0)),
                  pl.BlockSpec((k, tn), lambda i, j: (0, j + col0 // tn))],
        out_specs=pl.BlockSpec((tm, tn), lambda i, j: (i, j)),
        out_shape=jax.ShapeDtypeStruct((m, n), out_dtype),
        compiler_params=_cparams(("parallel", "parallel")),
    )(a, b)


def _s5_kernel(u_ref, mt_ref, bc_ref, cc_ref, ar_ref, ai_ref, d_ref, o_ref, *, nc):
    c = S5_CHUNK
    rows = u_ref.shape[0] // c
    groups = u_ref.shape[1] // S5_CH
    xs = [u_ref[pl.ds(tl, rows, stride=c), :] for tl in range(c)]
    cidx = lax.broadcasted_iota(jnp.int32, (rows, 2 * S5_STATE), 0) & (nc - 1)
    ys = []
    for g in range(groups):
        lanes = slice(g * S5_CH, (g + 1) * S5_CH)
        u = jnp.concatenate([x[:, lanes] for x in xs], axis=1).astype(BF16)
        y = jnp.dot(u, mt_ref[g], preferred_element_type=F32)
        s = jnp.dot(u, bc_ref[g], preferred_element_type=F32)
        d, k = 1, 0
        while d < nc:
            sh = jnp.where(cidx >= d, pltpu.roll(s, d, axis=0), 0.0)
            shs = pltpu.roll(sh, S5_STATE, axis=1)
            s = s + sh * ar_ref[g, k:k + 1, :] + shs * ai_ref[g, k:k + 1, :]
            d, k = d * 2, k + 1
        sp = jnp.where(cidx >= 1, pltpu.roll(s, 1, axis=0), 0.0)
        ys.append(y + jnp.dot(sp.astype(BF16), cc_ref[g], preferred_element_type=F32))
    for tl in range(c):
        y = jnp.concatenate([yg[:, tl * S5_CH:(tl + 1) * S5_CH] for yg in ys], axis=1)
        o_ref[pl.ds(tl, rows, stride=c), :] = _gelu(y + d_ref[...] * xs[tl])


def _s5_params(lam_re, lam_im, log_step, b_re, b_im, c_re, c_im, d_skip, nc):
    hi = lax.Precision.HIGHEST
    c = S5_CHUNK
    step = jnp.exp(log_step)[:, None]
    a_re = jnp.exp(lam_re * step) * jnp.cos(lam_im * step)
    a_im = jnp.exp(lam_re * step) * jnp.sin(lam_im * step)
    den = lam_re * lam_re + lam_im * lam_im
    f_re = ((a_re - 1.0) * lam_re + a_im * lam_im) / den
    f_im = (a_im * lam_re - (a_re - 1.0) * lam_im) / den
    bb_re = f_re[..., None] * b_re - f_im[..., None] * b_im
    bb_im = f_re[..., None] * b_im + f_im[..., None] * b_re

    def power(j):
        jj = j[None, :, None]
        mag = jnp.exp(lam_re[:, None, :] * step[:, None, :] * jj)
        ang = lam_im[:, None, :] * step[:, None, :] * jj
        return mag * jnp.cos(ang), mag * jnp.sin(ang)

    pw_re, pw_im = power(jnp.arange(c + 1, dtype=F32))
    ce_re = c_re[:, None] * pw_re[:, :, None, :] - c_im[:, None] * pw_im[:, :, None, :]
    ce_im = c_re[:, None] * pw_im[:, :, None, :] + c_im[:, None] * pw_re[:, :, None, :]
    kern = jnp.einsum('gjop,gpi->gjoi', jnp.concatenate([ce_re[:, :c], -ce_im[:, :c]], axis=-1),
                      jnp.concatenate([bb_re, bb_im], axis=1), precision=hi)
    t_idx = jnp.arange(c)
    lag = t_idx[None, :] - t_idx[:, None]
    place = (lag[:, :, None] == t_idx[None, None, :]).astype(F32)
    g = lam_re.shape[0]
    mt = jnp.einsum('stj,gjoi->gsito', place, kern, precision=hi)
    mt = mt.reshape(g, c * S5_CH, c * S5_CH)
    rv_re, rv_im = pw_re[:, c - 1::-1][:, :c], pw_im[:, c - 1::-1][:, :c]
    bc_r = rv_re[:, :, None, :] * jnp.transpose(bb_re, (0, 2, 1))[:, None] \
        - rv_im[:, :, None, :] * jnp.transpose(bb_im, (0, 2, 1))[:, None]
    bc_i = rv_re[:, :, None, :] * jnp.transpose(bb_im, (0, 2, 1))[:, None] \
        + rv_im[:, :, None, :] * jnp.transpose(bb_re, (0, 2, 1))[:, None]
    bc = jnp.concatenate([bc_r, bc_i], axis=-1).reshape(g, c * S5_CH, 2 * S5_STATE)
    cc_r = jnp.transpose(ce_re[:, 1:], (0, 3, 1, 2)).reshape(g, S5_STATE, c * S5_CH)
    cc_i = -jnp.transpose(ce_im[:, 1:], (0, 3, 1, 2)).reshape(g, S5_STATE, c * S5_CH)
    cc = jnp.concatenate([cc_r, cc_i], axis=1)
    sc_re, sc_im = power(c * (2.0 ** jnp.arange(8, dtype=F32)))
    ar = jnp.concatenate([sc_re, sc_re], axis=-1)
    ai = jnp.concatenate([-sc_im, sc_im], axis=-1)
    return mt.astype(BF16), bc.astype(BF16), cc.astype(BF16), ar, ai, d_skip[None, :]


def _s5_scan(u_tok, params, bsz, seq, gpb=8):
    mt, bc, cc, ar, ai, d_row = params
    nc = seq // S5_CHUNK
    t = bsz * seq
    w = S5_CHUNK * S5_CH
    lanes = gpb * S5_CH
    tok = pl.BlockSpec((t, lanes), lambda i: (0, i))
    spec3 = lambda a, b: pl.BlockSpec((gpb, a, b), lambda i: (i, 0, 0))
    return pl.pallas_call(
        functools.partial(_s5_kernel, nc=nc),
        grid=(S5_GROUPS // gpb,),
        in_specs=[tok, spec3(w, w), spec3(w, 2 * S5_STATE), spec3(2 * S5_STATE, w),
                  spec3(8, 2 * S5_STATE), spec3(8, 2 * S5_STATE),
                  pl.BlockSpec((1, lanes), lambda i: (0, i))],
        out_specs=tok,
        out_shape=jax.ShapeDtypeStruct((t, D_SSM), F32),
        compiler_params=_cparams(("parallel",)),
    )(u_tok, mt, bc, cc, ar, ai, d_row)


def _glu_kernel(y_ref, w_ref, b_ref, o_ref):
    y = y_ref[...]
    z = jnp.dot(y.astype(BF16), w_ref[...], preferred_element_type=F32) + b_ref[...]
    o_ref[...] = (y * _sigmoid(z)).astype(o_ref.dtype)


def _glu(y, w, b, tm=512):
    m, n = y.shape
    tm = min(tm, m)
    return pl.pallas_call(
        _glu_kernel,
        grid=(m // tm,),
        in_specs=[pl.BlockSpec((tm, n), lambda i: (i, 0)),
                  pl.BlockSpec((n, n), lambda i: (0, 0)),
                  pl.BlockSpec((1, n), lambda i: (0, 0))],
        out_specs=pl.BlockSpec((tm, n), lambda i: (i, 0)),
        out_shape=jax.ShapeDtypeStruct((m, n), BF16),
        compiler_params=_cparams(("parallel",)),
    )(y, w, b)


def _rw_prep_kernel(p_ref, pp_ref, l_ref, lp_ref, mu_ref, mul_ref, w0_ref, a0_ref,
                    kk_ref, ka_ref, wup_ref, aup_ref, gup_ref,
                    r_ref, k_ref, v_ref, kkr_ref, a_ref, lw_ref, g_ref, *, tiles_per_seq):
    first = (pl.program_id(0) % tiles_per_seq) == 0

    def shifted(cur_ref, prev_ref, mu):
        cur = cur_ref[...].astype(F32)
        prev_row = jnp.where(first, 0.0, prev_ref[15:16, :].astype(F32))
        row = lax.broadcasted_iota(jnp.int32, cur.shape, 0)
        prev = jnp.where(row == 0, prev_row, pltpu.roll(cur, 1, axis=0))
        return cur + mu * (prev - cur)

    p = shifted(p_ref, pp_ref, mu_ref[...])
    lo = shifted(l_ref, lp_ref, mul_ref[...])
    r = p[:, :D_RWKV]
    k = p[:, D_RWKV:2 * D_RWKV]
    v = p[:, 2 * D_RWKV:]
    w_pre = w0_ref[...] + _dot(jnp.tanh(lo), wup_ref[...])
    a = _sigmoid(a0_ref[...] + _dot(lo, aup_ref[...]))
    g = _dot(_sigmoid(lo), gup_ref[...])
    z = -w_pre
    softplus = jnp.maximum(z, 0.0) + jnp.log(1.0 + jnp.exp(-jnp.abs(z)))
    w = -softplus - 0.5
    r_ref[...] = r.astype(r_ref.dtype)
    k_ref[...] = (k * (1.0 + (a - 1.0) * ka_ref[...])).astype(k_ref.dtype)
    v_ref[...] = v.astype(v_ref.dtype)
    kkr_ref[...] = (k * kk_ref[...]).astype(kkr_ref.dtype)
    a_ref[...] = a.astype(a_ref.dtype)
    lw_ref[...] = -jnp.exp(w)
    g_ref[...] = g.astype(g_ref.dtype)


def _rw_prep(p_rkv, p_lora, mu_rkv, mu_lora, w0, a0, k_k, k_a, wup, aup, gup, seq, tm=256):
    t = p_rkv.shape[0]
    tm = min(tm, seq)
    n3 = 3 * D_RWKV
    row = lambda n: pl.BlockSpec((1, n), lambda i: (0, 0))
    full = lambda a, b: pl.BlockSpec((a, b), lambda i: (0, 0))
    prev = lambda n: pl.BlockSpec((16, n), lambda i: (jnp.maximum(i * (tm // 16) - 1, 0), 0))
    out = lambda dt: jax.ShapeDtypeStruct((t, D_RWKV), dt)
    ospec = pl.BlockSpec((tm, D_RWKV), lambda i: (i, 0))
    return pl.pallas_call(
        functools.partial(_rw_prep_kernel, tiles_per_seq=seq // tm),
        grid=(t // tm,),
        in_specs=[pl.BlockSpec((tm, n3), lambda i: (i, 0)), prev(n3),
                  pl.BlockSpec((tm, RW_LORA_PAD), lambda i: (i, 0)), prev(RW_LORA_PAD),
                  row(n3), row(RW_LORA_PAD), row(D_RWKV), row(D_RWKV), row(D_RWKV), row(D_RWKV),
                  full(RW_LORA_PAD, D_RWKV), full(RW_LORA_PAD, D_RWKV), full(RW_LORA_PAD, D_RWKV)],
        out_specs=[ospec] * 7,
        out_shape=[out(BF16)] * 5 + [out(F32), out(BF16)],
        compiler_params=_cparams(("parallel",)),
    )(p_rkv, p_rkv, p_lora, p_lora, mu_rkv, mu_lora, w0, a0, k_k, k_a, wup, aup, gup)


def _rw_chunk_kernel(r_ref, k_ref, v_ref, kkr_ref, a_ref, lw_ref, rk_ref, ones_ref,
                     q_ref, oi_ref, bonus_ref, m_ref, n_ref):
    c = RW_CHUNK
    hd = RW_HEAD
    heads = range(r_ref.shape[1] // hd)
    ri = lax.broadcasted_iota(jnp.int32, (c, c), 0)
    ci = lax.broadcasted_iota(jnp.int32, (c, c), 1)
    tri_incl = (ci <= ri)
    tri_strict = (ci < ri)
    ltri = tri_incl.astype(BF16)
    eye = (ci == ri).astype(F32)
    lw = lw_ref[...]
    lw_hi = lw.astype(BF16)
    lw_lo = (lw - lw_hi.astype(F32)).astype(BF16)
    cl = (jnp.dot(ltri, lw_hi, preferred_element_type=F32)
          + jnp.dot(ltri, lw_lo, preferred_element_type=F32))
    cl_end = cl[c - 1:c, :]
    r = r_ref[...].astype(F32)
    k = k_ref[...].astype(F32)
    rt_d = r * jnp.exp(cl)
    e_neg_d = jnp.exp(-cl)
    kt_d = k * e_neg_d
    e_prev_d = jnp.exp(cl - lw)
    e_end_d = jnp.exp(cl_end - cl)
    kte_d = k * e_end_d
    gam_d = jnp.exp(cl_end)
    kkr_d = kkr_ref[...].astype(F32)

    nrm = jnp.sqrt(_head_sum(kkr_d * kkr_d, ones_ref))
    kk_d = kkr_d / jnp.maximum(nrm, 1e-12)
    b_d = kk_d * a_ref[...].astype(F32)
    bte_d = b_d * e_end_d
    kr_d = jnp.concatenate([kk_d * e_prev_d, rt_d], axis=0)
    kb_d = jnp.concatenate([kt_d, b_d * e_neg_d], axis=0)
    v_d = v_ref[...]
    bonus_ref[...] = _head_sum(r * k * rk_ref[0:1, :], ones_ref) * v_d.astype(F32)
    sl = [slice(h * hd, (h + 1) * hd) for h in heads]
    v = [v_d[:, s] for s in sl]
    rt = [rt_d[:, s] for s in sl]
    kkt = [kr_d[:c, s] for s in sl]
    bte = [bte_d[:, s] for s in sl]
    a4 = [_dot_nt(kr_d[:, s], kb_d[:, s]) for s in sl]
    a_kb = [jnp.where(tri_strict, a4[h][:c, c:], 0.0) for h in heads]
    a_rb = [jnp.where(tri_incl, a4[h][c:, c:], 0.0) for h in heads]
    a_kr = [jnp.concatenate([jnp.where(tri_strict, a4[h][:c, :c], 0.0),
                             jnp.where(tri_incl, a4[h][c:, :c], 0.0)], axis=0) for h in heads]
    x = [eye - a_kb[h] for h in heads]
    pw = [_dot(a_kb[h], a_kb[h]) for h in heads]
    n_sq = int(math.log2(c)) - 1
    for it in range(n_sq):
        if it + 1 < n_sq:
            xp = [_dot(jnp.concatenate([x[h], pw[h]], axis=0), pw[h]) for h in heads]
            x = [x[h] + xp[h][:c] for h in heads]
            pw = [xp[h][c:] for h in heads]
        else:
            x = [x[h] + _dot(x[h], pw[h]) for h in heads]
    av = [_dot(a_kr[h], v[h]) for h in heads]
    wu = [_dot(x[h], jnp.concatenate([kkt[h], av[h][:c]], axis=1)) for h in heads]
    rb = [_dot(a_rb[h], wu[h]) for h in heads]
    wub = [_dot_tn(wu[h], bte[h]) for h in heads]
    vk = [_dot_tn(v[h], kte_d[:, sl[h]]) for h in heads]
    ji = lax.broadcasted_iota(jnp.int32, (hd, hd), 0)
    jo = lax.broadcasted_iota(jnp.int32, (hd, hd), 1)
    for h in heads:
        m_ref[0, h] = jnp.where(ji == jo, gam_d[:, sl[h]], 0.0) - wub[h][:hd]
        n_ref[0, h] = vk[h] - wub[h][hd:]
    q_ref[...] = jnp.concatenate([rt[h] - rb[h][:, :hd] for h in heads], axis=1)
    oi_ref[...] = jnp.concatenate([av[h][c:] - rb[h][:, hd:] for h in heads], axis=1)


def _rw_chunks(r, k, v, kkr, a, lw, r_k, hp=RW_HEADS):
    t = r.shape[0]
    c = RW_CHUNK
    nch = t // c
    wdt = hp * RW_HEAD
    blk = pl.BlockSpec((c, wdt), lambda i, j: (i, j))
    mat = pl.BlockSpec((1, hp, RW_HEAD, RW_HEAD), lambda i, j: (i, j, 0, 0))
    tok = jax.ShapeDtypeStruct((t, D_RWKV), F32)
    mshape = jax.ShapeDtypeStruct((nch, RW_HEADS, RW_HEAD, RW_HEAD), F32)
    return pl.pallas_call(
        _rw_chunk_kernel,
        grid=(nch, RW_HEADS // hp),
        in_specs=[blk] * 6 + [pl.BlockSpec((1, wdt), lambda i, j: (0, j)),
                  pl.BlockSpec((4 * RW_HEAD, 4 * RW_HEAD), lambda i, j: (0, 0))],
        out_specs=[blk, blk, blk, mat, mat],
        out_shape=[tok, tok, tok, mshape, mshape],
        compiler_params=_cparams(("parallel", "parallel")),
    )(r, k, v, kkr, a, lw, r_k, _head_ones())


def _rw_seq_kernel(q_ref, oi_ref, bonus_ref, g_ref, m_ref, n_ref, gg_ref, gb_ref, ones_ref,
                   o_ref, st_ref):
    @pl.when(pl.program_id(0) == 0)
    def _():
        st_ref[...] = jnp.zeros_like(st_ref)

    hd = RW_HEAD
    sl = [slice(h * hd, (h + 1) * hd) for h in range(RW_HEADS)]
    streams = [(b, h) for b in range(q_ref.shape[0]) for h in range(RW_HEADS)]
    s = {bh: st_ref[bh[0], bh[1]] for bh in streams}
    q = [q_ref[b] for b in range(q_ref.shape[0])]
    o = {(b, h): _dot_nt(q[b][:, sl[h]], s[b, h]) for b, h in streams}
    for b, h in streams:
        st_ref[b, h] = _dot(s[b, h], m_ref[b, 0, h]) + n_ref[b, 0, h]
    inv_n = 1.0 / hd
    for b in range(q_ref.shape[0]):
        ob = jnp.concatenate([o[b, h] for h in range(RW_HEADS)], axis=1) + oi_ref[b]
        ctr = ob - _head_sum(ob, ones_ref) * inv_n
        var = _head_sum(ctr * ctr, ones_ref) * inv_n
        y = ctr * lax.rsqrt(var + RW_GN_EPS) * gg_ref[...] + gb_ref[...] + bonus_ref[b]
        o_ref[b] = (y * g_ref[b].astype(F32)).astype(o_ref.dtype)


def _rw_seq(q, oi, bonus, g, m, n, gn_g, gn_b, bsz, seq):
    c = RW_CHUNK
    nch = seq // c
    r3 = lambda x: x.reshape(bsz, seq, D_RWKV)
    r5 = lambda x: x.reshape(bsz, nch, RW_HEADS, RW_HEAD, RW_HEAD)
    tok = pl.BlockSpec((bsz, c, D_RWKV), lambda i: (0, i, 0))
    mat = pl.BlockSpec((bsz, 1, RW_HEADS, RW_HEAD, RW_HEAD), lambda i: (0, i, 0, 0, 0))
    row = pl.BlockSpec((1, D_RWKV), lambda i: (0, 0))
    y = pl.pallas_call(
        _rw_seq_kernel,
        grid=(nch,),
        in_specs=[tok, tok, tok, tok, mat, mat, row, row,
                  pl.BlockSpec((4 * RW_HEAD, 4 * RW_HEAD), lambda i: (0, 0))],
        out_specs=tok,
        out_shape=jax.ShapeDtypeStruct((bsz, seq, D_RWKV), BF16),
        scratch_shapes=[pltpu.VMEM((bsz, RW_HEADS, RW_HEAD, RW_HEAD), F32)],
        compiler_params=_cparams(("arbitrary",)),
    )(r3(q), r3(oi), r3(bonus), r3(g), r5(m), r5(n), gn_g, gn_b, _head_ones())
    return y.reshape(bsz * seq, D_RWKV)


def _layer_norm(x, g, b):
    mu = jnp.mean(x, axis=-1, keepdims=True)
    var = jnp.mean(jnp.square(x - mu), axis=-1, keepdims=True)
    return (x - mu) * lax.rsqrt(var + LN_EPS) * g + b


def _outproj_kernel(ys_ref, yr_ref, w1_ref, w2_ref, x_ref, g_ref, b_ref, h_ref, hb_ref, ht_ref, isc_ref):
    mix = (jnp.dot(ys_ref[...], w1_ref[...], preferred_element_type=F32)
           + jnp.dot(yr_ref[...], w2_ref[...], preferred_element_type=F32))
    h = _layer_norm(DEEPNORM_ALPHA * x_ref[...] + mix, g_ref[...], b_ref[...])
    h_ref[...] = h
    hb_ref[...] = h.astype(BF16)
    ht = h.T
    sx = _pow2_scale(jnp.max(jnp.abs(ht), axis=0, keepdims=True))
    ht_ref[...] = (ht * sx).astype(F8)
    isc_ref[...] = 1.0 / sx


def _outproj(ys, yr, w_out, x, g, b, tm=512):
    t = x.shape[0]
    tm = min(tm, t)
    half = pl.BlockSpec((tm, D_SSM), lambda i: (i, 0))
    full = pl.BlockSpec((tm, D_MODEL), lambda i: (i, 0))
    row = pl.BlockSpec((1, D_MODEL), lambda i: (0, 0))
    return pl.pallas_call(
        _outproj_kernel,
        grid=(t // tm,),
        in_specs=[half, half,
                  pl.BlockSpec((D_SSM, D_MODEL), lambda i: (0, 0)),
                  pl.BlockSpec((D_RWKV, D_MODEL), lambda i: (1, 0)),
                  full, row, row],
        out_specs=[full, full, pl.BlockSpec((D_MODEL, tm), lambda i: (0, i)),
                   pl.BlockSpec((1, tm), lambda i: (0, i))],
        out_shape=[jax.ShapeDtypeStruct((t, D_MODEL), F32),
                   jax.ShapeDtypeStruct((t, D_MODEL), BF16),
                   jax.ShapeDtypeStruct((D_MODEL, t), F8),
                   jax.ShapeDtypeStruct((1, t), F32)],
        compiler_params=_cparams(("parallel",)),
    )(ys, yr, w_out, w_out, x, g, b)


def _peer_scores_kernel(h_ref, wq_ref, keys_ref, s_ref):
    q = jnp.dot(h_ref[...], wq_ref[...], preferred_element_type=F32)
    for blk in range(2 * PEER_HEADS):
        qb = q[:, blk * PEER_HALF:(blk + 1) * PEER_HALF]
        s_ref[blk] = _dot_nt(keys_ref[blk], qb)


def _peer_scores(hb, wq, keys, tm=512):
    t = hb.shape[0]
    tm = min(tm, t)
    nb = 2 * PEER_HEADS
    return pl.pallas_call(
        _peer_scores_kernel,
        grid=(t // tm,),
        in_specs=[pl.BlockSpec((tm, D_MODEL), lambda i: (i, 0)),
                  pl.BlockSpec((D_MODEL, D_MODEL), lambda i: (0, 0)),
                  pl.BlockSpec((nb, PEER_NKEYS, PEER_HALF), lambda i: (0, 0, 0))],
        out_specs=pl.BlockSpec((nb, PEER_NKEYS, tm), lambda i: (0, 0, i)),
        out_shape=jax.ShapeDtypeStruct((nb, PEER_NKEYS, t), F32),
        compiler_params=_cparams(("parallel",)),
    )(hb, wq, keys)


NO_RANK = 127.0


def _extract_top(x, n, want_rank=False):
    vals = []
    rank = jnp.full(x.shape, NO_RANK, F32) if want_rank else None
    for i in range(n):
        m = jnp.max(x, axis=0, keepdims=True)
        vals.append(m)
        hit = x == m
        if want_rank:
            rank = jnp.where(hit, float(i), rank)
        x = jnp.where(hit, NEG_INF, x)
    return vals, rank


def _peer_topk_kernel(s_ref, ne_ref, rank2_ref, e2_ref):
    k = PEER_TOPK

    def per_head(h, carry):
        s1 = s_ref[2 * h]
        s2 = s_ref[2 * h + 1]
        ta, _ = _extract_top(s1, k)
        tb, rank2 = _extract_top(s2, k, want_rank=True)
        tbs = jnp.concatenate(tb, axis=0)
        cands = [ta[i] + tbs[0:k // (i + 1), :] for i in range(k)]
        n_c = sum(k // (i + 1) for i in range(k))
        pad = (-n_c) % 8
        padded = cands + ([jnp.full((pad, s1.shape[1]), NEG_INF, F32)] if pad else [])
        best, _ = _extract_top(jnp.concatenate(padded, axis=0), k)
        m0 = best[0]
        tau = best[k - 1]
        z = jnp.zeros_like(m0)
        for bv in best:
            z = z + jnp.exp(bv - m0)
        nsel = jnp.zeros_like(s1)
        for i in range(k):
            cnt = jnp.sum(jnp.where(cands[i] >= tau, 1.0, 0.0), axis=0, keepdims=True)
            nsel = jnp.where(s1 == ta[i], cnt, nsel)
        ne_ref[h, 0] = nsel
        ne_ref[h, 1] = jnp.exp(s1 - ta[0]) / z
        rank2_ref[h] = rank2.astype(BF16)
        e2_ref[h] = jnp.exp(s2 - tb[0]).astype(BF16)
        return carry

    lax.fori_loop(0, PEER_HEADS, per_head, 0)


def _peer_topk(scores, tt=512):
    nb, nk, t = scores.shape
    tt = min(tt, t)
    big = pl.BlockSpec((PEER_HEADS, nk, tt), lambda i: (0, 0, i))
    f32 = jax.ShapeDtypeStruct((PEER_HEADS, 2, nk, t), F32)
    b16 = jax.ShapeDtypeStruct((PEER_HEADS, nk, t), BF16)
    return pl.pallas_call(
        _peer_topk_kernel,
        grid=(t // tt,),
        in_specs=[pl.BlockSpec((nb, nk, tt), lambda i: (0, 0, i))],
        out_specs=[pl.BlockSpec((PEER_HEADS, 2, nk, tt), lambda i: (0, 0, 0, i)), big, big],
        out_shape=[f32, b16, b16],
        compiler_params=_cparams(("parallel",)),
    )(scores)


def _peer_dense_kernel(ht_ref, asc_ref, u0_ref, ub_ref, un_ref, v_ref, ne0_ref, nec_ref, nen_ref,
                       rank2_ref, e2_ref, h1_ref, lg_ref, lb_ref,
                       o_ref, hsa_ref, hsb_ref, *, q, nj):
    j = pl.program_id(1)
    nk = PEER_NKEYS
    tm = ht_ref.shape[1]
    half = q * nk
    ht = ht_ref[...]

    def gate_part(gate, heads, ne_ref, row, live):
        for h in heads:
            nsel = ne_ref[h, 0, 0, row:row + 1, :]
            nsel = nsel if live is None else nsel * live
            nsel = jnp.concatenate([jnp.broadcast_to(nsel, (16, tm)).astype(BF16)] * (nk // 16), axis=0)
            e1 = jnp.concatenate(
                [jnp.broadcast_to(ne_ref[h, 1, 0, row:row + 1, :], (16, tm)).astype(BF16)] * (nk // 16), axis=0)
            gate = gate + e1 * jnp.where(rank2_ref[h] < nsel, e2_ref[h], 0.0)
        return gate

    def hidden_block(u_ref, ne_ref, row, qi, live, dst_ref, values=None):
        gate = gate_part(jnp.zeros((nk, tm), BF16), range(0, PEER_HEADS // 2), ne_ref, row, live)
        if values is not None:
            values()
        gate = gate_part(gate, range(PEER_HEADS // 2, PEER_HEADS), ne_ref, row, live)
        act = jnp.dot(u_ref[qi * nk:(qi + 1) * nk, :], ht, preferred_element_type=F32) * asc_ref[...]
        c0 = math.sqrt(2.0 / math.pi)
        inner = act * (c0 + (c0 * 0.044715) * (act * act))
        half_act = 0.5 * act.astype(BF16)
        hid = (half_act + half_act * jnp.tanh(inner.astype(BF16))) * gate
        dst_ref[:, qi * nk:(qi + 1) * nk] = hid.T

    @pl.when(j == 0)
    def _():
        o_ref[...] = jnp.zeros_like(o_ref)
        for qi in range(q):
            hidden_block(u0_ref, ne0_ref, qi, qi, None, hsa_ref)

    def phase(src_ref, v_rows, u_ref, ne_ref, row0, live, dst_ref):
        wv = D_MODEL // q
        for qi in range(q):
            cols = slice(qi * wv, (qi + 1) * wv)

            def values():
                o_ref[:, cols] += jnp.dot(src_ref[...], v_ref[v_rows, cols], preferred_element_type=F32)

            hidden_block(u_ref, ne_ref, row0 + qi, qi, live, dst_ref, values)

    phase(hsa_ref, slice(0, half), ub_ref, nec_ref, q, None, hsb_ref)
    more = jnp.where(j + 1 < nj, 1.0, 0.0)
    phase(hsb_ref, slice(half, 2 * half), un_ref, nen_ref, 0, more, hsa_ref)

    @pl.when(j == nj - 1)
    def _():
        o_ref[...] = _layer_norm(DEEPNORM_ALPHA * h1_ref[...] + o_ref[...], lg_ref[...], lb_ref[...])


def _peer_dense(ht, act_scale, u_tab, v_tab, ne, rank2, e2, h1, ln_g, ln_b, tm=512, te=2048):
    t = ht.shape[1]
    tm = min(tm, t)
    half = te // 2
    q = half // PEER_NKEYS
    ng = PEER_NKEYS // (2 * q)
    nj = PEER_EXPERTS // te
    ne = ne.reshape(PEER_HEADS, 2, ng, 2 * q, t)
    nxt = lambda j: jnp.minimum(j + 1, nj - 1)
    u_spec = lambda f: pl.BlockSpec((half, D_MODEL), lambda i, j: (f(j), 0))
    row_spec = lambda f: pl.BlockSpec((PEER_HEADS, 2, 1, 2 * q, tm), lambda i, j: (0, 0, f(j), 0, i))
    key_spec = pl.BlockSpec((PEER_HEADS, PEER_NKEYS, tm), lambda i, j: (0, 0, i))
    vec_spec = pl.BlockSpec((1, D_MODEL), lambda i, j: (0, 0))
    first, cur = (lambda j: 0), (lambda j: j)
    return pl.pallas_call(
        functools.partial(_peer_dense_kernel, q=q, nj=nj),
        grid=(t // tm, nj),
        in_specs=[pl.BlockSpec((D_MODEL, tm), lambda i, j: (0, i)),
                  pl.BlockSpec((1, tm), lambda i, j: (0, i)),
                  u_spec(first), u_spec(lambda j: 2 * j + 1), u_spec(lambda j: 2 * nxt(j)),
                  pl.BlockSpec((te, D_MODEL), lambda i, j: (j, 0)),
                  row_spec(first), row_spec(cur), row_spec(nxt), key_spec, key_spec,
                  pl.BlockSpec((tm, D_MODEL), lambda i, j: (i, 0)), vec_spec, vec_spec],
        out_specs=pl.BlockSpec((tm, D_MODEL), lambda i, j: (i, 0)),
        out_shape=jax.ShapeDtypeStruct((t, D_MODEL), F32),
        scratch_shapes=[pltpu.VMEM((tm, half), BF16), pltpu.VMEM((tm, half), BF16)],
        compiler_params=_cparams(("parallel", "arbitrary")),
    )(ht, act_scale, u_tab, u_tab, u_tab, v_tab, ne, ne, ne, rank2, e2, h1, ln_g, ln_b)


def _layer(h, bsz, seq, w_in, s5, rw, w_out, ln1, peer, ln2):
    (lam_re, lam_im, log_step, b_re, b_im, c_re, c_im, d_skip, w_glu, b_glu) = s5
    (mu, w0, w_up, a0, a_up, g_up, k_k, k_a, r_k, gn_g, gn_b) = rw
    (w_q, keys1, keys2, u_tab, v_tab) = peer
    hb = h.astype(BF16)
    n_rkv = 3 * D_RWKV
    w_in_b = w_in.astype(BF16)
    pad = RW_LORA_PAD - RW_LORA

    u = _matmul(hb, w_in_b, F32, 1024, 1024, 0, D_SSM)
    p_rkv = _matmul(hb, w_in_b, BF16, 1024, 1024, D_SSM, n_rkv)
    p_lora = _matmul(hb, jnp.pad(w_in_b[:, D_SSM + n_rkv:], ((0, 0), (0, pad))), F32, 512, RW_LORA_PAD)

    yg = _s5_scan(u, _s5_params(lam_re, lam_im, log_step, b_re, b_im, c_re, c_im, d_skip,
                                seq // S5_CHUNK), bsz, seq)
    y_ssm = _glu(yg, w_glu.astype(BF16), b_glu[None, :])

    mu_rkv = mu[None, :n_rkv]
    mu_lora = jnp.pad(mu[n_rkv:], (0, pad))[None, :]
    zrow = lambda n: jnp.zeros((n, D_RWKV), F32)
    wup = jnp.concatenate([w_up, zrow(RW_LORA_PAD - RW_W_LORA)], axis=0).astype(BF16)
    aup = jnp.concatenate([zrow(RW_W_LORA), a_up, zrow(RW_LORA_PAD - RW_W_LORA - RW_A_LORA)],
                          axis=0).astype(BF16)
    gup = jnp.concatenate([zrow(RW_W_LORA + RW_A_LORA), g_up, zrow(pad)], axis=0).astype(BF16)
    r, k, v, kkr, a, lw, g = _rw_prep(p_rkv, p_lora, mu_rkv, mu_lora, w0[None], a0[None],
                                      k_k[None], k_a[None], wup, aup, gup, seq)
    q, oi, bonus, m, n = _rw_chunks(r, k, v, kkr, a, lw, r_k.reshape(1, D_RWKV))
    y_rw = _rw_seq(q, oi, bonus, g, m, n, gn_g[None], gn_b[None], bsz, seq)

    h1, h1b, h1t, inv_sx = _outproj(y_ssm, y_rw, w_out.astype(BF16), h, ln1[0][None], ln1[1][None])

    keys = jnp.stack([keys1, keys2], axis=1).reshape(2 * PEER_HEADS, PEER_NKEYS, PEER_HALF)
    scores = _peer_scores(h1b, w_q.astype(BF16), keys.astype(BF16))
    ne, rank2, e2 = _peer_topk(scores)
    su = _pow2_scale(jnp.max(jnp.abs(u_tab)))
    return _peer_dense(h1t, inv_sx / su, (u_tab * su).astype(F8), v_tab.astype(BF16), ne, rank2, e2,
                       h1, ln2[0][None], ln2[1][None])


def kernel(x, w_in, s5_lam_re, s5_lam_im, s5_log_step, s5_b_re, s5_b_im, s5_c_re, s5_c_im, s5_d, s5_w_glu, s5_b_glu, rw_mu, rw_w0, rw_w_up, rw_a0, rw_a_up, rw_g_up, rw_k_k, rw_k_a, rw_r_k, rw_gn_g, rw_gn_b, w_out, ln1_g, ln1_b, peer_w_q, peer_keys1, peer_keys2, peer_u, peer_v, ln2_g, ln2_b):
    bsz, seq, dim = x.shape
    h = x.reshape(bsz * seq, dim)
    for l in range(w_in.shape[0]):
        h = _layer(
            h, bsz, seq, w_in[l],
            (s5_lam_re[l], s5_lam_im[l], s5_log_step[l], s5_b_re[l], s5_b_im[l], s5_c_re[l],
             s5_c_im[l], s5_d[l], s5_w_glu[l], s5_b_glu[l]),
            (rw_mu[l], rw_w0[l], rw_w_up[l], rw_a0[l], rw_a_up[l], rw_g_up[l], rw_k_k[l],
             rw_k_a[l], rw_r_k[l], rw_gn_g[l], rw_gn_b[l]),
            w_out[l], (ln1_g[l], ln1_b[l]),
            (peer_w_q[l], peer_keys1[l], peer_keys2[l], peer_u[l], peer_v[l]),
            (ln2_g[l], ln2_b[l]))
    return h.reshape(bsz, seq, dim)
```

```python
import functools
import math

import jax
import jax.numpy as jnp
from jax import lax
from jax.experimental import pallas as pl
from jax.experimental.pallas import tpu as pltpu

F32 = jnp.float32
BF16 = jnp.bfloat16
F8 = jnp.float8_e4m3fn
F8_TARGET = 224.0

D_MODEL = 2048
D_SSM = 1024
D_RWKV = 1024
S5_CH = 16
S5_GROUPS = D_SSM // S5_CH
S5_STATE = 64
S5_CHUNK = 16
RW_HEAD = 64
RW_HEADS = D_RWKV // RW_HEAD
RW_W_LORA = 64
RW_A_LORA = 64
RW_G_LORA = 160
RW_LORA = RW_W_LORA + RW_A_LORA + RW_G_LORA
RW_LORA_PAD = 384
RW_CHUNK = 64
PEER_HEADS = 8
PEER_NKEYS = 128
PEER_EXPERTS = PEER_NKEYS * PEER_NKEYS
PEER_HALF = 128
PEER_TOPK = 16
DEPTH = 1
DEEPNORM_ALPHA = (2.0 * DEPTH) ** 0.25
LN_EPS = 1e-5
RW_GN_EPS = 64e-5
NEG_INF = float("-inf")

VMEM_LIMIT = 60 * 1024 * 1024


def _cparams(sem):
    return pltpu.CompilerParams(dimension_semantics=sem, vmem_limit_bytes=VMEM_LIMIT)


def _dot(a, b):
    return jnp.dot(a.astype(BF16), b.astype(BF16), preferred_element_type=F32)


def _dot_nt(a, b):
    return lax.dot_general(a.astype(BF16), b.astype(BF16), (((1,), (1,)), ((), ())),
                           preferred_element_type=F32)


def _dot_tn(a, b):
    return lax.dot_general(a.astype(BF16), b.astype(BF16), (((0,), (0,)), ((), ())),
                           preferred_element_type=F32)


def _gelu(x):
    c = math.sqrt(2.0 / math.pi)
    return 0.5 * x * (1.0 + jnp.tanh(c * (x + 0.044715 * (x * x * x))))


def _sigmoid(x):
    return 1.0 / (1.0 + jnp.exp(-x))


def _head_sum(x, ones_ref):
    x_hi = x.astype(BF16)
    x_lo = (x - x_hi.astype(F32)).astype(BF16)
    w = ones_ref.shape[0]
    return jnp.concatenate(
        [jnp.dot(x_hi[:, i:i + w], ones_ref[...], preferred_element_type=F32)
         + jnp.dot(x_lo[:, i:i + w], ones_ref[...], preferred_element_type=F32)
         for i in range(0, x.shape[1], w)], axis=1)


def _pow2_scale(amax):
    return jnp.exp2(jnp.floor(jnp.log2(F8_TARGET / jnp.maximum(amax, 1e-30))))


def _head_ones():
    hid = jnp.arange(4 * RW_HEAD) // RW_HEAD
    return (hid[:, None] == hid[None, :]).astype(BF16)


def _mm_kernel(a_ref, b_ref, o_ref):
    o_ref[...] = jnp.dot(a_ref[...], b_ref[...],
                         preferred_element_type=F32).astype(o_ref.dtype)


def _matmul(a, b, out_dtype, tm, tn, col0=0, n=None):
    m, k = a.shape
    n = b.shape[1] if n is None else n
    tm = min(tm, m)
    assert col0 % tn == 0 and n % tn == 0
    return pl.pallas_call(
        _mm_kernel,
        grid=(m // tm, n // tn),
        in_specs=[pl.BlockSpec((tm, k), lambda i, j: (i, 0)),
                  pl.BlockSpec((k, tn), lambda i, j: (0, j + col0 // tn))],
        out_specs=pl.BlockSpec((tm, tn), lambda i, j: (i, j)),
        out_shape=jax.ShapeDtypeStruct((m, n), out_dtype),
        compiler_params=_cparams(("parallel", "parallel")),
    )(a, b)


def _s5_kernel(u_ref, mt_ref, bc_ref, cc_ref, ar_ref, ai_ref, d_ref, o_ref, *, nc):
    c = S5_CHUNK
    rows = u_ref.shape[0] // c
    groups = u_ref.shape[1] // S5_CH
    xs = [u_ref[pl.ds(tl, rows, stride=c), :] for tl in range(c)]
    cidx = lax.broadcasted_iota(jnp.int32, (rows, 2 * S5_STATE), 0) & (nc - 1)
    ys = []
    for g in range(groups):
        lanes = slice(g * S5_CH, (g + 1) * S5_CH)
        u = jnp.concatenate([x[:, lanes] for x in xs], axis=1).astype(BF16)
        y = jnp.dot(u, mt_ref[g], preferred_element_type=F32)
        s = jnp.dot(u, bc_ref[g], preferred_element_type=F32)
        d, k = 1, 0
        while d < nc:
            sh = jnp.where(cidx >= d, pltpu.roll(s, d, axis=0), 0.0)
            shs = pltpu.roll(sh, S5_STATE, axis=1)
            s = s + sh * ar_ref[g, k:k + 1, :] + shs * ai_ref[g, k:k + 1, :]
            d, k = d * 2, k + 1
        sp = jnp.where(cidx >= 1, pltpu.roll(s, 1, axis=0), 0.0)
        ys.append(y + jnp.dot(sp.astype(BF16), cc_ref[g], preferred_element_type=F32))
    for tl in range(c):
        y = jnp.concatenate([yg[:, tl * S5_CH:(tl + 1) * S5_CH] for yg in ys], axis=1)
        o_ref[pl.ds(tl, rows, stride=c), :] = _gelu(y + d_ref[...] * xs[tl])


def _s5_params(lam_re, lam_im, log_step, b_re, b_im, c_re, c_im, d_skip, nc):
    hi = lax.Precision.HIGHEST
    c = S5_CHUNK
    step = jnp.exp(log_step)[:, None]
    a_re = jnp.exp(lam_re * step) * jnp.cos(lam_im * step)
    a_im = jnp.exp(lam_re * step) * jnp.sin(lam_im * step)
    den = lam_re * lam_re + lam_im * lam_im
    f_re = ((a_re - 1.0) * lam_re + a_im * lam_im) / den
    f_im = (a_im * lam_re - (a_re - 1.0) * lam_im) / den
    bb_re = f_re[..., None] * b_re - f_im[..., None] * b_im
    bb_im = f_re[..., None] * b_im + f_im[..., None] * b_re

    def power(j):
        jj = j[None, :, None]
        mag = jnp.exp(lam_re[:, None, :] * step[:, None, :] * jj)
        ang = lam_im[:, None, :] * step[:, None, :] * jj
        return mag * jnp.cos(ang), mag * jnp.sin(ang)

    pw_re, pw_im = power(jnp.arange(c + 1, dtype=F32))
    ce_re = c_re[:, None] * pw_re[:, :, None, :] - c_im[:, None] * pw_im[:, :, None, :]
    ce_im = c_re[:, None] * pw_im[:, :, None, :] + c_im[:, None] * pw_re[:, :, None, :]
    kern = jnp.einsum('gjop,gpi->gjoi', jnp.concatenate([ce_re[:, :c], -ce_im[:, :c]], axis=-1),
                      jnp.concatenate([bb_re, bb_im], axis=1), precision=hi)
    t_idx = jnp.arange(c)
    lag = t_idx[None, :] - t_idx[:, None]
    place = (lag[:, :, None] == t_idx[None, None, :]).astype(F32)
    g = lam_re.shape[0]
    mt = jnp.einsum('stj,gjoi->gsito', place, kern, precision=hi)
    mt = mt.reshape(g, c * S5_CH, c * S5_CH)
    rv_re, rv_im = pw_re[:, c - 1::-1][:, :c], pw_im[:, c - 1::-1][:, :c]
    bc_r = rv_re[:, :, None, :] * jnp.transpose(bb_re, (0, 2, 1))[:, None] \
        - rv_im[:, :, None, :] * jnp.transpose(bb_im, (0, 2, 1))[:, None]
    bc_i = rv_re[:, :, None, :] * jnp.transpose(bb_im, (0, 2, 1))[:, None] \
        + rv_im[:, :, None, :] * jnp.transpose(bb_re, (0, 2, 1))[:, None]
    bc = jnp.concatenate([bc_r, bc_i], axis=-1).reshape(g, c * S5_CH, 2 * S5_STATE)
    cc_r = jnp.transpose(ce_re[:, 1:], (0, 3, 1, 2)).reshape(g, S5_STATE, c * S5_CH)
    cc_i = -jnp.transpose(ce_im[:, 1:], (0, 3, 1, 2)).reshape(g, S5_STATE, c * S5_CH)
    cc = jnp.concatenate([cc_r, cc_i], axis=1)
    sc_re, sc_im = power(c * (2.0 ** jnp.arange(8, dtype=F32)))
    ar = jnp.concatenate([sc_re, sc_re], axis=-1)
    ai = jnp.concatenate([-sc_im, sc_im], axis=-1)
    return mt.astype(BF16), bc.astype(BF16), cc.astype(BF16), ar, ai, d_skip[None, :]


def _s5_scan(u_tok, params, bsz, seq, gpb=8):
    mt, bc, cc, ar, ai, d_row = params
    nc = seq // S5_CHUNK
    t = bsz * seq
    w = S5_CHUNK * S5_CH
    lanes = gpb * S5_CH
    tok = pl.BlockSpec((t, lanes), lambda i: (0, i))
    spec3 = lambda a, b: pl.BlockSpec((gpb, a, b), lambda i: (i, 0, 0))
    return pl.pallas_call(
        functools.partial(_s5_kernel, nc=nc),
        grid=(S5_GROUPS // gpb,),
        in_specs=[tok, spec3(w, w), spec3(w, 2 * S5_STATE), spec3(2 * S5_STATE, w),
                  spec3(8, 2 * S5_STATE), spec3(8, 2 * S5_STATE),
                  pl.BlockSpec((1, lanes), lambda i: (0, i))],
        out_specs=tok,
        out_shape=jax.ShapeDtypeStruct((t, D_SSM), F32),
        compiler_params=_cparams(("parallel",)),
    )(u_tok, mt, bc, cc, ar, ai, d_row)


def _glu_kernel(y_ref, w_ref, b_ref, o_ref):
    y = y_ref[...]
    z = jnp.dot(y.astype(BF16), w_ref[...], preferred_element_type=F32) + b_ref[...]
    o_ref[...] = (y * _sigmoid(z)).astype(o_ref.dtype)


def _glu(y, w, b, tm=512):
    m, n = y.shape
    tm = min(tm, m)
    return pl.pallas_call(
        _glu_kernel,
        grid=(m // tm,),
        in_specs=[pl.BlockSpec((tm, n), lambda i: (i, 0)),
                  pl.BlockSpec((n, n), lambda i: (0, 0)),
                  pl.BlockSpec((1, n), lambda i: (0, 0))],
        out_specs=pl.BlockSpec((tm, n), lambda i: (i, 0)),
        out_shape=jax.ShapeDtypeStruct((m, n), BF16),
        compiler_params=_cparams(("parallel",)),
    )(y, w, b)


def _rw_prep_kernel(p_ref, pp_ref, l_ref, lp_ref, mu_ref, mul_ref, w0_ref, a0_ref,
                    kk_ref, ka_ref, wup_ref, aup_ref, gup_ref,
                    r_ref, k_ref, v_ref, kkr_ref, a_ref, lw_ref, g_ref, *, tiles_per_seq):
    first = (pl.program_id(0) % tiles_per_seq) == 0

    def shifted(cur_ref, prev_ref, mu):
        cur = cur_ref[...].astype(F32)
        prev_row = jnp.where(first, 0.0, prev_ref[15:16, :].astype(F32))
        row = lax.broadcasted_iota(jnp.int32, cur.shape, 0)
        prev = jnp.where(row == 0, prev_row, pltpu.roll(cur, 1, axis=0))
        return cur + mu * (prev - cur)

    p = shifted(p_ref, pp_ref, mu_ref[...])
    lo = shifted(l_ref, lp_ref, mul_ref[...])
    r = p[:, :D_RWKV]
    k = p[:, D_RWKV:2 * D_RWKV]
    v = p[:, 2 * D_RWKV:]
    w_pre = w0_ref[...] + _dot(jnp.tanh(lo), wup_ref[...])
    a = _sigmoid(a0_ref[...] + _dot(lo, aup_ref[...]))
    g = _dot(_sigmoid(lo), gup_ref[...])
    z = -w_pre
    softplus = jnp.maximum(z, 0.0) + jnp.log(1.0 + jnp.exp(-jnp.abs(z)))
    w = -softplus - 0.5
    r_ref[...] = r.astype(r_ref.dtype)
    k_ref[...] = (k * (1.0 + (a - 1.0) * ka_ref[...])).astype(k_ref.dtype)
    v_ref[...] = v.astype(v_ref.dtype)
    kkr_ref[...] = (k * kk_ref[...]).astype(kkr_ref.dtype)
    a_ref[...] = a.astype(a_ref.dtype)
    lw_ref[...] = -jnp.exp(w)
    g_ref[...] = g.astype(g_ref.dtype)


def _rw_prep(p_rkv, p_lora, mu_rkv, mu_lora, w0, a0, k_k, k_a, wup, aup, gup, seq, tm=256):
    t = p_rkv.shape[0]
    tm = min(tm, seq)
    n3 = 3 * D_RWKV
    row = lambda n: pl.BlockSpec((1, n), lambda i: (0, 0))
    full = lambda a, b: pl.BlockSpec((a, b), lambda i: (0, 0))
    prev = lambda n: pl.BlockSpec((16, n), lambda i: (jnp.maximum(i * (tm // 16) - 1, 0), 0))
    out = lambda dt: jax.ShapeDtypeStruct((t, D_RWKV), dt)
    ospec = pl.BlockSpec((tm, D_RWKV), lambda i: (i, 0))
    return pl.pallas_call(
        functools.partial(_rw_prep_kernel, tiles_per_seq=seq // tm),
        grid=(t // tm,),
        in_specs=[pl.BlockSpec((tm, n3), lambda i: (i, 0)), prev(n3),
                  pl.BlockSpec((tm, RW_LORA_PAD), lambda i: (i, 0)), prev(RW_LORA_PAD),
                  row(n3), row(RW_LORA_PAD), row(D_RWKV), row(D_RWKV), row(D_RWKV), row(D_RWKV),
                  full(RW_LORA_PAD, D_RWKV), full(RW_LORA_PAD, D_RWKV), full(RW_LORA_PAD, D_RWKV)],
        out_specs=[ospec] * 7,
        out_shape=[out(BF16)] * 5 + [out(F32), out(BF16)],
        compiler_params=_cparams(("parallel",)),
    )(p_rkv, p_rkv, p_lora, p_lora, mu_rkv, mu_lora, w0, a0, k_k, k_a, wup, aup, gup)


def _rw_chunk_kernel(r_ref, k_ref, v_ref, kkr_ref, a_ref, lw_ref, rk_ref, ones_ref,
                     q_ref, oi_ref, bonus_ref, m_ref, n_ref):
    c = RW_CHUNK
    for ck in range(r_ref.shape[0] // c):
        rows = pl.ds(ck * c, c)
        one = pl.ds(ck, 1)
        _rw_chunk_body(r_ref.at[rows], k_ref.at[rows], v_ref.at[rows], kkr_ref.at[rows], a_ref.at[rows],
                       lw_ref.at[rows], rk_ref, ones_ref, q_ref.at[rows], oi_ref.at[rows],
                       bonus_ref.at[rows], m_ref.at[one], n_ref.at[one])


def _rw_chunk_body(r_ref, k_ref, v_ref, kkr_ref, a_ref, lw_ref, rk_ref, ones_ref,
                   q_ref, oi_ref, bonus_ref, m_ref, n_ref):
    c = RW_CHUNK
    hd = RW_HEAD
    heads = range(r_ref.shape[1] // hd)
    ri = lax.broadcasted_iota(jnp.int32, (c, c), 0)
    ci = lax.broadcasted_iota(jnp.int32, (c, c), 1)
    tri_incl = (ci <= ri)
    tri_strict = (ci < ri)
    ltri = tri_incl.astype(BF16)
    eye = (ci == ri).astype(F32)
    lw = lw_ref[...]
    lw_hi = lw.astype(BF16)
    lw_lo = (lw - lw_hi.astype(F32)).astype(BF16)
    cl = (jnp.dot(ltri, lw_hi, preferred_element_type=F32)
          + jnp.dot(ltri, lw_lo, preferred_element_type=F32))
    cl_end = cl[c - 1:c, :]
    r = r_ref[...].astype(F32)
    k = k_ref[...].astype(F32)
    rt_d = r * jnp.exp(cl)
    e_neg_d = jnp.exp(-cl)
    kt_d = k * e_neg_d
    e_prev_d = jnp.exp(cl - lw)
    e_end_d = jnp.exp(cl_end - cl)
    kte_d = k * e_end_d
    gam_d = jnp.exp(cl_end)
    kkr_d = kkr_ref[...].astype(F32)

    nrm = jnp.sqrt(_head_sum(kkr_d * kkr_d, ones_ref))
    kk_d = kkr_d / jnp.maximum(nrm, 1e-12)
    b_d = kk_d * a_ref[...].astype(F32)
    bte_d = b_d * e_end_d
    kr_d = jnp.concatenate([kk_d * e_prev_d, rt_d], axis=0)
    kb_d = jnp.concatenate([kt_d, b_d * e_neg_d], axis=0)
    v_d = v_ref[...]
    bonus_ref[...] = _head_sum(r * k * rk_ref[0:1, :], ones_ref) * v_d.astype(F32)
    sl = [slice(h * hd, (h + 1) * hd) for h in heads]
    v = [v_d[:, s] for s in sl]
    rt = [rt_d[:, s] for s in sl]
    kkt = [kr_d[:c, s] for s in sl]
    bte = [bte_d[:, s] for s in sl]
    a4 = [_dot_nt(kr_d[:, s], kb_d[:, s]) for s in sl]
    a_kb = [jnp.where(tri_strict, a4[h][:c, c:], 0.0) for h in heads]
    a_rb = [jnp.where(tri_incl, a4[h][c:, c:], 0.0) for h in heads]
    a_kr = [jnp.concatenate([jnp.where(tri_strict, a4[h][:c, :c], 0.0),
                             jnp.where(tri_incl, a4[h][c:, :c], 0.0)], axis=0) for h in heads]
    x = [eye - a_kb[h] for h in heads]
    pw = [_dot(a_kb[h], a_kb[h]) for h in heads]
    n_sq = int(math.log2(c)) - 1
    for it in range(n_sq):
        if it + 1 < n_sq:
            xp = [_dot(jnp.concatenate([x[h], pw[h]], axis=0), pw[h]) for h in heads]
            x = [x[h] + xp[h][:c] for h in heads]
            pw = [xp[h][c:] for h in heads]
        else:
            x = [x[h] + _dot(x[h], pw[h]) for h in heads]
    av = [_dot(a_kr[h], v[h]) for h in heads]
    wu = [_dot(x[h], jnp.concatenate([kkt[h], av[h][:c]], axis=1)) for h in heads]
    rb = [_dot(a_rb[h], wu[h]) for h in heads]
    wub = [_dot_tn(wu[h], bte[h]) for h in heads]
    vk = [_dot_tn(v[h], kte_d[:, sl[h]]) for h in heads]
    ji = lax.broadcasted_iota(jnp.int32, (hd, hd), 0)
    jo = lax.broadcasted_iota(jnp.int32, (hd, hd), 1)
    for h in heads:
        m_ref[0, h] = jnp.where(ji == jo, gam_d[:, sl[h]], 0.0) - wub[h][:hd]
        n_ref[0, h] = vk[h] - wub[h][hd:]
    q_ref[...] = jnp.concatenate([rt[h] - rb[h][:, :hd] for h in heads], axis=1)
    oi_ref[...] = jnp.concatenate([av[h][c:] - rb[h][:, hd:] for h in heads], axis=1)


def _rw_chunks(r, k, v, kkr, a, lw, r_k, hp=RW_HEADS, nck=2):
    t = r.shape[0]
    c = RW_CHUNK
    nch = t // c
    wdt = hp * RW_HEAD
    blk = pl.BlockSpec((nck * c, wdt), lambda i, j: (i, j))
    mat = pl.BlockSpec((nck, hp, RW_HEAD, RW_HEAD), lambda i, j: (i, j, 0, 0))
    tok = jax.ShapeDtypeStruct((t, D_RWKV), F32)
    mshape = jax.ShapeDtypeStruct((nch, RW_HEADS, RW_HEAD, RW_HEAD), F32)
    return pl.pallas_call(
        _rw_chunk_kernel,
        grid=(nch // nck, RW_HEADS // hp),
        in_specs=[blk] * 6 + [pl.BlockSpec((1, wdt), lambda i, j: (0, j)),
                  pl.BlockSpec((4 * RW_HEAD, 4 * RW_HEAD), lambda i, j: (0, 0))],
        out_specs=[blk, blk, blk, mat, mat],
        out_shape=[tok, tok, tok, mshape, mshape],
        compiler_params=_cparams(("parallel", "parallel")),
    )(r, k, v, kkr, a, lw, r_k, _head_ones())


def _rw_seq_kernel(q_ref, oi_ref, bonus_ref, g_ref, m_ref, n_ref, gg_ref, gb_ref, ones_ref,
                   o_ref, st_ref):
    @pl.when(pl.program_id(0) == 0)
    def _():
        st_ref[...] = jnp.zeros_like(st_ref)

    hd = RW_HEAD
    sl = [slice(h * hd, (h + 1) * hd) for h in range(RW_HEADS)]
    streams = [(b, h) for b in range(q_ref.shape[0]) for h in range(RW_HEADS)]
    s = {bh: st_ref[bh[0], bh[1]] for bh in streams}
    q = [q_ref[b] for b in range(q_ref.shape[0])]
    o = {(b, h): _dot_nt(q[b][:, sl[h]], s[b, h]) for b, h in streams}
    for b, h in streams:
        st_ref[b, h] = _dot(s[b, h], m_ref[b, 0, h]) + n_ref[b, 0, h]
    inv_n = 1.0 / hd
    for b in range(q_ref.shape[0]):
        ob = jnp.concatenate([o[b, h] for h in range(RW_HEADS)], axis=1) + oi_ref[b]
        ctr = ob - _head_sum(ob, ones_ref) * inv_n
        var = _head_sum(ctr * ctr, ones_ref) * inv_n
        y = ctr * lax.rsqrt(var + RW_GN_EPS) * gg_ref[...] + gb_ref[...] + bonus_ref[b]
        o_ref[b] = (y * g_ref[b].astype(F32)).astype(o_ref.dtype)


def _rw_seq(q, oi, bonus, g, m, n, gn_g, gn_b, bsz, seq):
    c = RW_CHUNK
    nch = seq // c
    r3 = lambda x: x.reshape(bsz, seq, D_RWKV)
    r5 = lambda x: x.reshape(bsz, nch, RW_HEADS, RW_HEAD, RW_HEAD)
    tok = pl.BlockSpec((bsz, c, D_RWKV), lambda i: (0, i, 0))
    mat = pl.BlockSpec((bsz, 1, RW_HEADS, RW_HEAD, RW_HEAD), lambda i: (0, i, 0, 0, 0))
    row = pl.BlockSpec((1, D_RWKV), lambda i: (0, 0))
    y = pl.pallas_call(
        _rw_seq_kernel,
        grid=(nch,),
        in_specs=[tok, tok, tok, tok, mat, mat, row, row,
                  pl.BlockSpec((4 * RW_HEAD, 4 * RW_HEAD), lambda i: (0, 0))],
        out_specs=tok,
        out_shape=jax.ShapeDtypeStruct((bsz, seq, D_RWKV), BF16),
        scratch_shapes=[pltpu.VMEM((bsz, RW_HEADS, RW_HEAD, RW_HEAD), F32)],
        compiler_params=_cparams(("arbitrary",)),
    )(r3(q), r3(oi), r3(bonus), r3(g), r5(m), r5(n), gn_g, gn_b, _head_ones())
    return y.reshape(bsz * seq, D_RWKV)


def _layer_norm(x, g, b):
    mu = jnp.mean(x, axis=-1, keepdims=True)
    var = jnp.mean(jnp.square(x - mu), axis=-1, keepdims=True)
    return (x - mu) * lax.rsqrt(var + LN_EPS) * g + b


def _outproj_kernel(ys_ref, yr_ref, w1_ref, w2_ref, x_ref, g_ref, b_ref, h_ref, hb_ref, ht_ref, isc_ref):
    mix = (jnp.dot(ys_ref[...], w1_ref[...], preferred_element_type=F32)
           + jnp.dot(yr_ref[...], w2_ref[...], preferred_element_type=F32))
    h = _layer_norm(DEEPNORM_ALPHA * x_ref[...] + mix, g_ref[...], b_ref[...])
    h_ref[...] = h
    hb_ref[...] = h.astype(BF16)
    ht = h.T
    sx = _pow2_scale(jnp.max(jnp.abs(ht), axis=0, keepdims=True))
    ht_ref[...] = (ht * sx).astype(F8)
    isc_ref[...] = 1.0 / sx


def _outproj(ys, yr, w_out, x, g, b, tm=512):
    t = x.shape[0]
    tm = min(tm, t)
    half = pl.BlockSpec((tm, D_SSM), lambda i: (i, 0))
    full = pl.BlockSpec((tm, D_MODEL), lambda i: (i, 0))
    row = pl.BlockSpec((1, D_MODEL), lambda i: (0, 0))
    return pl.pallas_call(
        _outproj_kernel,
        grid=(t // tm,),
        in_specs=[half, half,
                  pl.BlockSpec((D_SSM, D_MODEL), lambda i: (0, 0)),
                  pl.BlockSpec((D_RWKV, D_MODEL), lambda i: (1, 0)),
                  full, row, row],
        out_specs=[full, full, pl.BlockSpec((D_MODEL, tm), lambda i: (0, i)),
                   pl.BlockSpec((1, tm), lambda i: (0, i))],
        out_shape=[jax.ShapeDtypeStruct((t, D_MODEL), F32),
                   jax.ShapeDtypeStruct((t, D_MODEL), BF16),
                   jax.ShapeDtypeStruct((D_MODEL, t), F8),
                   jax.ShapeDtypeStruct((1, t), F32)],
        compiler_params=_cparams(("parallel",)),
    )(ys, yr, w_out, w_out, x, g, b)


def _peer_scores_kernel(h_ref, wq_ref, keys_ref, s_ref):
    q = jnp.dot(h_ref[...], wq_ref[...], preferred_element_type=F32)
    for blk in range(2 * PEER_HEADS):
        qb = q[:, blk * PEER_HALF:(blk + 1) * PEER_HALF]
        s_ref[blk] = _dot_nt(keys_ref[blk], qb)


def _peer_scores(hb, wq, keys, tm=512):
    t = hb.shape[0]
    tm = min(tm, t)
    nb = 2 * PEER_HEADS
    return pl.pallas_call(
        _peer_scores_kernel,
        grid=(t // tm,),
        in_specs=[pl.BlockSpec((tm, D_MODEL), lambda i: (i, 0)),
                  pl.BlockSpec((D_MODEL, D_MODEL), lambda i: (0, 0)),
                  pl.BlockSpec((nb, PEER_NKEYS, PEER_HALF), lambda i: (0, 0, 0))],
        out_specs=pl.BlockSpec((nb, PEER_NKEYS, tm), lambda i: (0, 0, i)),
        out_shape=jax.ShapeDtypeStruct((nb, PEER_NKEYS, t), F32),
        compiler_params=_cparams(("parallel",)),
    )(hb, wq, keys)


NO_RANK = 127.0


def _extract_top(x, n, want_rank=False):
    vals = []
    rank = jnp.full(x.shape, NO_RANK, F32) if want_rank else None
    for i in range(n):
        m = jnp.max(x, axis=0, keepdims=True)
        vals.append(m)
        hit = x == m
        if want_rank:
            rank = jnp.where(hit, float(i), rank)
        x = jnp.where(hit, NEG_INF, x)
    return vals, rank


def _peer_topk_kernel(s_ref, nsel_ref, e1_ref, rank2_ref, e2_ref):
    k = PEER_TOPK

    def per_head(h, carry):
        s1 = s_ref[2 * h]
        s2 = s_ref[2 * h + 1]
        ta, _ = _extract_top(s1, k)
        tb, rank2 = _extract_top(s2, k, want_rank=True)
        tbs = jnp.concatenate(tb, axis=0)
        cands = [ta[i] + tbs[0:k // (i + 1), :] for i in range(k)]
        n_c = sum(k // (i + 1) for i in range(k))
        pad = (-n_c) % 8
        padded = cands + ([jnp.full((pad, s1.shape[1]), NEG_INF, F32)] if pad else [])
        best, _ = _extract_top(jnp.concatenate(padded, axis=0), k)
        m0 = best[0]
        tau = best[k - 1]
        z = jnp.zeros_like(m0)
        for bv in best:
            z = z + jnp.exp(bv - m0)
        nsel = jnp.zeros_like(s1)
        for i in range(k):
            cnt = jnp.sum(jnp.where(cands[i] >= tau, 1.0, 0.0), axis=0, keepdims=True)
            nsel = jnp.where(s1 == ta[i], cnt, nsel)
        nsel_ref[h] = nsel
        e1_ref[h] = jnp.exp(s1 - ta[0]) / z
        rank2_ref[h] = rank2.astype(BF16)
        e2_ref[h] = jnp.exp(s2 - tb[0]).astype(BF16)
        return carry

    lax.fori_loop(0, PEER_HEADS, per_head, 0)


def _peer_topk(scores, tt=256):
    nb, nk, t = scores.shape
    tt = min(tt, t)
    big = pl.BlockSpec((PEER_HEADS, nk, tt), lambda i: (0, 0, i))
    f32 = jax.ShapeDtypeStruct((PEER_HEADS, nk, t), F32)
    b16 = jax.ShapeDtypeStruct((PEER_HEADS, nk, t), BF16)
    return pl.pallas_call(
        _peer_topk_kernel,
        grid=(t // tt,),
        in_specs=[pl.BlockSpec((nb, nk, tt), lambda i: (0, 0, i))],
        out_specs=[big, big, big, big],
        out_shape=[f32, f32, b16, b16],
        compiler_params=_cparams(("parallel",)),
    )(scores)


def _peer_dense_kernel(ht_ref, asc_ref, u0_ref, ub_ref, un_ref, v_ref, ns0_ref, e10_ref, nsc_ref, e1c_ref,
                       nsn_ref, e1n_ref, rank2_ref, e2_ref, h1_ref, lg_ref, lb_ref,
                       o_ref, hsa_ref, hsb_ref, *, q, nj):
    j = pl.program_id(1)
    nk = PEER_NKEYS
    tm = ht_ref.shape[1]
    half = q * nk
    ht = ht_ref[...]

    def gate_part(gate, heads, ns_ref, e1_ref, row, live):
        for h in heads:
            nsel = ns_ref[h, 0, row:row + 1, :]
            nsel = nsel if live is None else nsel * live
            nsel = jnp.concatenate([jnp.broadcast_to(nsel, (16, tm)).astype(BF16)] * (nk // 16), axis=0)
            e1 = jnp.concatenate(
                [jnp.broadcast_to(e1_ref[h, 0, row:row + 1, :], (16, tm)).astype(BF16)] * (nk // 16), axis=0)
            gate = gate + e1 * jnp.where(rank2_ref[h] < nsel, e2_ref[h], 0.0)
        return gate

    def hidden_block(u_ref, ns_ref, e1_ref, row, qi, live, dst_ref, values=None):
        gate = gate_part(jnp.zeros((nk, tm), BF16), range(0, PEER_HEADS // 2), ns_ref, e1_ref, row, live)
        if values is not None:
            values()
        gate = gate_part(gate, range(PEER_HEADS // 2, PEER_HEADS), ns_ref, e1_ref, row, live)
        act = jnp.dot(u_ref[qi * nk:(qi + 1) * nk, :], ht, preferred_element_type=F32) * asc_ref[...]
        c0 = math.sqrt(2.0 / math.pi)
        inner = act * (c0 + (c0 * 0.044715) * (act * act))
        half_act = 0.5 * act.astype(BF16)
        hid = (half_act + half_act * jnp.tanh(inner.astype(BF16))) * gate
        dst_ref[:, qi * nk:(qi + 1) * nk] = hid.T

    @pl.when(j == 0)
    def _():
        o_ref[...] = jnp.zeros_like(o_ref)
        for qi in range(q):
            hidden_block(u0_ref, ns0_ref, e10_ref, qi, qi, None, hsa_ref)

    def phase(src_ref, v_rows, u_ref, ns_ref, e1_ref, row0, live, dst_ref):
        wv = D_MODEL // q
        for qi in range(q):
            cols = slice(qi * wv, (qi + 1) * wv)

            def values():
                o_ref[:, cols] += jnp.dot(src_ref[...], v_ref[v_rows, cols], preferred_element_type=F32)

            hidden_block(u_ref, ns_ref, e1_ref, row0 + qi, qi, live, dst_ref, values)

    phase(hsa_ref, slice(0, half), ub_ref, nsc_ref, e1c_ref, q, None, hsb_ref)
    more = jnp.where(j + 1 < nj, 1.0, 0.0)
    phase(hsb_ref, slice(half, 2 * half), un_ref, nsn_ref, e1n_ref, 0, more, hsa_ref)

    @pl.when(j == nj - 1)
    def _():
        o_ref[...] = _layer_norm(DEEPNORM_ALPHA * h1_ref[...] + o_ref[...], lg_ref[...], lb_ref[...])


def _peer_dense(ht, act_scale, u_tab, v_tab, nsel, e1, rank2, e2, h1, ln_g, ln_b, tm=512, te=2048):
    t = ht.shape[1]
    tm = min(tm, t)
    half = te // 2
    q = half // PEER_NKEYS
    ng = PEER_NKEYS // (2 * q)
    nj = PEER_EXPERTS // te
    rows = lambda a: a.reshape(PEER_HEADS, ng, 2 * q, t)
    nxt = lambda j: jnp.minimum(j + 1, nj - 1)
    u_spec = lambda f: pl.BlockSpec((half, D_MODEL), lambda i, j: (f(j), 0))
    row_spec = lambda f: pl.BlockSpec((PEER_HEADS, 1, 2 * q, tm), lambda i, j: (0, f(j), 0, i))
    key_spec = pl.BlockSpec((PEER_HEADS, PEER_NKEYS, tm), lambda i, j: (0, 0, i))
    vec_spec = pl.BlockSpec((1, D_MODEL), lambda i, j: (0, 0))
    first, cur = (lambda j: 0), (lambda j: j)
    return pl.pallas_call(
        functools.partial(_peer_dense_kernel, q=q, nj=nj),
        grid=(t // tm, nj),
        in_specs=[pl.BlockSpec((D_MODEL, tm), lambda i, j: (0, i)),
                  pl.BlockSpec((1, tm), lambda i, j: (0, i)),
                  u_spec(first), u_spec(lambda j: 2 * j + 1), u_spec(lambda j: 2 * nxt(j)),
                  pl.BlockSpec((te, D_MODEL), lambda i, j: (j, 0)),
                  row_spec(first), row_spec(first), row_spec(cur), row_spec(cur),
                  row_spec(nxt), row_spec(nxt), key_spec, key_spec,
                  pl.BlockSpec((tm, D_MODEL), lambda i, j: (i, 0)), vec_spec, vec_spec],
        out_specs=pl.BlockSpec((tm, D_MODEL), lambda i, j: (i, 0)),
        out_shape=jax.ShapeDtypeStruct((t, D_MODEL), F32),
        scratch_shapes=[pltpu.VMEM((tm, half), BF16), pltpu.VMEM((tm, half), BF16)],
        compiler_params=_cparams(("parallel", "arbitrary")),
    )(ht, act_scale, u_tab, u_tab, u_tab, v_tab, rows(nsel), rows(e1), rows(nsel), rows(e1),
      rows(nsel), rows(e1), rank2, e2, h1, ln_g, ln_b)


def _layer(h, bsz, seq, w_in, s5, rw, w_out, ln1, peer, ln2):
    (lam_re, lam_im, log_step, b_re, b_im, c_re, c_im, d_skip, w_glu, b_glu) = s5
    (mu, w0, w_up, a0, a_up, g_up, k_k, k_a, r_k, gn_g, gn_b) = rw
    (w_q, keys1, keys2, u_tab, v_tab) = peer
    hb = h.astype(BF16)
    n_rkv = 3 * D_RWKV
    w_in_b = w_in.astype(BF16)
    pad = RW_LORA_PAD - RW_LORA

    u = _matmul(hb, w_in_b, F32, 1024, 1024, 0, D_SSM)
    p_rkv = _matmul(hb, w_in_b, BF16, 1024, 1024, D_SSM, n_rkv)
    p_lora = _matmul(hb, jnp.pad(w_in_b[:, D_SSM + n_rkv:], ((0, 0), (0, pad))), F32, 512, RW_LORA_PAD)

    yg = _s5_scan(u, _s5_params(lam_re, lam_im, log_step, b_re, b_im, c_re, c_im, d_skip,
                                seq // S5_CHUNK), bsz, seq)
    y_ssm = _glu(yg, w_glu.astype(BF16), b_glu[None, :])

    mu_rkv = mu[None, :n_rkv]
    mu_lora = jnp.pad(mu[n_rkv:], (0, pad))[None, :]
    zrow = lambda n: jnp.zeros((n, D_RWKV), F32)
    wup = jnp.concatenate([w_up, zrow(RW_LORA_PAD - RW_W_LORA)], axis=0).astype(BF16)
    aup = jnp.concatenate([zrow(RW_W_LORA), a_up, zrow(RW_LORA_PAD - RW_W_LORA - RW_A_LORA)],
                          axis=0).astype(BF16)
    gup = jnp.concatenate([zrow(RW_W_LORA + RW_A_LORA), g_up, zrow(pad)], axis=0).astype(BF16)
    r, k, v, kkr, a, lw, g = _rw_prep(p_rkv, p_lora, mu_rkv, mu_lora, w0[None], a0[None],
                                      k_k[None], k_a[None], wup, aup, gup, seq)
    q, oi, bonus, m, n = _rw_chunks(r, k, v, kkr, a, lw, r_k.reshape(1, D_RWKV))
    y_rw = _rw_seq(q, oi, bonus, g, m, n, gn_g[None], gn_b[None], bsz, seq)

    h1, h1b, h1t, inv_sx = _outproj(y_ssm, y_rw, w_out.astype(BF16), h, ln1[0][None], ln1[1][None])

    keys = jnp.stack([keys1, keys2], axis=1).reshape(2 * PEER_HEADS, PEER_NKEYS, PEER_HALF)
    scores = _peer_scores(h1b, w_q.astype(BF16), keys.astype(BF16))
    nsel, e1, rank2, e2 = _peer_topk(scores)
    su = _pow2_scale(jnp.max(jnp.abs(u_tab)))
    return _peer_dense(h1t, inv_sx / su, (u_tab * su).astype(F8), v_tab.astype(BF16), nsel, e1, rank2, e2,
                       h1, ln2[0][None], ln2[1][None])


def kernel(x, w_in, s5_lam_re, s5_lam_im, s5_log_step, s5_b_re, s5_b_im, s5_c_re, s5_c_im, s5_d, s5_w_glu, s5_b_glu, rw_mu, rw_w0, rw_w_up, rw_a0, rw_a_up, rw_g_up, rw_k_k, rw_k_a, rw_r_k, rw_gn_g, rw_gn_b, w_out, ln1_g, ln1_b, peer_w_q, peer_keys1, peer_keys2, peer_u, peer_v, ln2_g, ln2_b):
    bsz, seq, dim = x.shape
    h = x.reshape(bsz * seq, dim)
    for l in range(w_in.shape[0]):
        h = _layer(
            h, bsz, seq, w_in[l],
            (s5_lam_re[l], s5_lam_im[l], s5_log_step[l], s5_b_re[l], s5_b_im[l], s5_c_re[l],
             s5_c_im[l], s5_d[l], s5_w_glu[l], s5_b_glu[l]),
            (rw_mu[l], rw_w0[l], rw_w_up[l], rw_a0[l], rw_a_up[l], rw_g_up[l], rw_k_k[l],
             rw_k_a[l], rw_r_k[l], rw_gn_g[l], rw_gn_b[l]),
            w_out[l], (ln1_g[l], ln1_b[l]),
            (peer_w_q[l], peer_keys1[l], peer_keys2[l], peer_u[l], peer_v[l]),
            (ln2_g[l], ln2_b[l]))
    return h.reshape(bsz, seq, dim)
```
